```python
import math
import jax, jax.numpy as jnp
from jax import lax
import numpy as np

D_MODEL = 2048
BATCH = 1
SEQ = 8192
DEPTH = 2

N_MIXERS = 2
N_HYENA_LAYERS = (DEPTH + 1) // 2
N_MLSTM_LAYERS = DEPTH // 2
SHORT_CONV = 3
HY_ORDER = 2
HY_IN_DIM = (HY_ORDER + 1) * D_MODEL
HY_BANDS = 16
HY_EMB = 1 + 2 * HY_BANDS
HY_FILTER_WIDTH = 64
HY_DECAY_TARGET = 1e-2
HY_FAST_DECAY = 0.3
HY_SLOW_DECAY = 1.5
ML_HEADS = 8
ML_QK_DIM = D_MODEL // 2
ML_V_DIM = D_MODEL
ML_DQK = ML_QK_DIM // ML_HEADS
ML_DV = ML_V_DIM // ML_HEADS
ML_CHUNK = 64
ML_IN_DIM = 2 * ML_QK_DIM + 2 * ML_V_DIM + 4 * ML_HEADS
N_EXPERTS = 64
TOP_K = 6
N_GROUPS = 8
TOPK_GROUPS = 4
EXPERT_FF = 512
SHARED_FF = 512
ROUTED_SCALE = 2.5
MOE_BLOCK = 128
EPS = 1e-6

kernel_name = "hybrid_hyena_mlstm_moe_encoder"


def rms_norm(x, g):
    xf = x.astype(jnp.float32)
    y = xf * lax.rsqrt(jnp.mean(xf * xf, axis=-1, keepdims=True) + EPS)
    return (y * g.astype(jnp.float32)).astype(x.dtype)


def short_conv3(u, w, b):
    up = jnp.pad(u, ((0, 0), (1, 1), (0, 0)))
    return up[:, :-2] * w[0] + up[:, 1:-1] * w[1] + up[:, 2:] * w[2] + b


def hyena_implicit_filter(L, w0, b0, w1, b1, w2, b2, freq, wout):
    f32 = jnp.float32
    w0, b0, w1, b1, w2, b2, freq, wout = [a.astype(f32) for a in (w0, b0, w1, b1, w2, b2, freq, wout)]
    t = jnp.linspace(0.0, 1.0, L, dtype=f32)[:, None]
    ang = 2.0 * math.pi * jnp.arange(L, dtype=f32)[:, None] / L
    bands = jnp.linspace(1e-4, HY_BANDS - 1, HY_BANDS, dtype=f32)
    z = jnp.concatenate([t, jnp.cos(ang * bands), -jnp.sin(ang * bands)], axis=-1)
    a = jnp.sin(freq * (z @ w0 + b0))
    a = jnp.sin(freq * (a @ w1 + b1))
    a = jnp.sin(freq * (a @ w2 + b2))
    k = a @ wout
    D = k.shape[-1] // 2
    deltas = jnp.abs(jnp.linspace(math.log(HY_DECAY_TARGET) / HY_FAST_DECAY,
                                  math.log(HY_DECAY_TARGET) / HY_SLOW_DECAY, D, dtype=f32))
    window = jnp.exp(-t * deltas)
    k_fwd = k[:, :D] * window
    k_bwd = k[:, D:] * window
    return jnp.concatenate([k_fwd, jnp.zeros((1, D), f32), k_bwd[1:][::-1]], axis=0)


def hyena_mixer(h, w_in, conv_w, conv_b, f_w0, f_b0, f_w1, f_b1, f_w2, f_b2, f_freq, f_wout, skip, w_out):
    B, L, D = h.shape
    u = short_conv3(h @ w_in, conv_w, conv_b)
    x0, x1, v = jnp.split(u, 3, axis=-1)
    filt = hyena_implicit_filter(L, f_w0, f_b0, f_w1, f_b1, f_w2, f_b2, f_freq, f_wout)
    vg = (v * x1).astype(jnp.float32)
    spec = jnp.fft.rfft(vg, n=2 * L, axis=1) * jnp.fft.rfft(filt, axis=0)[None]
    conv = jnp.fft.irfft(spec, n=2 * L, axis=1)[:, :L]
    y = (conv + vg * skip.astype(jnp.float32)).astype(h.dtype) * x0
    return y @ w_out


def mlstm_chunkwise(q, k, v, li, lf):
    B, H, L, dk = q.shape
    dv = v.shape[-1]
    nc = L // ML_CHUNK
    q = q.reshape(B, H, nc, ML_CHUNK, dk)
    k = k.reshape(B, H, nc, ML_CHUNK, dk)
    v = v.reshape(B, H, nc, ML_CHUNK, dv)
    li = li.reshape(B, H, nc, ML_CHUNK)
    lf = lf.reshape(B, H, nc, ML_CHUNK)
    b = jnp.cumsum(lf, axis=-1)
    g = b[..., -1]
    a = g[..., None] - b + li
    m_loc = jnp.max(a, axis=-1)
    wgt = jnp.exp(a - m_loc[..., None])
    C_loc = jnp.einsum('bhncd,bhnce->bhnde', k * wgt[..., None], v)
    n_loc = jnp.einsum('bhncd,bhnc->bhnd', k, wgt)

    def step(carry, inp):
        C, n, m = carry
        C_l, n_l, m_l, g_c = inp
        m_new = jnp.maximum(g_c + m, m_l)
        s_prev = jnp.exp(g_c + m - m_new)
        s_loc = jnp.exp(m_l - m_new)
        C_new = s_prev[..., None, None] * C + s_loc[..., None, None] * C_l
        n_new = s_prev[..., None] * n + s_loc[..., None] * n_l
        return (C_new, n_new, m_new), (C, n, m)

    init = (jnp.zeros((B, H, dk, dv), jnp.float32), jnp.zeros((B, H, dk), jnp.float32),
            jnp.zeros((B, H), jnp.float32))
    xs = (jnp.moveaxis(C_loc, 2, 0), jnp.moveaxis(n_loc, 2, 0), jnp.moveaxis(m_loc, 2, 0), jnp.moveaxis(g, 2, 0))
    _, (C_st, n_st, m_st) = lax.scan(step, init, xs)
    C_st = jnp.moveaxis(C_st, 0, 2)
    n_st = jnp.moveaxis(n_st, 0, 2)
    m_st = jnp.moveaxis(m_st, 0, 2)

    mask = jnp.tril(jnp.ones((ML_CHUNK, ML_CHUNK), bool))
    Dlog = jnp.where(mask, b[..., :, None] - b[..., None, :] + li[..., None, :], -jnp.inf)
    m_inter = b + m_st[..., None]
    m_t = jnp.maximum(m_inter, jnp.max(Dlog, axis=-1))
    P = jnp.exp(Dlog - m_t[..., None]) * jnp.einsum('bhntd,bhnsd->bhnts', q, k)
    s_inter = jnp.exp(m_inter - m_t)
    num = s_inter[..., None] * jnp.einsum('bhntd,bhnde->bhnte', q, C_st) + jnp.einsum('bhnts,bhnse->bhnte', P, v)
    den = s_inter * jnp.einsum('bhntd,bhnd->bhnt', q, n_st) + jnp.sum(P, axis=-1)
    hout = num / jnp.maximum(jnp.abs(den), jnp.exp(-m_t))[..., None]
    return hout.reshape(B, H, L, dv)


def mlstm_mixer(h, w_in, conv_w, conv_b, gate_b, head_g, w_out):
    B, L, D = h.shape
    f32 = jnp.float32
    p = h @ w_in
    qk = jax.nn.silu(short_conv3(p[..., :2 * ML_QK_DIM], conv_w, conv_b))
    v = p[..., 2 * ML_QK_DIM:2 * ML_QK_DIM + ML_V_DIM]
    o = p[..., 2 * ML_QK_DIM + ML_V_DIM:2 * ML_QK_DIM + 2 * ML_V_DIM]
    gt = (p[..., 2 * ML_QK_DIM + 2 * ML_V_DIM:] + gate_b).astype(f32).transpose(0, 2, 1)

    def heads(a, dh):
        return a.reshape(B, L, ML_HEADS, dh).transpose(0, 2, 1, 3).astype(f32)

    q = heads(qk[..., :ML_QK_DIM], ML_DQK)
    k = heads(qk[..., ML_QK_DIM:], ML_DQK) / math.sqrt(ML_DQK)
    vh = heads(v, ML_DV)
    i_f, f_f, i_b, f_b = jnp.split(gt, 4, axis=1)
    h_f = mlstm_chunkwise(q, k, vh, i_f, jax.nn.log_sigmoid(f_f))
    flip = lambda a: jnp.flip(a, axis=2)
    h_b = flip(mlstm_chunkwise(flip(q), flip(k), flip(vh), flip(i_b), flip(jax.nn.log_sigmoid(f_b))))
    hs = h_f + h_b
    hs = hs * lax.rsqrt(jnp.mean(hs * hs, axis=-1, keepdims=True) + EPS)
    hs = hs.transpose(0, 2, 1, 3).reshape(B, L, ML_V_DIM) * head_g.astype(f32)
    return (hs.astype(h.dtype) * jax.nn.sigmoid(o)) @ w_out


def moe_ffn(h, router_w, router_bias, w_gate, w_up, w_down, sh_gate, sh_up, sh_down):
    B, L, D = h.shape
    t = h.reshape(B * L, D)
    T = B * L
    scores = jax.nn.sigmoid(t.astype(jnp.float32) @ router_w.astype(jnp.float32))
    sel = scores + router_bias.astype(jnp.float32)
    grp_score = jnp.sum(lax.top_k(sel.reshape(T, N_GROUPS, N_EXPERTS // N_GROUPS), 2)[0], axis=-1)
    _, gidx = lax.top_k(grp_score, TOPK_GROUPS)
    gmask = jnp.any(gidx[..., None] == jnp.arange(N_GROUPS), axis=-2)
    sel = jnp.where(jnp.repeat(gmask, N_EXPERTS // N_GROUPS, axis=-1), sel, -jnp.inf)
    _, eidx = lax.top_k(sel, TOP_K)
    wk = jnp.take_along_axis(scores, eidx, axis=-1)
    wk = wk / jnp.sum(wk, axis=-1, keepdims=True) * ROUTED_SCALE

    A = T * TOP_K
    flat_e = eidx.reshape(A)
    flat_w = wk.reshape(A).astype(t.dtype)
    flat_tok = jnp.arange(A, dtype=jnp.int32) // TOP_K
    order = jnp.argsort(flat_e)
    e_sorted = flat_e[order]
    counts = jnp.zeros(N_EXPERTS, jnp.int32).at[flat_e].add(1)
    padded = (counts + MOE_BLOCK - 1) // MOE_BLOCK * MOE_BLOCK
    pad_end = jnp.cumsum(padded)
    pad_start = pad_end - padded
    start = jnp.cumsum(counts) - counts
    dest = pad_start[e_sorted] + jnp.arange(A, dtype=jnp.int32) - start[e_sorted]
    P = (A + N_EXPERTS * (MOE_BLOCK - 1) + MOE_BLOCK - 1) // MOE_BLOCK * MOE_BLOCK
    n_blk = P // MOE_BLOCK
    slot_tok = jnp.zeros(P, jnp.int32).at[dest].set(flat_tok[order])
    slot_w = jnp.zeros(P, t.dtype).at[dest].set(flat_w[order])
    blk_e = jnp.minimum(jnp.searchsorted(pad_end, jnp.arange(n_blk, dtype=jnp.int32) * MOE_BLOCK, side='right'),
                        N_EXPERTS - 1)
    xb = t[slot_tok].reshape(n_blk, MOE_BLOCK, D)

    def expert_block(args):
        xe, e = args
        return (jax.nn.silu(xe @ w_gate[e]) * (xe @ w_up[e])) @ w_down[e]

    yb = lax.map(expert_block, (xb, blk_e)).reshape(P, D) * slot_w[:, None]
    routed = jnp.zeros_like(t).at[slot_tok].add(yb)
    shared = (jax.nn.silu(t @ sh_gate) * (t @ sh_up)) @ sh_down
    return (routed + shared).reshape(B, L, D)


def setup_inputs(seed: int = 0) -> dict:
    key = jax.random.key(seed)
    ks = iter(jax.random.split(key, 48))
    nrm = lambda shape, std: jax.random.normal(next(ks), shape, jnp.float32) * std
    D = D_MODEL
    NHY = N_HYENA_LAYERS
    NML = N_MLSTM_LAYERS
    FW = HY_FILTER_WIDTH
    i_bias = nrm((NML, 2, ML_HEADS), 0.1)
    f_bias = jnp.linspace(3.0, 6.0, ML_HEADS, dtype=jnp.float32) + nrm((NML, 2, ML_HEADS), 0.1)
    gate_b = jnp.stack([i_bias, f_bias], axis=2).reshape(NML, 4 * ML_HEADS)
    return {
        "x": nrm((BATCH, SEQ, D), 1.0),
        "c": nrm((BATCH, D), 1.0),
        "ada_w": nrm((DEPTH, D, 6 * D), 0.5 * D ** -0.5),
        "ada_b": nrm((DEPTH, 6 * D), 0.02),
        "norm_mix_g": 1.0 + nrm((DEPTH, D), 0.02),
        "norm_ffn_g": 1.0 + nrm((DEPTH, D), 0.02),
        "hy_w_in": nrm((NHY, D, HY_IN_DIM), D ** -0.5),
        "hy_conv_w": nrm((NHY, SHORT_CONV, HY_IN_DIM), SHORT_CONV ** -0.5),
        "hy_conv_b": nrm((NHY, HY_IN_DIM), 0.02),
        "hy_f_w0": nrm((NHY, HY_EMB, FW), HY_EMB ** -0.5),
        "hy_f_b0": nrm((NHY, FW), 0.02),
        "hy_f_w1": nrm((NHY, FW, FW), FW ** -0.5),
        "hy_f_b1": nrm((NHY, FW), 0.02),
        "hy_f_w2": nrm((NHY, FW, FW), FW ** -0.5),
        "hy_f_b2": nrm((NHY, FW), 0.02),
        "hy_f_freq": 1.0 + nrm((NHY, FW), 0.01),
        "hy_f_wout": nrm((NHY, FW, 2 * D), (FW * SEQ) ** -0.5),
        "hy_skip": nrm((NHY, D), 1.0),
        "hy_w_out": nrm((NHY, D, D), D ** -0.5),
        "ml_w_in": nrm((NML, D, ML_IN_DIM), D ** -0.5),
        "ml_conv_w": nrm((NML, SHORT_CONV, 2 * ML_QK_DIM), SHORT_CONV ** -0.5),
        "ml_conv_b": nrm((NML, 2 * ML_QK_DIM), 0.02),
        "ml_gate_b": gate_b,
        "ml_head_g": 1.0 + nrm((NML, ML_V_DIM), 0.02),
        "ml_w_out": nrm((NML, ML_V_DIM, D), ML_V_DIM ** -0.5),
        "moe_router_w": nrm((DEPTH, D, N_EXPERTS), D ** -0.5),
        "moe_router_bias": nrm((DEPTH, N_EXPERTS), 0.01),
        "moe_w_gate": nrm((DEPTH, N_EXPERTS, D, EXPERT_FF), D ** -0.5),
        "moe_w_up": nrm((DEPTH, N_EXPERTS, D, EXPERT_FF), D ** -0.5),
        "moe_w_down": nrm((DEPTH, N_EXPERTS, EXPERT_FF, D), EXPERT_FF ** -0.5),
        "sh_w_gate": nrm((DEPTH, D, SHARED_FF), D ** -0.5),
        "sh_w_up": nrm((DEPTH, D, SHARED_FF), D ** -0.5),
        "sh_w_down": nrm((DEPTH, SHARED_FF, D), SHARED_FF ** -0.5),
        "final_g": 1.0 + nrm((D,), 0.02),
    }


def reference(x, c, ada_w, ada_b, norm_mix_g, norm_ffn_g,
              hy_w_in, hy_conv_w, hy_conv_b, hy_f_w0, hy_f_b0, hy_f_w1, hy_f_b1, hy_f_w2, hy_f_b2,
              hy_f_freq, hy_f_wout, hy_skip, hy_w_out,
              ml_w_in, ml_conv_w, ml_conv_b, ml_gate_b, ml_head_g, ml_w_out,
              moe_router_w, moe_router_bias, moe_w_gate, moe_w_up, moe_w_down,
              sh_w_gate, sh_w_up, sh_w_down, final_g):
    cs = jax.nn.silu(c)
    for layer in range(DEPTH):
        ada = (cs @ ada_w[layer] + ada_b[layer])[:, None, :]
        sh_m, sc_m, g_m, sh_f, sc_f, g_f = jnp.split(ada, 6, axis=-1)
        hm = rms_norm(x, norm_mix_g[layer]) * (1.0 + sc_m) + sh_m
        j = layer // N_MIXERS
        if layer % N_MIXERS == 0:
            y = hyena_mixer(hm, hy_w_in[j], hy_conv_w[j], hy_conv_b[j], hy_f_w0[j], hy_f_b0[j],
                            hy_f_w1[j], hy_f_b1[j], hy_f_w2[j], hy_f_b2[j], hy_f_freq[j],
                            hy_f_wout[j], hy_skip[j], hy_w_out[j])
        else:
            y = mlstm_mixer(hm, ml_w_in[j], ml_conv_w[j], ml_conv_b[j], ml_gate_b[j],
                            ml_head_g[j], ml_w_out[j])
        x = x + g_m * y
        hf = rms_norm(x, norm_ffn_g[layer]) * (1.0 + sc_f) + sh_f
        x = x + g_f * moe_ffn(hf, moe_router_w[layer], moe_router_bias[layer], moe_w_gate[layer],
                              moe_w_up[layer], moe_w_down[layer], sh_w_gate[layer], sh_w_up[layer],
                              sh_w_down[layer])
    return rms_norm(x, final_g)
```

```python
import functools
import math

import numpy as np
import jax
import jax.numpy as jnp
from jax import lax
from jax.experimental import pallas as pl
from jax.experimental.pallas import tpu as pltpu

F32 = jnp.float32
BF16 = jnp.bfloat16
EPS = 1e-6

V7X_VMEM_BYTES = 64 * 1024 * 1024
VMEM_LIMIT = V7X_VMEM_BYTES - 8 * 1024 * 1024

SHORT_CONV = 3
HY_BANDS = 16
HY_DECAY_TARGET = 1e-2
HY_FAST_DECAY = 0.3
HY_SLOW_DECAY = 1.5
ML_HEADS = 8
ML_CHUNK = 256
N_EXPERTS = 64
TOP_K = 6
N_GROUPS = 8
TOPK_GROUPS = 4
ROUTED_SCALE = 2.5
MOE_BLOCK = 256
DFT_NB = 128


def _cparams(sem):
    return pltpu.CompilerParams(dimension_semantics=sem, vmem_limit_bytes=VMEM_LIMIT)


def _ada_kernel(c_ref, w_ref, b_ref, o_ref, cs_ref, *, kc):
    c = c_ref[...]
    cs_ref[...] = c * jax.nn.sigmoid(c)
    d = c_ref.shape[0]
    tn = o_ref.shape[-1]

    def body(i, acc):
        r = pl.multiple_of(i * kc, kc)
        wc = w_ref[0, pl.ds(r, kc), :] * cs_ref[pl.ds(r, kc), :]
        return acc + wc.reshape(kc // 8, 8, tn).sum(axis=0)

    acc = lax.fori_loop(0, d // kc, body, jnp.zeros((8, tn), F32))
    o_ref[0] = jnp.sum(acc, axis=0, keepdims=True) + b_ref[0]


def ada_proj(c, ada_w, ada_b):
    depth, d, n = ada_w.shape
    tn = 1024
    return pl.pallas_call(
        functools.partial(_ada_kernel, kc=256),
        out_shape=jax.ShapeDtypeStruct((depth, 1, n), F32),
        grid=(depth, n // tn),
        in_specs=[
            pl.BlockSpec((d, 1), lambda l, j: (0, 0)),
            pl.BlockSpec((1, d, tn), lambda l, j: (l, 0, j)),
            pl.BlockSpec((1, 1, tn), lambda l, j: (l, 0, j)),
        ],
        out_specs=pl.BlockSpec((1, 1, tn), lambda l, j: (l, 0, j)),
        scratch_shapes=[pltpu.VMEM((d, 1), F32)],
        compiler_params=_cparams(("arbitrary", "arbitrary")),
        name="ada_proj",
    )(c.reshape(d, 1), ada_w, ada_b.reshape(depth, 1, n))


def _normmod(x, g, sc, sh):
    r = lax.rsqrt(jnp.mean(x * x, axis=-1, keepdims=True) + EPS)
    return (x * r) * g * (1.0 + sc) + sh


def _normmod_kernel(x_ref, g_ref, sc_ref, sh_ref, o_ref):
    o_ref[...] = _normmod(x_ref[...], g_ref[...], sc_ref[...], sh_ref[...]).astype(o_ref.dtype)


def normmod(x, g, sc, sh, out_dtype=BF16, tm=512):
    t, d = x.shape
    vec = pl.BlockSpec((1, d), lambda i: (0, 0))
    return pl.pallas_call(
        _normmod_kernel,
        out_shape=jax.ShapeDtypeStruct((t, d), out_dtype),
        grid=(t // tm,),
        in_specs=[pl.BlockSpec((tm, d), lambda i: (i, 0)), vec, vec, vec],
        out_specs=pl.BlockSpec((tm, d), lambda i: (i, 0)),
        compiler_params=_cparams(("parallel",)),
        name="normmod",
    )(x, g, sc, sh)


def _rmsnorm_kernel(x_ref, g_ref, o_ref):
    x = x_ref[...]
    r = lax.rsqrt(jnp.mean(x * x, axis=-1, keepdims=True) + EPS)
    o_ref[...] = (x * r) * g_ref[...]


def rmsnorm(x, g, tm=512):
    t, d = x.shape
    return pl.pallas_call(
        _rmsnorm_kernel,
        out_shape=jax.ShapeDtypeStruct((t, d), F32),
        grid=(t // tm,),
        in_specs=[pl.BlockSpec((tm, d), lambda i: (i, 0)), pl.BlockSpec((1, d), lambda i: (0, 0))],
        out_specs=pl.BlockSpec((tm, d), lambda i: (i, 0)),
        compiler_params=_cparams(("parallel",)),
        name="final_rmsnorm",
    )(x, g)


def _mm_kernel(a_ref, w_ref, o_ref, wb_ref):
    @pl.when(pl.program_id(1) == 0)
    def _():
        wb_ref[...] = w_ref[...].astype(BF16)

    o_ref[...] = jnp.dot(a_ref[...], wb_ref[...], preferred_element_type=F32).astype(o_ref.dtype)


def matmul_cols(a, w, n_out, out_dtype=BF16, tm=1024, tn=512):
    m, k = a.shape
    tm = min(tm, m)
    return pl.pallas_call(
        _mm_kernel,
        out_shape=jax.ShapeDtypeStruct((m, n_out), out_dtype),
        grid=(n_out // tn, m // tm),
        in_specs=[pl.BlockSpec((tm, k), lambda j, i: (i, 0)),
                  pl.BlockSpec((k, tn), lambda j, i: (0, j))],
        out_specs=pl.BlockSpec((tm, tn), lambda j, i: (i, j)),
        scratch_shapes=[pltpu.VMEM((k, tn), BF16)],
        compiler_params=_cparams(("arbitrary", "arbitrary")),
        name="matmul_cols",
    )(a, w)


def _outproj_kernel(a_ref, x_ref, gate_ref, w_ref, o_ref, wb_ref):
    @pl.when(pl.program_id(1) == 0)
    def _():
        wb_ref[...] = w_ref[...].astype(BF16)

    y = jnp.dot(a_ref[...], wb_ref[...], preferred_element_type=F32)
    o_ref[...] = x_ref[...] + gate_ref[...] * y


def outproj_residual(a, x, gate, w, tm=1024, tn=512):
    t, k = a.shape
    d = w.shape[1]
    tm = min(tm, t)
    return pl.pallas_call(
        _outproj_kernel,
        out_shape=jax.ShapeDtypeStruct((t, d), F32),
        grid=(d // tn, t // tm),
        in_specs=[pl.BlockSpec((tm, k), lambda j, i: (i, 0)),
                  pl.BlockSpec((tm, tn), lambda j, i: (i, j)),
                  pl.BlockSpec((1, tn), lambda j, i: (0, j)),
                  pl.BlockSpec((k, tn), lambda j, i: (0, j))],
        out_specs=pl.BlockSpec((tm, tn), lambda j, i: (i, j)),
        scratch_shapes=[pltpu.VMEM((k, tn), BF16)],
        compiler_params=_cparams(("arbitrary", "arbitrary")),
        name="outproj_residual",
    )(a, x, gate, w)


HALO = 16


def _conv3(main, prev, nxt, w, b, i, nt):
    tm = main.shape[0]
    rows = lax.broadcasted_iota(jnp.int32, main.shape, 0)
    pr = jnp.where(i > 0, prev[HALO - 1:HALO, :], 0.0)
    nx = jnp.where(i < nt - 1, nxt[0:1, :], 0.0)
    up = jnp.where(rows == 0, pr, pltpu.roll(main, 1, 0))
    dn = jnp.where(rows == tm - 1, nx, pltpu.roll(main, tm - 1, 0))
    return up * w[0:1, :] + main * w[1:2, :] + dn * w[2:3, :] + b


def _halo_specs(tm, tc, t, col_fn):
    nh = t // HALO
    r = tm // HALO
    return [
        pl.BlockSpec((tm, tc), lambda i, j: (i, col_fn(j))),
        pl.BlockSpec((HALO, tc), lambda i, j: (jnp.maximum(i * r - 1, 0), col_fn(j))),
        pl.BlockSpec((HALO, tc), lambda i, j: (jnp.minimum((i + 1) * r, nh - 1), col_fn(j))),
    ]


def _hy_gate_kernel(*refs):
    (x0m, x0p, x0n, x1m, x1p, x1n, vm, vp, vn,
     w0, w1, w2, b0, b1, b2, vg_ref, x0_ref) = refs
    i = pl.program_id(0)
    nt = pl.num_programs(0)
    f = lambda r: r[...].astype(F32)
    x0 = _conv3(f(x0m), f(x0p), f(x0n), w0[...], b0[...], i, nt)
    x1 = _conv3(f(x1m), f(x1p), f(x1n), w1[...], b1[...], i, nt)
    v = _conv3(f(vm), f(vp), f(vn), w2[...], b2[...], i, nt)
    vg_ref[...] = (v * x1).astype(vg_ref.dtype)
    x0_ref[...] = x0.astype(x0_ref.dtype)


def hy_gate(p, conv_w, conv_b, tm=512, tc=512):
    t, c3 = p.shape
    d = c3 // 3
    nb = d // tc
    specs = []
    for g in range(3):
        specs += _halo_specs(tm, tc, t, lambda j, g=g: g * nb + j)
    for g in range(3):
        specs.append(pl.BlockSpec((SHORT_CONV, tc), lambda i, j, g=g: (0, g * nb + j)))
    for g in range(3):
        specs.append(pl.BlockSpec((1, tc), lambda i, j, g=g: (0, g * nb + j)))
    out_spec = pl.BlockSpec((tm, tc), lambda i, j: (i, j))
    cb = conv_b.reshape(1, c3)
    return pl.pallas_call(
        _hy_gate_kernel,
        out_shape=(jax.ShapeDtypeStruct((t, d), BF16), jax.ShapeDtypeStruct((t, d), BF16)),
        grid=(t // tm, nb),
        in_specs=specs,
        out_specs=(out_spec, out_spec),
        compiler_params=_cparams(("parallel", "parallel")),
        name="hy_gate",
    )(p, p, p, p, p, p, p, p, p, conv_w, conv_w, conv_w, cb, cb, cb)


def _hy_features(seq):
    t = np.linspace(0.0, 1.0, seq, dtype=np.float64)[:, None]
    ang = 2.0 * math.pi * np.arange(seq, dtype=np.float64)[:, None] / seq
    bands = np.linspace(1e-4, HY_BANDS - 1, HY_BANDS, dtype=np.float64)
    z = np.concatenate([t, np.cos(ang * bands), -np.sin(ang * bands)], axis=-1)
    zp = np.zeros((seq, 128), np.float32)
    zp[:, :z.shape[1]] = z
    return zp


def _hy_deltas(d):
    return np.abs(np.linspace(math.log(HY_DECAY_TARGET) / HY_FAST_DECAY,
                              math.log(HY_DECAY_TARGET) / HY_SLOW_DECAY, d,
                              dtype=np.float64)).astype(np.float32)[None, :]


def _hy_filter_kernel(z_ref, w0_ref, b0_ref, w1_ref, b1_ref, w2_ref, b2_ref, fr_ref, wo_ref,
                      dl_ref, o_ref, *, seq):
    hi = lax.Precision.HIGHEST
    fr = fr_ref[...]
    a = jnp.sin(fr * (jnp.dot(z_ref[...], w0_ref[...], precision=hi, preferred_element_type=F32)
                      + b0_ref[...]))
    a = jnp.sin(fr * (jnp.dot(a, w1_ref[...], precision=hi, preferred_element_type=F32) + b1_ref[...]))
    a = jnp.sin(fr * (jnp.dot(a, w2_ref[...], precision=hi, preferred_element_type=F32) + b2_ref[...]))
    k = jnp.dot(a.astype(BF16), wo_ref[...].astype(BF16), preferred_element_type=F32)
    tl, d2 = k.shape
    d = d2 // 2
    row = pl.program_id(0) * tl + lax.broadcasted_iota(jnp.int32, (tl, 1), 0)
    tpos = row.astype(F32) * (1.0 / (seq - 1))
    window = jnp.exp(-tpos * dl_ref[...])
    o_ref[:, :d] = (k[:, :d] * window).astype(o_ref.dtype)
    o_ref[:, d:] = jnp.where(row == 0, 0.0, k[:, d:] * window).astype(o_ref.dtype)


def hy_filter(seq, w0, b0, w1, b1, w2, b2, freq, wout, tl=256):
    fw = w1.shape[0]
    d2 = wout.shape[1]
    z = jnp.asarray(_hy_features(seq))
    w0p = jnp.zeros((128, fw), F32).at[:w0.shape[0]].set(w0)
    full = lambda shape: pl.BlockSpec(shape, lambda i: (0,) * len(shape))
    r = lambda v: v.reshape(1, -1)
    return pl.pallas_call(
        functools.partial(_hy_filter_kernel, seq=seq),
        out_shape=jax.ShapeDtypeStruct((seq, d2), BF16),
        grid=(seq // tl,),
        in_specs=[pl.BlockSpec((tl, 128), lambda i: (i, 0)),
                  full((128, fw)), full((1, fw)), full((fw, fw)), full((1, fw)),
                  full((fw, fw)), full((1, fw)), full((1, fw)), full((fw, d2)), full((1, d2 // 2))],
        out_specs=pl.BlockSpec((tl, d2), lambda i: (i, 0)),
        compiler_params=_cparams(("parallel",)),
        name="hy_filter",
    )(z, w0p, r(b0), w1, r(b1), w2, r(b2), r(freq), wout, jnp.asarray(_hy_deltas(d2 // 2)))


@functools.lru_cache(maxsize=None)
def _dft_tables(seq):
    n = 2 * seq
    nb = DFT_NB
    na = n // nb
    s1 = np.arange(na // 2)
    f1 = np.arange(na)
    ph = 2.0 * math.pi * np.outer(f1, s1) / na
    fs = np.concatenate([np.cos(ph), -np.sin(ph)], axis=0)
    f2 = np.arange(nb)
    s2 = np.arange(nb)
    freq = f1[:, None, None] + na * f2[None, :, None]
    th = 2.0 * math.pi * ((freq * s2[None, None, :]) % n) / n
    gr, gi = np.cos(th), -np.sin(th)
    g = np.concatenate([np.concatenate([gr, -gi], axis=2),
                        np.concatenate([gi, gr], axis=2)], axis=1)
    ginv = np.transpose(g, (0, 2, 1))
    t1 = np.arange(na // 2)
    ph2 = 2.0 * math.pi * np.outer(t1, f1) / na
    finv = np.concatenate([np.cos(ph2), -np.sin(ph2)], axis=1)
    as_bf16 = lambda a: jnp.asarray(a.astype(np.float32)).astype(BF16)
    return as_bf16(fs), as_bf16(g), as_bf16(ginv), as_bf16(finv)


def _dft1_kernel(f_ref, x_ref, o_ref):
    y = jnp.dot(f_ref[...], x_ref[...], preferred_element_type=F32)
    o_ref[...] = y.reshape(o_ref.shape).astype(o_ref.dtype)


def dft_stage1(x2d, fs, tn=8192):
    k, cols = x2d.shape
    na = fs.shape[0] // 2
    return pl.pallas_call(
        _dft1_kernel,
        out_shape=jax.ShapeDtypeStruct((2, na, cols), BF16),
        grid=(cols // tn,),
        in_specs=[pl.BlockSpec((2 * na, k), lambda c: (0, 0)),
                  pl.BlockSpec((k, tn), lambda c: (0, c))],
        out_specs=pl.BlockSpec((2, na, tn), lambda c: (0, 0, c)),
        compiler_params=_cparams(("parallel",)),
        name="dft_stage1",
    )(fs, x2d)


def _hspec_kernel(g_ref, af_ref, ab_ref, o_ref, *, scale):
    nb2, dt = o_ref.shape[1], o_ref.shape[2]
    g = g_ref[0]
    hf = jnp.dot(g, af_ref[...].reshape(nb2, dt), preferred_element_type=F32)
    hb = jnp.dot(g, ab_ref[...].reshape(nb2, dt), preferred_element_type=F32)
    nb = nb2 // 2
    o_ref[0, :nb, :] = ((hf[:nb] + hb[:nb]) * scale).astype(o_ref.dtype)
    o_ref[0, nb:, :] = ((hf[nb:] - hb[nb:]) * scale).astype(o_ref.dtype)


def filter_spectrum(ak, g, d, dt=1024):
    _, na, nb, d2 = ak.shape
    nd = d // dt
    scale = 1.0 / (na * nb)
    return pl.pallas_call(
        functools.partial(_hspec_kernel, scale=scale),
        out_shape=jax.ShapeDtypeStruct((na, 2 * nb, d), BF16),
        grid=(na, nd),
        in_specs=[pl.BlockSpec((1, 2 * nb, 2 * nb), lambda f, j: (f, 0, 0)),
                  pl.BlockSpec((2, 1, nb, dt), lambda f, j: (0, f, 0, j)),
                  pl.BlockSpec((2, 1, nb, dt), lambda f, j: (0, f, 0, nd + j))],
        out_specs=pl.BlockSpec((1, 2 * nb, dt), lambda f, j: (f, 0, j)),
        compiler_params=_cparams(("parallel", "parallel")),
        name="filter_spectrum",
    )(g, ak, ak)


def _xspec_kernel(g_ref, gi_ref, a_ref, h_ref, o_ref):
    nb2, dt = h_ref.shape[1], h_ref.shape[2]
    nb = nb2 // 2
    x = jnp.dot(g_ref[0], a_ref[...].reshape(nb2, dt), preferred_element_type=F32)
    h = h_ref[0].astype(F32)
    xr, xi, hr, hi = x[:nb], x[nb:], h[:nb], h[nb:]
    y = jnp.concatenate([xr * hr - xi * hi, xr * hi + xi * hr], axis=0).astype(BF16)
    b = jnp.dot(gi_ref[0], y, preferred_element_type=F32)
    o_ref[...] = b.reshape(o_ref.shape).astype(o_ref.dtype)


def spectrum_product(a, h, g, ginv, dt=1024):
    _, na, nb, d = a.shape
    gspec = pl.BlockSpec((1, 2 * nb, 2 * nb), lambda f, j: (f, 0, 0))
    aspec = pl.BlockSpec((2, 1, nb, dt), lambda f, j: (0, f, 0, j))
    return pl.pallas_call(
        _xspec_kernel,
        out_shape=jax.ShapeDtypeStruct((2, na, nb, d), BF16),
        grid=(na, d // dt),
        in_specs=[gspec, gspec, aspec, pl.BlockSpec((1, 2 * nb, dt), lambda f, j: (f, 0, j))],
        out_specs=aspec,
        compiler_params=_cparams(("parallel", "parallel")),
        name="spectrum_product",
    )(g, ginv, a, h)


def _idft2_kernel(f_ref, b_ref, vg_ref, x0_ref, skip_ref, o_ref):
    conv = jnp.dot(f_ref[...], b_ref[...], preferred_element_type=F32)
    vg = vg_ref[...].astype(F32)
    o_ref[...] = ((conv + vg * skip_ref[...]) * x0_ref[...].astype(F32)).astype(o_ref.dtype)


def idft_stage2_mix(b2d, finv, vg2d, x02d, skip_t, tn=4096):
    k, cols = b2d.shape
    rows = finv.shape[0]
    return pl.pallas_call(
        _idft2_kernel,
        out_shape=jax.ShapeDtypeStruct((rows, cols), BF16),
        grid=(cols // tn,),
        in_specs=[pl.BlockSpec((rows, k), lambda c: (0, 0)),
                  pl.BlockSpec((k, tn), lambda c: (0, c)),
                  pl.BlockSpec((rows, tn), lambda c: (0, c)),
                  pl.BlockSpec((rows, tn), lambda c: (0, c)),
                  pl.BlockSpec((1, tn), lambda c: (0, 0))],
        out_specs=pl.BlockSpec((rows, tn), lambda c: (0, c)),
        compiler_params=_cparams(("parallel",)),
        name="idft_stage2_mix",
    )(finv, b2d, vg2d, x02d, skip_t)


def hyena_mixer(hm, x, gate, w_in, conv_w, conv_b, f_w0, f_b0, f_w1, f_b1, f_w2, f_b2, f_freq,
                f_wout, skip, w_out):
    seq, d = hm.shape
    nb = DFT_NB
    na = 2 * seq // nb
    fs, g, ginv, finv = _dft_tables(seq)
    p = matmul_cols(hm, w_in, 3 * d)
    vg, x0 = hy_gate(p, conv_w, conv_b)
    kf = hy_filter(seq, f_w0, f_b0, f_w1, f_b1, f_w2, f_b2, f_freq, f_wout)
    ak = dft_stage1(kf.reshape(na // 2, nb * 2 * d), fs)
    h = filter_spectrum(ak.reshape(2, na, nb, 2 * d), g, d)
    a = dft_stage1(vg.reshape(na // 2, nb * d), fs)
    b = spectrum_product(a.reshape(2, na, nb, d), h, g, ginv)
    tn = 2 * d
    skip_t = jnp.tile(skip.reshape(1, d), (1, tn // d))
    y = idft_stage2_mix(b.reshape(2 * na, nb * d), finv, vg.reshape(na // 2, nb * d),
                        x0.reshape(na // 2, nb * d), skip_t, tn=tn)
    return outproj_residual(y.reshape(seq, d), x, gate, w_out)


def _qk_prep_kernel(m_ref, p_ref, n_ref, w_ref, b_ref, s_ref, o_ref):
    i = pl.program_id(0)
    nt = pl.num_programs(0)
    f = lambda r: r[...].astype(F32)
    u = _conv3(f(m_ref), f(p_ref), f(n_ref), w_ref[...], b_ref[...], i, nt)
    o_ref[...] = (u * jax.nn.sigmoid(u) * s_ref[...]).astype(o_ref.dtype)


def qk_prep(p, conv_w, conv_b, qk_dim, dqk, tm=512, tc=512):
    t = p.shape[0]
    c = 2 * qk_dim
    scale = np.ones((1, c), np.float32)
    scale[:, qk_dim:] = 1.0 / math.sqrt(dqk)
    return pl.pallas_call(
        _qk_prep_kernel,
        out_shape=jax.ShapeDtypeStruct((t, c), BF16),
        grid=(t // tm, c // tc),
        in_specs=_halo_specs(tm, tc, t, lambda j: j) + [
            pl.BlockSpec((SHORT_CONV, tc), lambda i, j: (0, j)),
            pl.BlockSpec((1, tc), lambda i, j: (0, j)),
            pl.BlockSpec((1, tc), lambda i, j: (0, j))],
        out_specs=pl.BlockSpec((tm, tc), lambda i, j: (i, j)),
        compiler_params=_cparams(("parallel", "parallel")),
        name="qk_prep",
    )(p, p, p, conv_w, conv_b.reshape(1, c), jnp.asarray(scale))


def _log_sigmoid(x):
    return jnp.minimum(x, 0.0) - jnp.log1p(jnp.exp(-jnp.abs(x)))


def _gates_kernel(hm_ref, w_ref, wt_ref, b_ref, bt_ref, col_ref, row_ref, scal_ref):
    hi = lax.Precision.HIGHEST
    nh = ML_HEADS
    hm = hm_ref[...]
    cs = hm.shape[0]
    gt = jnp.dot(hm, w_ref[...].astype(BF16), preferred_element_type=F32) + b_ref[...]
    gtt = lax.dot_general(wt_ref[...].astype(BF16), hm, (((1,), (1,)), ((), ())),
                          preferred_element_type=F32) + bt_ref[...]
    r = lax.broadcasted_iota(jnp.int32, (cs, cs), 0)
    c = lax.broadcasted_iota(jnp.int32, (cs, cs), 1)
    lower = (r >= c).astype(F32)
    upper = (r <= c).astype(F32)
    i_f, f_f, i_b, f_b = (gt[:, k * nh:(k + 1) * nh] for k in range(4))
    lf_f, lf_b = _log_sigmoid(f_f), _log_sigmoid(f_b)
    b_f = jnp.dot(lower, lf_f, precision=hi, preferred_element_type=F32)
    b_b = jnp.dot(upper, lf_b, precision=hi, preferred_element_type=F32)
    g_f = jnp.sum(lf_f, axis=0, keepdims=True)
    g_b = jnp.sum(lf_b, axis=0, keepdims=True)
    a_f = g_f - b_f + i_f
    a_b = g_b - b_b + i_b
    col_ref[...] = jnp.concatenate([b_f, b_b, a_f, a_b], axis=1)
    scal_ref[0] = jnp.concatenate([g_f, g_b, jnp.max(a_f, axis=0, keepdims=True),
                                   jnp.max(a_b, axis=0, keepdims=True)], axis=1)
    i_ft, f_ft, i_bt, f_bt = (gtt[k * nh:(k + 1) * nh, :] for k in range(4))
    b_ft = jnp.dot(_log_sigmoid(f_ft), upper, precision=hi, preferred_element_type=F32)
    b_bt = jnp.dot(_log_sigmoid(f_bt), lower, precision=hi, preferred_element_type=F32)
    row_ref[...] = jnp.concatenate([b_ft, b_bt, i_ft, i_bt], axis=0)


def ml_gates(hm, w_g, gate_b, cs):
    t, d = hm.shape
    g4 = w_g.shape[1]
    nc = t // cs
    return pl.pallas_call(
        _gates_kernel,
        out_shape=(jax.ShapeDtypeStruct((t, g4), F32), jax.ShapeDtypeStruct((g4, t), F32),
                   jax.ShapeDtypeStruct((nc, 1, g4), F32)),
        grid=(nc,),
        in_specs=[pl.BlockSpec((cs, d), lambda c: (c, 0)),
                  pl.BlockSpec((d, g4), lambda c: (0, 0)),
                  pl.BlockSpec((g4, d), lambda c: (0, 0)),
                  pl.BlockSpec((1, g4), lambda c: (0, 0)),
                  pl.BlockSpec((g4, 1), lambda c: (0, 0))],
        out_specs=(pl.BlockSpec((cs, g4), lambda c: (c, 0)),
                   pl.BlockSpec((g4, cs), lambda c: (0, c)),
                   pl.BlockSpec((1, 1, g4), lambda c: (c, 0, 0))),
        compiler_params=_cparams(("parallel",)),
        name="ml_gates",
    )(hm, w_g, w_g.T, gate_b.reshape(1, g4), gate_b.reshape(g4, 1))


def _mlstm_head(q, k, v, b_col, a_col, b_row, li_row, g, m_loc, c_ref, n_ref, m_ref, idx, causal):
    cs = q.shape[0]
    c_st = c_ref[idx]
    n_st = n_ref[idx]
    m_st = m_ref[idx][:, 0:1]
    r = lax.broadcasted_iota(jnp.int32, (cs, cs), 0)
    s = lax.broadcasted_iota(jnp.int32, (cs, cs), 1)
    mask = (s <= r) if causal else (s >= r)
    dlog = jnp.where(mask, b_col - b_row + li_row, -jnp.inf)
    m_inter = b_col + m_st
    m_t = jnp.maximum(m_inter, jnp.max(dlog, axis=-1, keepdims=True))
    qk = lax.dot_general(q, k, (((1,), (1,)), ((), ())), preferred_element_type=F32)
    p = jnp.exp(dlog - m_t) * qk
    s_inter = jnp.exp(m_inter - m_t)
    qf = q.astype(F32)
    num = (s_inter * jnp.dot(q, c_st.astype(BF16), preferred_element_type=F32)
           + jnp.dot(p.astype(BF16), v, preferred_element_type=F32))
    den = s_inter * jnp.sum(qf * n_st, axis=-1, keepdims=True) + jnp.sum(p, axis=-1, keepdims=True)
    hout = num / jnp.maximum(jnp.abs(den), jnp.exp(-m_t))
    wgt = jnp.exp(a_col - m_loc)
    kw = k.astype(F32) * wgt
    c_loc = lax.dot_general(kw.astype(BF16), v, (((0,), (0,)), ((), ())), preferred_element_type=F32)
    n_loc = jnp.sum(kw, axis=0, keepdims=True)
    m_new = jnp.maximum(g + m_st, m_loc)
    s_prev = jnp.exp(g + m_st - m_new)
    s_loc = jnp.exp(m_loc - m_new)
    c_ref[idx] = s_prev * c_st + s_loc * c_loc
    n_ref[idx] = s_prev * n_st + s_loc * n_loc
    m_ref[idx] = jnp.broadcast_to(m_new, m_ref.shape[1:])
    return hout


def _mlstm_kernel(qkf_ref, qkb_ref, vf_ref, vb_ref, colf_ref, colb_ref, rowf_ref, rowb_ref,
                  scf_ref, scb_ref, of_ref, ob_ref, c_ref, n_ref, m_ref, *, dqk, dv):
    nh = ML_HEADS

    @pl.when(pl.program_id(0) == 0)
    def _():
        c_ref[...] = jnp.zeros_like(c_ref)
        n_ref[...] = jnp.zeros_like(n_ref)
        m_ref[...] = jnp.zeros_like(m_ref)

    qkd = nh * dqk
    for direction, (qk_ref, v_ref, col_ref, row_ref, sc_ref, o_ref) in enumerate(
            ((qkf_ref, vf_ref, colf_ref, rowf_ref, scf_ref, of_ref),
             (qkb_ref, vb_ref, colb_ref, rowb_ref, scb_ref, ob_ref))):
        for h in range(nh):
            gi = direction * nh + h
            q = qk_ref[:, h * dqk:(h + 1) * dqk]
            k = qk_ref[:, qkd + h * dqk:qkd + (h + 1) * dqk]
            v = v_ref[:, h * dv:(h + 1) * dv]
            b_col = col_ref[:, gi:gi + 1]
            a_col = col_ref[:, 2 * nh + gi:2 * nh + gi + 1]
            b_row = row_ref[gi:gi + 1, :]
            li_row = row_ref[2 * nh + gi:2 * nh + gi + 1, :]
            g = sc_ref[0, :, gi:gi + 1]
            m_loc = sc_ref[0, :, 2 * nh + gi:2 * nh + gi + 1]
            o_ref[:, h * dv:(h + 1) * dv] = _mlstm_head(
                q, k, v, b_col, a_col, b_row, li_row, g, m_loc, c_ref, n_ref, m_ref, gi,
                causal=(direction == 0))


def mlstm_bidir(qk, p, col, row, scal, cs, dqk, dv):
    t = qk.shape[0]
    nh = ML_HEADS
    nc = t // cs
    qkd2 = 2 * nh * dqk
    vd = nh * dv
    vblk = qkd2 // vd
    g4 = col.shape[1]
    fwd = lambda c: c
    bwd = lambda c: nc - 1 - c
    mk = lambda fn: dict(
        qk=pl.BlockSpec((cs, qkd2), lambda c: (fn(c), 0)),
        v=pl.BlockSpec((cs, vd), lambda c: (fn(c), vblk)),
        col=pl.BlockSpec((cs, g4), lambda c: (fn(c), 0)),
        row=pl.BlockSpec((g4, cs), lambda c: (0, fn(c))),
        sc=pl.BlockSpec((1, 1, g4), lambda c: (fn(c), 0, 0)),
        o=pl.BlockSpec((cs, vd), lambda c: (fn(c), 0)))
    sf, sb = mk(fwd), mk(bwd)
    return pl.pallas_call(
        functools.partial(_mlstm_kernel, dqk=dqk, dv=dv),
        out_shape=(jax.ShapeDtypeStruct((t, vd), F32), jax.ShapeDtypeStruct((t, vd), F32)),
        grid=(nc,),
        in_specs=[sf["qk"], sb["qk"], sf["v"], sb["v"], sf["col"], sb["col"], sf["row"], sb["row"],
                  sf["sc"], sb["sc"]],
        out_specs=(sf["o"], sb["o"]),
        scratch_shapes=[pltpu.VMEM((2 * nh, dqk, dv), F32), pltpu.VMEM((2 * nh, 1, dqk), F32),
                        pltpu.VMEM((2 * nh, 1, 128), F32)],
        compiler_params=_cparams(("arbitrary",)),
        name="mlstm_bidir",
    )(qk, qk, p, p, col, col, row, row, scal, scal)


def _ml_post_kernel(hf_ref, hb_ref, o_ref, hg_ref, out_ref, *, dv):
    nh = ML_HEADS
    for h in range(nh):
        sl = slice(h * dv, (h + 1) * dv)
        hs = hf_ref[:, sl] + hb_ref[:, sl]
        hs = hs * lax.rsqrt(jnp.mean(hs * hs, axis=-1, keepdims=True) + EPS)
        og = jax.nn.sigmoid(o_ref[:, sl].astype(F32))
        out_ref[:, sl] = (hs * hg_ref[:, sl] * og).astype(out_ref.dtype)


def ml_post(hf, hb, p, head_g, dv, tm=512):
    t, vd = hf.shape
    oblk = p.shape[1] // vd - 1
    spec = pl.BlockSpec((tm, vd), lambda i: (i, 0))
    return pl.pallas_call(
        functools.partial(_ml_post_kernel, dv=dv),
        out_shape=jax.ShapeDtypeStruct((t, vd), BF16),
        grid=(t // tm,),
        in_specs=[spec, spec, pl.BlockSpec((tm, vd), lambda i: (i, oblk)),
                  pl.BlockSpec((1, vd), lambda i: (0, 0))],
        out_specs=spec,
        compiler_params=_cparams(("parallel",)),
        name="ml_post",
    )(hf, hb, p, head_g.reshape(1, vd))


def mlstm_mixer(hm, x, gate, w_in, conv_w, conv_b, gate_b, head_g, w_out):
    seq, d = hm.shape
    nh = ML_HEADS
    qk_dim = d // 2
    dqk = qk_dim // nh
    dv = d // nh
    n_main = 2 * qk_dim + 2 * d
    cs = min(ML_CHUNK, seq)
    p = matmul_cols(hm, w_in, n_main)
    col, row, scal = ml_gates(hm, w_in[:, n_main:], gate_b, cs)
    qk = qk_prep(p, conv_w, conv_b, qk_dim, dqk)
    hf, hb = mlstm_bidir(qk, p, col, row, scal, cs, dqk, dv)
    a = ml_post(hf, hb, p, head_g, dv)
    return outproj_residual(a, x, gate, w_out)


def _first_argmax(vals, axis, n):
    m = jnp.max(vals, axis=axis, keepdims=True)
    iota = lax.broadcasted_iota(jnp.int32, vals.shape, axis)
    idx = jnp.min(jnp.where(vals == m, iota, n), axis=axis, keepdims=True)
    return m, idx, iota


def _router_kernel(x_ref, g_ref, sc_ref, sh_ref, rwt_ref, rb_ref,
                   hf_ref, eidx_ref, rank_ref, wk_ref, cnt_ref, carry_ref):
    ne, ng = N_EXPERTS, N_GROUPS
    per = ne // ng
    tm = x_ref.shape[0]

    @pl.when(pl.program_id(0) == 0)
    def _():
        carry_ref[...] = jnp.zeros_like(carry_ref)

    hf = _normmod(x_ref[...], g_ref[...], sc_ref[...], sh_ref[...])
    hf_ref[...] = hf
    logits = lax.dot_general(rwt_ref[...], hf, (((1,), (1,)), ((), ())),
                             precision=lax.Precision.HIGHEST, preferred_element_type=F32)
    scores = jax.nn.sigmoid(logits)
    sel = scores + rb_ref[...]
    sel3 = sel.reshape(ng, per, tm)
    m1, i1, io3 = _first_argmax(sel3, 1, per)
    m2 = jnp.max(jnp.where(io3 == i1, -jnp.inf, sel3), axis=1, keepdims=True)
    gs = (m1 + m2).reshape(ng, tm)
    gsel = jnp.zeros((ng, tm), F32)
    for _ in range(TOPK_GROUPS):
        _, gi, iog = _first_argmax(gs, 0, ng)
        hit = iog == gi
        gsel = jnp.where(hit, 1.0, gsel)
        gs = jnp.where(hit, -jnp.inf, gs)
    gmask = jnp.broadcast_to(gsel.reshape(ng, 1, tm), (ng, per, tm)).reshape(ne, tm)
    cand = jnp.where(gmask > 0.5, sel, -jnp.inf)
    picked = []
    chosen = jnp.zeros((ne, tm), F32)
    for _ in range(TOP_K):
        _, ei, ioe = _first_argmax(cand, 0, ne)
        hit = ioe == ei
        picked.append((ei, hit))
        chosen = jnp.where(hit, 1.0, chosen)
        cand = jnp.where(hit, -jnp.inf, cand)
    r = lax.broadcasted_iota(jnp.int32, (tm, tm), 0)
    c = lax.broadcasted_iota(jnp.int32, (tm, tm), 1)
    before = (r < c).astype(BF16)
    ranks = jnp.dot(chosen.astype(BF16), before, preferred_element_type=F32) + carry_ref[:, 0:1]
    carry_ref[...] = carry_ref[...] + jnp.sum(chosen, axis=1, keepdims=True)
    cnt_ref[...] = carry_ref[...]
    wks = [jnp.sum(jnp.where(hit, scores, 0.0), axis=0, keepdims=True) for _, hit in picked]
    wsum = functools.reduce(lambda a, b: a + b, wks)
    eidx_ref[...] = jnp.zeros_like(eidx_ref)
    rank_ref[...] = jnp.zeros_like(rank_ref)
    wk_ref[...] = jnp.zeros_like(wk_ref)
    for j, ((ei, hit), wk) in enumerate(zip(picked, wks)):
        eidx_ref[j:j + 1, :] = ei
        rank_ref[j:j + 1, :] = jnp.sum(jnp.where(hit, ranks, 0.0), axis=0, keepdims=True).astype(jnp.int32)
        wk_ref[j:j + 1, :] = wk / wsum * ROUTED_SCALE


def moe_router(x, g, sc, sh, router_w, router_bias, tm=512):
    t, d = x.shape
    ne = N_EXPERTS
    vec = pl.BlockSpec((1, d), lambda i: (0, 0))
    lane = pl.BlockSpec((8, tm), lambda i: (0, i))
    return pl.pallas_call(
        _router_kernel,
        out_shape=(jax.ShapeDtypeStruct((t, d), F32), jax.ShapeDtypeStruct((8, t), jnp.int32),
                   jax.ShapeDtypeStruct((8, t), jnp.int32), jax.ShapeDtypeStruct((8, t), F32),
                   jax.ShapeDtypeStruct((ne, 128), F32)),
        grid=(t // tm,),
        in_specs=[pl.BlockSpec((tm, d), lambda i: (i, 0)), vec, vec, vec,
                  pl.BlockSpec((ne, d), lambda i: (0, 0)), pl.BlockSpec((ne, 1), lambda i: (0, 0))],
        out_specs=(pl.BlockSpec((tm, d), lambda i: (i, 0)), lane, lane, lane,
                   pl.BlockSpec((ne, 128), lambda i: (0, 0))),
        scratch_shapes=[pltpu.VMEM((ne, 128), F32)],
        compiler_params=_cparams(("arbitrary",)),
        name="moe_router",
    )(x, g, sc, sh, router_w.T, router_bias.reshape(ne, 1))


def _dispatch_kernel(dest_ref, hf_ref, xb_ref, sem, *, t_total):
    tm = hf_ref.shape[0]
    base = pl.program_id(0) * tm

    def row_copy(t, j):
        d = dest_ref[j * t_total + base + t]
        return pltpu.make_async_copy(hf_ref.at[pl.ds(t, 1)], xb_ref.at[pl.ds(d, 1)], sem)

    def start(t, carry):
        for j in range(TOP_K):
            row_copy(t, j).start()
        return carry

    def wait(t, carry):
        for j in range(TOP_K):
            row_copy(t, j).wait()
        return carry

    lax.fori_loop(0, tm, start, 0)
    lax.fori_loop(0, tm, wait, 0)


def moe_dispatch(dest, hf, n_slots, tm=256):
    t, d = hf.shape
    return pl.pallas_call(
        functools.partial(_dispatch_kernel, t_total=t),
        out_shape=jax.ShapeDtypeStruct((n_slots, d), F32),
        grid_spec=pltpu.PrefetchScalarGridSpec(
            num_scalar_prefetch=1,
            grid=(t // tm,),
            in_specs=[pl.BlockSpec((tm, d), lambda i, dest: (i, 0))],
            out_specs=pl.BlockSpec(memory_space=pl.ANY),
            scratch_shapes=[pltpu.SemaphoreType.DMA(())]),
        compiler_params=_cparams(("arbitrary",)),
        name="moe_dispatch",
    )(dest, hf)


def _ffn(x, wg, wu, wd):
    hg = jnp.dot(x, wg, preferred_element_type=F32)
    hu = jnp.dot(x, wu, preferred_element_type=F32)
    h = (hg * jax.nn.sigmoid(hg) * hu).astype(BF16)
    return jnp.dot(h, wd, preferred_element_type=F32)


def _expert_kernel(be_ref, na_ref, x_ref, wg_ref, wu_ref, wd_ref, o_ref, wgb, wub, wdb):
    b = pl.program_id(0)
    prev = be_ref[jnp.maximum(b - 1, 0)]
    active = b < na_ref[0]

    @pl.when(jnp.logical_and(active, jnp.logical_or(b == 0, be_ref[b] != prev)))
    def _():
        wgb[...] = wg_ref[0, 0].astype(BF16)
        wub[...] = wu_ref[0, 0].astype(BF16)
        wdb[...] = wd_ref[0, 0].astype(BF16)

    @pl.when(active)
    def _():
        o_ref[...] = _ffn(x_ref[...].astype(BF16), wgb[...], wub[...], wdb[...])


def moe_experts(blk_e, n_active, xb, w_gate, w_up, w_down, layer):
    p, d = xb.shape
    ff = w_gate.shape[3]
    nblk = p // MOE_BLOCK
    row = lambda b, be, na: (jnp.minimum(b, na[0] - 1), 0)
    wsel = lambda b, be, na: (layer, be[b], 0, 0)
    return pl.pallas_call(
        _expert_kernel,
        out_shape=jax.ShapeDtypeStruct((p, d), F32),
        grid_spec=pltpu.PrefetchScalarGridSpec(
            num_scalar_prefetch=2,
            grid=(nblk,),
            in_specs=[pl.BlockSpec((MOE_BLOCK, d), row),
                      pl.BlockSpec((1, 1, d, ff), wsel), pl.BlockSpec((1, 1, d, ff), wsel),
                      pl.BlockSpec((1, 1, ff, d), wsel)],
            out_specs=pl.BlockSpec((MOE_BLOCK, d), row),
            scratch_shapes=[pltpu.VMEM((d, ff), BF16), pltpu.VMEM((d, ff), BF16),
                            pltpu.VMEM((ff, d), BF16)]),
        compiler_params=_cparams(("arbitrary",)),
        name="moe_experts",
    )(blk_e, n_active, xb, w_gate, w_up, w_down)


def _shared_kernel(x_ref, wg_ref, wu_ref, wd_ref, o_ref, wgb, wub, wdb):
    @pl.when(pl.program_id(0) == 0)
    def _():
        wgb[...] = wg_ref[...].astype(BF16)
        wub[...] = wu_ref[...].astype(BF16)
        wdb[...] = wd_ref[...].astype(BF16)

    o_ref[...] = _ffn(x_ref[...].astype(BF16), wgb[...], wub[...], wdb[...])


def shared_expert(hf, w_gate, w_up, w_down, tm=512):
    t, d = hf.shape
    ff = w_gate.shape[1]
    full = lambda shape: pl.BlockSpec(shape, lambda i: (0, 0))
    return pl.pallas_call(
        _shared_kernel,
        out_shape=jax.ShapeDtypeStruct((t, d), F32),
        grid=(t // tm,),
        in_specs=[pl.BlockSpec((tm, d), lambda i: (i, 0)), full((d, ff)), full((d, ff)), full((ff, d))],
        out_specs=pl.BlockSpec((tm, d), lambda i: (i, 0)),
        scratch_shapes=[pltpu.VMEM((d, ff), BF16), pltpu.VMEM((d, ff), BF16), pltpu.VMEM((ff, d), BF16)],
        compiler_params=_cparams(("arbitrary",)),
        name="shared_expert",
    )(hf, w_gate, w_up, w_down)


def _combine_kernel(dest_ref, yb_ref, x_ref, sh_ref, wk_ref, gate_ref, o_ref, buf, sem, *, t_total):
    tm = x_ref.shape[0]
    base = pl.program_id(0) * tm

    def row_copy(t, j):
        d = dest_ref[j * t_total + base + t]
        return pltpu.make_async_copy(yb_ref.at[pl.ds(d, 1)], buf.at[j, pl.ds(t, 1)], sem)

    def start(t, carry):
        for j in range(TOP_K):
            row_copy(t, j).start()
        return carry

    def wait(t, carry):
        for j in range(TOP_K):
            row_copy(t, j).wait()
        return carry

    lax.fori_loop(0, tm, start, 0)
    lax.fori_loop(0, tm, wait, 0)
    acc = sh_ref[...]
    for j in range(TOP_K):
        acc = acc + wk_ref[:, j:j + 1] * buf[j]
    o_ref[...] = x_ref[...] + gate_ref[...] * acc


def moe_combine(dest, yb, x, shared, wk_t, gate, tm=128):
    t, d = x.shape
    tile = lambda i, dest: (i, 0)
    return pl.pallas_call(
        functools.partial(_combine_kernel, t_total=t),
        out_shape=jax.ShapeDtypeStruct((t, d), F32),
        grid_spec=pltpu.PrefetchScalarGridSpec(
            num_scalar_prefetch=1,
            grid=(t // tm,),
            in_specs=[pl.BlockSpec(memory_space=pl.ANY),
                      pl.BlockSpec((tm, d), tile), pl.BlockSpec((tm, d), tile),
                      pl.BlockSpec((tm, 8), tile), pl.BlockSpec((1, d), lambda i, dest: (0, 0))],
            out_specs=pl.BlockSpec((tm, d), tile),
            scratch_shapes=[pltpu.VMEM((TOP_K, tm, d), F32), pltpu.SemaphoreType.DMA(())]),
        compiler_params=_cparams(("arbitrary",)),
        name="moe_combine",
    )(dest, yb, x, shared, wk_t, gate)


def moe_layer(x, g, sc, sh, gate, router_w, router_bias, w_gate, w_up, w_down, sh_gate, sh_up, sh_down,
              layer):
    t, d = x.shape
    ne = N_EXPERTS
    hf, eidx, rank, wk, cnt = moe_router(x, g, sc, sh, router_w, router_bias)
    counts = cnt[:, 0].astype(jnp.int32)
    padded = (counts + MOE_BLOCK - 1) // MOE_BLOCK * MOE_BLOCK
    pad_end = jnp.cumsum(padded)
    pad_start = pad_end - padded
    n_slots = (t * TOP_K + ne * (MOE_BLOCK - 1) + MOE_BLOCK - 1) // MOE_BLOCK * MOE_BLOCK
    nblk = n_slots // MOE_BLOCK
    dest = (pad_start[eidx[:TOP_K]] + rank[:TOP_K]).reshape(TOP_K * t)
    blk_start = jnp.arange(nblk, dtype=jnp.int32) * MOE_BLOCK
    n_active = (pad_end[-1] // MOE_BLOCK).astype(jnp.int32).reshape(1)
    blk_e = jnp.minimum(jnp.searchsorted(pad_end, blk_start, side="right"), ne - 1).astype(jnp.int32)
    last_e = blk_e[jnp.maximum(n_active[0] - 1, 0)]
    blk_e = jnp.where(blk_start < pad_end[-1], blk_e, last_e)

    xb = moe_dispatch(dest, hf, n_slots)
    yb = moe_experts(blk_e, n_active, xb, w_gate, w_up, w_down, layer)
    shared = shared_expert(hf, sh_gate, sh_up, sh_down)
    return moe_combine(dest, yb, x, shared, wk.T, gate)


def kernel(x, c, ada_w, ada_b, norm_mix_g, norm_ffn_g, hy_w_in, hy_conv_w, hy_conv_b, hy_f_w0, hy_f_b0, hy_f_w1, hy_f_b1, hy_f_w2, hy_f_b2, hy_f_freq, hy_f_wout, hy_skip, hy_w_out, ml_w_in, ml_conv_w, ml_conv_b, ml_gate_b, ml_head_g, ml_w_out, moe_router_w, moe_router_bias, moe_w_gate, moe_w_up, moe_w_down, sh_w_gate, sh_w_up, sh_w_down, final_g):
    bsz, seq, d = x.shape
    assert bsz == 1, "kernels are written for a single sequence"
    depth = ada_w.shape[0]
    xs = x.reshape(seq, d)
    ada = ada_proj(c, ada_w, ada_b)
    for layer in range(depth):
        sh_m, sc_m, g_m, sh_f, sc_f, g_f = (ada[layer, :, k * d:(k + 1) * d] for k in range(6))
        hm = normmod(xs, norm_mix_g[layer].reshape(1, d), sc_m, sh_m)
        j = layer // 2
        if layer % 2 == 0:
            xs = hyena_mixer(hm, xs, g_m, hy_w_in[j], hy_conv_w[j], hy_conv_b[j], hy_f_w0[j], hy_f_b0[j],
                             hy_f_w1[j], hy_f_b1[j], hy_f_w2[j], hy_f_b2[j], hy_f_freq[j], hy_f_wout[j],
                             hy_skip[j], hy_w_out[j])
        else:
            xs = mlstm_mixer(hm, xs, g_m, ml_w_in[j], ml_conv_w[j], ml_conv_b[j], ml_gate_b[j],
                             ml_head_g[j], ml_w_out[j])
        xs = moe_layer(xs, norm_ffn_g[layer].reshape(1, d), sc_f, sh_f, g_f, moe_router_w[layer],
                       moe_router_bias[layer], moe_w_gate, moe_w_up, moe_w_down,
                       sh_w_gate[layer], sh_w_up[layer], sh_w_down[layer], layer)
    return rmsnorm(xs, final_g.reshape(1, d)).reshape(bsz, seq, d)
```

```python
import functools
import math

import numpy as np
import jax
import jax.numpy as jnp
from jax import lax
from jax.experimental import pallas as pl
from jax.experimental.pallas import tpu as pltpu

F32 = jnp.float32
BF16 = jnp.bfloat16
EPS = 1e-6

V7X_VMEM_BYTES = 64 * 1024 * 1024
VMEM_LIMIT = V7X_VMEM_BYTES - 8 * 1024 * 1024

SHORT_CONV = 3
HY_BANDS = 16
HY_DECAY_TARGET = 1e-2
HY_FAST_DECAY = 0.3
HY_SLOW_DECAY = 1.5
ML_HEADS = 8
ML_CHUNK = 256
N_EXPERTS = 64
TOP_K = 6
N_GROUPS = 8
TOPK_GROUPS = 4
ROUTED_SCALE = 2.5
MOE_BLOCK = 256
DFT_NB = 128


def _cparams(sem):
    return pltpu.CompilerParams(dimension_semantics=sem, vmem_limit_bytes=VMEM_LIMIT)


ADA_SPLIT = 4


def _ada_kernel(c_ref, *refs, kc):
    w_refs, (b_ref, o_ref, cs_ref) = refs[:ADA_SPLIT], refs[ADA_SPLIT:]
    tn = o_ref.shape[-1]
    rows = w_refs[0].shape[1]

    @pl.when(jnp.logical_and(pl.program_id(0) == 0, pl.program_id(1) == 0))
    def _():
        c = c_ref[...]
        cs_ref[...] = c * jax.nn.sigmoid(c)

    acc = jnp.zeros((8, tn), F32)
    for s, w_ref in enumerate(w_refs):
        def body(i, acc, s=s, w_ref=w_ref):
            r = pl.multiple_of(i * kc, kc)
            wc = w_ref[0, pl.ds(r, kc), :] * cs_ref[pl.ds(s * rows + r, kc), :]
            return acc + wc.reshape(kc // 8, 8, tn).sum(axis=0)
        acc = lax.fori_loop(0, rows // kc, body, acc)
    o_ref[0] = jnp.sum(acc, axis=0, keepdims=True) + b_ref[0]


def ada_proj(c, ada_w, ada_b):
    depth, d, n = ada_w.shape
    tn = 1024
    rows = d // ADA_SPLIT
    w_specs = [pl.BlockSpec((1, rows, tn), lambda l, j, s=s: (l, s, j)) for s in range(ADA_SPLIT)]
    return pl.pallas_call(
        functools.partial(_ada_kernel, kc=256),
        out_shape=jax.ShapeDtypeStruct((depth, 1, n), F32),
        grid=(depth, n // tn),
        in_specs=[pl.BlockSpec((d, 1), lambda l, j: (0, 0))] + w_specs + [
            pl.BlockSpec((1, 1, tn), lambda l, j: (l, 0, j))],
        out_specs=pl.BlockSpec((1, 1, tn), lambda l, j: (l, 0, j)),
        scratch_shapes=[pltpu.VMEM((d, 1), F32)],
        compiler_params=_cparams(("arbitrary", "arbitrary")),
        name="ada_proj",
    )(c.reshape(d, 1), *([ada_w] * ADA_SPLIT), ada_b.reshape(depth, 1, n))


def _normmod(x, g, sc, sh):
    r = lax.rsqrt(jnp.mean(x * x, axis=-1, keepdims=True) + EPS)
    return (x * r) * g * (1.0 + sc) + sh


def _normmod_kernel(x_ref, g_ref, sc_ref, sh_ref, o_ref):
    o_ref[...] = _normmod(x_ref[...], g_ref[...], sc_ref[...], sh_ref[...]).astype(o_ref.dtype)


def normmod(x, g, sc, sh, out_dtype=BF16, tm=512):
    t, d = x.shape
    vec = pl.BlockSpec((1, d), lambda i: (0, 0))
    return pl.pallas_call(
        _normmod_kernel,
        out_shape=jax.ShapeDtypeStruct((t, d), out_dtype),
        grid=(t // tm,),
        in_specs=[pl.BlockSpec((tm, d), lambda i: (i, 0)), vec, vec, vec],
        out_specs=pl.BlockSpec((tm, d), lambda i: (i, 0)),
        compiler_params=_cparams(("parallel",)),
        name="normmod",
    )(x, g, sc, sh)


def _rmsnorm_kernel(x_ref, g_ref, o_ref):
    x = x_ref[...]
    r = lax.rsqrt(jnp.mean(x * x, axis=-1, keepdims=True) + EPS)
    o_ref[...] = (x * r) * g_ref[...]


def rmsnorm(x, g, tm=512):
    t, d = x.shape
    return pl.pallas_call(
        _rmsnorm_kernel,
        out_shape=jax.ShapeDtypeStruct((t, d), F32),
        grid=(t // tm,),
        in_specs=[pl.BlockSpec((tm, d), lambda i: (i, 0)), pl.BlockSpec((1, d), lambda i: (0, 0))],
        out_specs=pl.BlockSpec((tm, d), lambda i: (i, 0)),
        compiler_params=_cparams(("parallel",)),
        name="final_rmsnorm",
    )(x, g)


def _mm_kernel(a_ref, w_ref, o_ref, wb_ref):
    @pl.when(pl.program_id(1) == 0)
    def _():
        wb_ref[...] = w_ref[...].astype(BF16)

    o_ref[...] = jnp.dot(a_ref[...], wb_ref[...], preferred_element_type=F32).astype(o_ref.dtype)


def matmul_cols(a, w, n_out, out_dtype=BF16, tm=1024, tn=512):
    m, k = a.shape
    tm = min(tm, m)
    return pl.pallas_call(
        _mm_kernel,
        out_shape=jax.ShapeDtypeStruct((m, n_out), out_dtype),
        grid=(n_out // tn, m // tm),
        in_specs=[pl.BlockSpec((tm, k), lambda j, i: (i, 0)),
                  pl.BlockSpec((k, tn), lambda j, i: (0, j))],
        out_specs=pl.BlockSpec((tm, tn), lambda j, i: (i, j)),
        scratch_shapes=[pltpu.VMEM((k, tn), BF16)],
        compiler_params=_cparams(("arbitrary", "arbitrary")),
        name="matmul_cols",
    )(a, w)


def _outproj_kernel(a_ref, x_ref, gate_ref, w_ref, o_ref, wb_ref):
    @pl.when(pl.program_id(1) == 0)
    def _():
        wb_ref[...] = w_ref[...].astype(BF16)

    y = jnp.dot(a_ref[...], wb_ref[...], preferred_element_type=F32)
    o_ref[...] = x_ref[...] + gate_ref[...] * y


def outproj_residual(a, x, gate, w, tm=1024, tn=512):
    t, k = a.shape
    d = w.shape[1]
    tm = min(tm, t)
    return pl.pallas_call(
        _outproj_kernel,
        out_shape=jax.ShapeDtypeStruct((t, d), F32),
        grid=(d // tn, t // tm),
        in_specs=[pl.BlockSpec((tm, k), lambda j, i: (i, 0)),
                  pl.BlockSpec((tm, tn), lambda j, i: (i, j)),
                  pl.BlockSpec((1, tn), lambda j, i: (0, j)),
                  pl.BlockSpec((k, tn), lambda j, i: (0, j))],
        out_specs=pl.BlockSpec((tm, tn), lambda j, i: (i, j)),
        scratch_shapes=[pltpu.VMEM((k, tn), BF16)],
        compiler_params=_cparams(("arbitrary", "arbitrary")),
        name="outproj_residual",
    )(a, x, gate, w)


HALO = 16


def _conv3(main, prev, nxt, w, b, i, nt):
    tm = main.shape[0]
    rows = lax.broadcasted_iota(jnp.int32, main.shape, 0)
    pr = jnp.where(i > 0, prev[HALO - 1:HALO, :], 0.0)
    nx = jnp.where(i < nt - 1, nxt[0:1, :], 0.0)
    up = jnp.where(rows == 0, pr, pltpu.roll(main, 1, 0))
    dn = jnp.where(rows == tm - 1, nx, pltpu.roll(main, tm - 1, 0))
    return up * w[0:1, :] + main * w[1:2, :] + dn * w[2:3, :] + b


def _halo_specs(tm, tc, t, col_fn):
    nh = t // HALO
    r = tm // HALO
    return [
        pl.BlockSpec((tm, tc), lambda i, j: (i, col_fn(j))),
        pl.BlockSpec((HALO, tc), lambda i, j: (jnp.maximum(i * r - 1, 0), col_fn(j))),
        pl.BlockSpec((HALO, tc), lambda i, j: (jnp.minimum((i + 1) * r, nh - 1), col_fn(j))),
    ]


def _hy_gate_kernel(*refs):
    (x0m, x0p, x0n, x1m, x1p, x1n, vm, vp, vn,
     w0, w1, w2, b0, b1, b2, vg_ref, x0_ref) = refs
    i = pl.program_id(0)
    nt = pl.num_programs(0)
    f = lambda r: r[...].astype(F32)
    x0 = _conv3(f(x0m), f(x0p), f(x0n), w0[...], b0[...], i, nt)
    x1 = _conv3(f(x1m), f(x1p), f(x1n), w1[...], b1[...], i, nt)
    v = _conv3(f(vm), f(vp), f(vn), w2[...], b2[...], i, nt)
    vg_ref[...] = (v * x1).astype(vg_ref.dtype)
    x0_ref[...] = x0.astype(x0_ref.dtype)


def hy_gate(p, conv_w, conv_b, tm=512, tc=512):
    t, c3 = p.shape
    d = c3 // 3
    nb = d // tc
    specs = []
    for g in range(3):
        specs += _halo_specs(tm, tc, t, lambda j, g=g: g * nb + j)
    for g in range(3):
        specs.append(pl.BlockSpec((SHORT_CONV, tc), lambda i, j, g=g: (0, g * nb + j)))
    for g in range(3):
        specs.append(pl.BlockSpec((1, tc), lambda i, j, g=g: (0, g * nb + j)))
    out_spec = pl.BlockSpec((tm, tc), lambda i, j: (i, j))
    cb = conv_b.reshape(1, c3)
    return pl.pallas_call(
        _hy_gate_kernel,
        out_shape=(jax.ShapeDtypeStruct((t, d), BF16), jax.ShapeDtypeStruct((t, d), BF16)),
        grid=(t // tm, nb),
        in_specs=specs,
        out_specs=(out_spec, out_spec),
        compiler_params=_cparams(("parallel", "parallel")),
        name="hy_gate",
    )(p, p, p, p, p, p, p, p, p, conv_w, conv_w, conv_w, cb, cb, cb)


def _hy_features(seq):
    t = np.linspace(0.0, 1.0, seq, dtype=np.float64)[:, None]
    ang = 2.0 * math.pi * np.arange(seq, dtype=np.float64)[:, None] / seq
    bands = np.linspace(1e-4, HY_BANDS - 1, HY_BANDS, dtype=np.float64)
    z = np.concatenate([t, np.cos(ang * bands), -np.sin(ang * bands)], axis=-1)
    zp = np.zeros((seq, 128), np.float32)
    zp[:, :z.shape[1]] = z
    return zp


def _hy_deltas(d):
    return np.abs(np.linspace(math.log(HY_DECAY_TARGET) / HY_FAST_DECAY,
                              math.log(HY_DECAY_TARGET) / HY_SLOW_DECAY, d,
                              dtype=np.float64)).astype(np.float32)[None, :]


def _hy_filter_kernel(z_ref, w0_ref, b0_ref, w1_ref, b1_ref, w2_ref, b2_ref, fr_ref, wo_ref,
                      dl_ref, o_ref, *, seq):
    hi = lax.Precision.HIGHEST
    fr = fr_ref[...]
    a = jnp.sin(fr * (jnp.dot(z_ref[...], w0_ref[...], precision=hi, preferred_element_type=F32)
                      + b0_ref[...]))
    a = jnp.sin(fr * (jnp.dot(a, w1_ref[...], precision=hi, preferred_element_type=F32) + b1_ref[...]))
    a = jnp.sin(fr * (jnp.dot(a, w2_ref[...], precision=hi, preferred_element_type=F32) + b2_ref[...]))
    k = jnp.dot(a.astype(BF16), wo_ref[...].astype(BF16), preferred_element_type=F32)
    tl, d2 = k.shape
    d = d2 // 2
    row = pl.program_id(0) * tl + lax.broadcasted_iota(jnp.int32, (tl, 1), 0)
    tpos = row.astype(F32) * (1.0 / (seq - 1))
    window = jnp.exp(-tpos * dl_ref[...])
    o_ref[:, :d] = (k[:, :d] * window).astype(o_ref.dtype)
    o_ref[:, d:] = jnp.where(row == 0, 0.0, k[:, d:] * window).astype(o_ref.dtype)


def hy_filter(seq, w0, b0, w1, b1, w2, b2, freq, wout, tl=256):
    fw = w1.shape[0]
    d2 = wout.shape[1]
    z = jnp.asarray(_hy_features(seq))
    w0p = jnp.zeros((128, fw), F32).at[:w0.shape[0]].set(w0)
    full = lambda shape: pl.BlockSpec(shape, lambda i: (0,) * len(shape))
    r = lambda v: v.reshape(1, -1)
    return pl.pallas_call(
        functools.partial(_hy_filter_kernel, seq=seq),
        out_shape=jax.ShapeDtypeStruct((seq, d2), BF16),
        grid=(seq // tl,),
        in_specs=[pl.BlockSpec((tl, 128), lambda i: (i, 0)),
                  full((128, fw)), full((1, fw)), full((fw, fw)), full((1, fw)),
                  full((fw, fw)), full((1, fw)), full((1, fw)), full((fw, d2)), full((1, d2 // 2))],
        out_specs=pl.BlockSpec((tl, d2), lambda i: (i, 0)),
        compiler_params=_cparams(("parallel",)),
        name="hy_filter",
    )(z, w0p, r(b0), w1, r(b1), w2, r(b2), r(freq), wout, jnp.asarray(_hy_deltas(d2 // 2)))


DFT_ROW_ALIGN = 16


def _dft_slabs(seq):
    nf = (2 * seq // DFT_NB) // 2 + 1
    return nf, -(-nf // DFT_ROW_ALIGN) * DFT_ROW_ALIGN


@functools.lru_cache(maxsize=None)
def _dft_tables(seq):
    n = 2 * seq
    nb = DFT_NB
    na = n // nb
    nf, nfp = _dft_slabs(seq)
    s1 = np.arange(na // 2)
    f1 = np.arange(nf)
    ph = 2.0 * math.pi * np.outer(f1, s1) / na
    fs = np.zeros((2 * nfp, na // 2))
    fs[:nf] = np.cos(ph)
    fs[nfp:nfp + nf] = -np.sin(ph)
    f2 = np.arange(nb)
    s2 = np.arange(nb)
    freq = f1[:, None, None] + na * f2[None, :, None]
    th = 2.0 * math.pi * ((freq * s2[None, None, :]) % n) / n
    gr, gi = np.cos(th), -np.sin(th)
    g = np.concatenate([np.concatenate([gr, -gi], axis=2),
                        np.concatenate([gi, gr], axis=2)], axis=1)
    ginv = np.transpose(g, (0, 2, 1))
    t1 = np.arange(na // 2)
    ph2 = 2.0 * math.pi * np.outer(t1, f1) / na
    wgt = np.full((nf,), 2.0)
    wgt[0] = wgt[nf - 1] = 1.0
    finv = np.zeros((na // 2, 2 * nfp))
    finv[:, :nf] = np.cos(ph2) * wgt
    finv[:, nfp:nfp + nf] = -np.sin(ph2) * wgt
    as_bf16 = lambda a: jnp.asarray(a.astype(np.float32)).astype(BF16)
    return as_bf16(fs), as_bf16(g), as_bf16(ginv), as_bf16(finv)


def _dft1_kernel(f_ref, x_ref, o_ref):
    y = jnp.dot(f_ref[...], x_ref[...], preferred_element_type=F32)
    o_ref[...] = y.reshape(o_ref.shape).astype(o_ref.dtype)


def dft_stage1(x2d, fs, tn=8192):
    k, cols = x2d.shape
    na = fs.shape[0] // 2
    return pl.pallas_call(
        _dft1_kernel,
        out_shape=jax.ShapeDtypeStruct((2, na, cols), BF16),
        grid=(cols // tn,),
        in_specs=[pl.BlockSpec((2 * na, k), lambda c: (0, 0)),
                  pl.BlockSpec((k, tn), lambda c: (0, c))],
        out_specs=pl.BlockSpec((2, na, tn), lambda c: (0, 0, c)),
        compiler_params=_cparams(("parallel",)),
        name="dft_stage1",
    )(fs, x2d)


def _hspec_kernel(g_ref, af_ref, ab_ref, o_ref, *, scale):
    nb2, dt = o_ref.shape[1], o_ref.shape[2]
    g = g_ref[0]
    hf = jnp.dot(g, af_ref[...].reshape(nb2, dt), preferred_element_type=F32)
    hb = jnp.dot(g, ab_ref[...].reshape(nb2, dt), preferred_element_type=F32)
    nb = nb2 // 2
    o_ref[0, :nb, :] = ((hf[:nb] + hb[:nb]) * scale).astype(o_ref.dtype)
    o_ref[0, nb:, :] = ((hf[nb:] - hb[nb:]) * scale).astype(o_ref.dtype)


def filter_spectrum(ak, g, d, n, dt=2048):
    nb = ak.shape[2]
    nf = g.shape[0]
    nd = d // dt
    scale = 1.0 / n
    return pl.pallas_call(
        functools.partial(_hspec_kernel, scale=scale),
        out_shape=jax.ShapeDtypeStruct((nf, 2 * nb, d), BF16),
        grid=(nf, nd),
        in_specs=[pl.BlockSpec((1, 2 * nb, 2 * nb), lambda f, j: (f, 0, 0)),
                  pl.BlockSpec((2, 1, nb, dt), lambda f, j: (0, f, 0, j)),
                  pl.BlockSpec((2, 1, nb, dt), lambda f, j: (0, f, 0, nd + j))],
        out_specs=pl.BlockSpec((1, 2 * nb, dt), lambda f, j: (f, 0, j)),
        compiler_params=_cparams(("parallel", "parallel")),
        name="filter_spectrum",
    )(g, ak, ak)


def _xspec_kernel(g_ref, gi_ref, a_ref, h_ref, o_ref, *, nf):
    nb2, dt = h_ref.shape[1], h_ref.shape[2]
    nb = nb2 // 2
    live = pl.program_id(0) < nf

    @pl.when(live)
    def _():
        x = jnp.dot(g_ref[0], a_ref[...].reshape(nb2, dt), preferred_element_type=F32)
        h = h_ref[0].astype(F32)
        xr, xi, hr, hi = x[:nb], x[nb:], h[:nb], h[nb:]
        y = jnp.concatenate([xr * hr - xi * hi, xr * hi + xi * hr], axis=0).astype(BF16)
        b = jnp.dot(gi_ref[0], y, preferred_element_type=F32)
        o_ref[...] = b.reshape(o_ref.shape).astype(o_ref.dtype)

    @pl.when(jnp.logical_not(live))
    def _():
        o_ref[...] = jnp.zeros_like(o_ref)


def spectrum_product(a, h, g, ginv, dt=2048):
    _, nfp, nb, d = a.shape
    nf = g.shape[0]
    slab = lambda f: jnp.minimum(f, nf - 1)
    gspec = pl.BlockSpec((1, 2 * nb, 2 * nb), lambda f, j: (slab(f), 0, 0))
    aspec = pl.BlockSpec((2, 1, nb, dt), lambda f, j: (0, f, 0, j))
    return pl.pallas_call(
        functools.partial(_xspec_kernel, nf=nf),
        out_shape=jax.ShapeDtypeStruct((2, nfp, nb, d), BF16),
        grid=(nfp, d // dt),
        in_specs=[gspec, gspec, aspec, pl.BlockSpec((1, 2 * nb, dt), lambda f, j: (slab(f), 0, j))],
        out_specs=aspec,
        compiler_params=_cparams(("parallel", "parallel")),
        name="spectrum_product",
    )(g, ginv, a, h)


def _idft2_kernel(f_ref, b_ref, vg_ref, x0_ref, skip_ref, o_ref):
    conv = jnp.dot(f_ref[...], b_ref[...], preferred_element_type=F32)
    vg = vg_ref[...].astype(F32)
    o_ref[...] = ((conv + vg * skip_ref[...]) * x0_ref[...].astype(F32)).astype(o_ref.dtype)


def idft_stage2_mix(b2d, finv, vg2d, x02d, skip_t, tn=4096):
    k, cols = b2d.shape
    rows = finv.shape[0]
    return pl.pallas_call(
        _idft2_kernel,
        out_shape=jax.ShapeDtypeStruct((rows, cols), BF16),
        grid=(cols // tn,),
        in_specs=[pl.BlockSpec((rows, k), lambda c: (0, 0)),
                  pl.BlockSpec((k, tn), lambda c: (0, c)),
                  pl.BlockSpec((rows, tn), lambda c: (0, c)),
                  pl.BlockSpec((rows, tn), lambda c: (0, c)),
                  pl.BlockSpec((1, tn), lambda c: (0, 0))],
        out_specs=pl.BlockSpec((rows, tn), lambda c: (0, c)),
        compiler_params=_cparams(("parallel",)),
        name="idft_stage2_mix",
    )(finv, b2d, vg2d, x02d, skip_t)


def hyena_mixer(hm, x, gate, w_in, conv_w, conv_b, f_w0, f_b0, f_w1, f_b1, f_w2, f_b2, f_freq,
                f_wout, skip, w_out):
    seq, d = hm.shape
    nb = DFT_NB
    na = 2 * seq // nb
    fs, g, ginv, finv = _dft_tables(seq)
    p = matmul_cols(hm, w_in, 3 * d)
    vg, x0 = hy_gate(p, conv_w, conv_b)
    kf = hy_filter(seq, f_w0, f_b0, f_w1, f_b1, f_w2, f_b2, f_freq, f_wout)
    _, nfp = _dft_slabs(seq)
    ak = dft_stage1(kf.reshape(na // 2, nb * 2 * d), fs)
    h = filter_spectrum(ak.reshape(2, nfp, nb, 2 * d), g, d, 2 * seq)
    a = dft_stage1(vg.reshape(na // 2, nb * d), fs)
    b = spectrum_product(a.reshape(2, nfp, nb, d), h, g, ginv)
    tn = 2 * d
    skip_t = jnp.tile(skip.reshape(1, d), (1, tn // d))
    y = idft_stage2_mix(b.reshape(2 * nfp, nb * d), finv, vg.reshape(na // 2, nb * d),
                        x0.reshape(na // 2, nb * d), skip_t, tn=tn)
    return outproj_residual(y.reshape(seq, d), x, gate, w_out)


def _qk_prep_kernel(m_ref, p_ref, n_ref, w_ref, b_ref, s_ref, o_ref):
    i = pl.program_id(0)
    nt = pl.num_programs(0)
    f = lambda r: r[...].astype(F32)
    u = _conv3(f(m_ref), f(p_ref), f(n_ref), w_ref[...], b_ref[...], i, nt)
    o_ref[...] = (u * jax.nn.sigmoid(u) * s_ref[...]).astype(o_ref.dtype)


def qk_prep(p, conv_w, conv_b, qk_dim, dqk, tm=512, tc=512):
    t = p.shape[0]
    c = 2 * qk_dim
    scale = np.ones((1, c), np.float32)
    scale[:, qk_dim:] = 1.0 / math.sqrt(dqk)
    return pl.pallas_call(
        _qk_prep_kernel,
        out_shape=jax.ShapeDtypeStruct((t, c), BF16),
        grid=(t // tm, c // tc),
        in_specs=_halo_specs(tm, tc, t, lambda j: j) + [
            pl.BlockSpec((SHORT_CONV, tc), lambda i, j: (0, j)),
            pl.BlockSpec((1, tc), lambda i, j: (0, j)),
            pl.BlockSpec((1, tc), lambda i, j: (0, j))],
        out_specs=pl.BlockSpec((tm, tc), lambda i, j: (i, j)),
        compiler_params=_cparams(("parallel", "parallel")),
        name="qk_prep",
    )(p, p, p, conv_w, conv_b.reshape(1, c), jnp.asarray(scale))


def _log_sigmoid(x):
    return jnp.minimum(x, 0.0) - jnp.log1p(jnp.exp(-jnp.abs(x)))


def _gates_kernel(hm_ref, w_ref, wt_ref, b_ref, bt_ref, col_ref, row_ref, scal_ref):
    hi = lax.Precision.HIGHEST
    nh = ML_HEADS
    hm = hm_ref[...]
    cs = hm.shape[0]
    gt = jnp.dot(hm, w_ref[...].astype(BF16), preferred_element_type=F32) + b_ref[...]
    gtt = lax.dot_general(wt_ref[...].astype(BF16), hm, (((1,), (1,)), ((), ())),
                          preferred_element_type=F32) + bt_ref[...]
    r = lax.broadcasted_iota(jnp.int32, (cs, cs), 0)
    c = lax.broadcasted_iota(jnp.int32, (cs, cs), 1)
    lower = (r >= c).astype(F32)
    upper = (r <= c).astype(F32)
    i_f, f_f, i_b, f_b = (gt[:, k * nh:(k + 1) * nh] for k in range(4))
    lf_f, lf_b = _log_sigmoid(f_f), _log_sigmoid(f_b)
    b_f = jnp.dot(lower, lf_f, precision=hi, preferred_element_type=F32)
    b_b = jnp.dot(upper, lf_b, precision=hi, preferred_element_type=F32)
    g_f = jnp.sum(lf_f, axis=0, keepdims=True)
    g_b = jnp.sum(lf_b, axis=0, keepdims=True)
    a_f = g_f - b_f + i_f
    a_b = g_b - b_b + i_b
    col_ref[...] = jnp.concatenate([b_f, b_b, a_f, a_b], axis=1)
    scal_ref[0] = jnp.concatenate([g_f, g_b, jnp.max(a_f, axis=0, keepdims=True),
                                   jnp.max(a_b, axis=0, keepdims=True)], axis=1)
    i_ft, f_ft, i_bt, f_bt = (gtt[k * nh:(k + 1) * nh, :] for k in range(4))
    b_ft = jnp.dot(_log_sigmoid(f_ft), upper, precision=hi, preferred_element_type=F32)
    b_bt = jnp.dot(_log_sigmoid(f_bt), lower, precision=hi, preferred_element_type=F32)
    row_ref[...] = jnp.concatenate([b_ft, b_bt, i_ft, i_bt], axis=0)


def ml_gates(hm, w_g, gate_b, cs):
    t, d = hm.shape
    g4 = w_g.shape[1]
    nc = t // cs
    return pl.pallas_call(
        _gates_kernel,
        out_shape=(jax.ShapeDtypeStruct((t, g4), F32), jax.ShapeDtypeStruct((g4, t), F32),
                   jax.ShapeDtypeStruct((nc, 1, g4), F32)),
        grid=(nc,),
        in_specs=[pl.BlockSpec((cs, d), lambda c: (c, 0)),
                  pl.BlockSpec((d, g4), lambda c: (0, 0)),
                  pl.BlockSpec((g4, d), lambda c: (0, 0)),
                  pl.BlockSpec((1, g4), lambda c: (0, 0)),
                  pl.BlockSpec((g4, 1), lambda c: (0, 0))],
        out_specs=(pl.BlockSpec((cs, g4), lambda c: (c, 0)),
                   pl.BlockSpec((g4, cs), lambda c: (0, c)),
                   pl.BlockSpec((1, 1, g4), lambda c: (c, 0, 0))),
        compiler_params=_cparams(("parallel",)),
        name="ml_gates",
    )(hm, w_g, w_g.T, gate_b.reshape(1, g4), gate_b.reshape(g4, 1))


def _mlstm_head(q, k, v, b_col, a_col, b_row, li_row, g, m_loc, c_ref, n_ref, m_ref, idx, causal):
    cs = q.shape[0]
    c_st = c_ref[idx]
    n_st = n_ref[idx]
    m_st = m_ref[idx][:, 0:1]
    r = lax.broadcasted_iota(jnp.int32, (cs, cs), 0)
    s = lax.broadcasted_iota(jnp.int32, (cs, cs), 1)
    mask = (s <= r) if causal else (s >= r)
    dlog = jnp.where(mask, b_col - b_row + li_row, -jnp.inf)
    m_inter = b_col + m_st
    m_t = jnp.maximum(m_inter, jnp.max(dlog, axis=-1, keepdims=True))
    qk = lax.dot_general(q, k, (((1,), (1,)), ((), ())), preferred_element_type=F32)
    p = jnp.exp(dlog - m_t) * qk
    s_inter = jnp.exp(m_inter - m_t)
    qf = q.astype(F32)
    num = (s_inter * jnp.dot(q, c_st.astype(BF16), preferred_element_type=F32)
           + jnp.dot(p.astype(BF16), v, preferred_element_type=F32))
    den = s_inter * jnp.sum(qf * n_st, axis=-1, keepdims=True) + jnp.sum(p, axis=-1, keepdims=True)
    hout = num / jnp.maximum(jnp.abs(den), jnp.exp(-m_t))
    wgt = jnp.exp(a_col - m_loc)
    kw = k.astype(F32) * wgt
    c_loc = lax.dot_general(kw.astype(BF16), v, (((0,), (0,)), ((), ())), preferred_element_type=F32)
    n_loc = jnp.sum(kw, axis=0, keepdims=True)
    m_new = jnp.maximum(g + m_st, m_loc)
    s_prev = jnp.exp(g + m_st - m_new)
    s_loc = jnp.exp(m_loc - m_new)
    c_ref[idx] = s_prev * c_st + s_loc * c_loc
    n_ref[idx] = s_prev * n_st + s_loc * n_loc
    m_ref[idx] = jnp.broadcast_to(m_new, m_ref.shape[1:])
    return hout


def _mlstm_kernel(qkf_ref, qkb_ref, vf_ref, vb_ref, colf_ref, colb_ref, rowf_ref, rowb_ref,
                  scf_ref, scb_ref, of_ref, ob_ref, c_ref, n_ref, m_ref, *, dqk, dv):
    nh = ML_HEADS

    @pl.when(pl.program_id(0) == 0)
    def _():
        c_ref[...] = jnp.zeros_like(c_ref)
        n_ref[...] = jnp.zeros_like(n_ref)
        m_ref[...] = jnp.zeros_like(m_ref)

    qkd = nh * dqk
    for direction, (qk_ref, v_ref, col_ref, row_ref, sc_ref, o_ref) in enumerate(
            ((qkf_ref, vf_ref, colf_ref, rowf_ref, scf_ref, of_ref),
             (qkb_ref, vb_ref, colb_ref, rowb_ref, scb_ref, ob_ref))):
        for h in range(nh):
            gi = direction * nh + h
            q = qk_ref[:, h * dqk:(h + 1) * dqk]
            k = qk_ref[:, qkd + h * dqk:qkd + (h + 1) * dqk]
            v = v_ref[:, h * dv:(h + 1) * dv]
            b_col = col_ref[:, gi:gi + 1]
            a_col = col_ref[:, 2 * nh + gi:2 * nh + gi + 1]
            b_row = row_ref[gi:gi + 1, :]
            li_row = row_ref[2 * nh + gi:2 * nh + gi + 1, :]
            g = sc_ref[0, :, gi:gi + 1]
            m_loc = sc_ref[0, :, 2 * nh + gi:2 * nh + gi + 1]
            o_ref[:, h * dv:(h + 1) * dv] = _mlstm_head(
                q, k, v, b_col, a_col, b_row, li_row, g, m_loc, c_ref, n_ref, m_ref, gi,
                causal=(direction == 0))


def mlstm_bidir(qk, p, col, row, scal, cs, dqk, dv):
    t = qk.shape[0]
    nh = ML_HEADS
    nc = t // cs
    qkd2 = 2 * nh * dqk
    vd = nh * dv
    vblk = qkd2 // vd
    g4 = col.shape[1]
    fwd = lambda c: c
    bwd = lambda c: nc - 1 - c
    mk = lambda fn: dict(
        qk=pl.BlockSpec((cs, qkd2), lambda c: (fn(c), 0)),
        v=pl.BlockSpec((cs, vd), lambda c: (fn(c), vblk)),
        col=pl.BlockSpec((cs, g4), lambda c: (fn(c), 0)),
        row=pl.BlockSpec((g4, cs), lambda c: (0, fn(c))),
        sc=pl.BlockSpec((1, 1, g4), lambda c: (fn(c), 0, 0)),
        o=pl.BlockSpec((cs, vd), lambda c: (fn(c), 0)))
    sf, sb = mk(fwd), mk(bwd)
    return pl.pallas_call(
        functools.partial(_mlstm_kernel, dqk=dqk, dv=dv),
        out_shape=(jax.ShapeDtypeStruct((t, vd), F32), jax.ShapeDtypeStruct((t, vd), F32)),
        grid=(nc,),
        in_specs=[sf["qk"], sb["qk"], sf["v"], sb["v"], sf["col"], sb["col"], sf["row"], sb["row"],
                  sf["sc"], sb["sc"]],
        out_specs=(sf["o"], sb["o"]),
        scratch_shapes=[pltpu.VMEM((2 * nh, dqk, dv), F32), pltpu.VMEM((2 * nh, 1, dqk), F32),
                        pltpu.VMEM((2 * nh, 1, 128), F32)],
        compiler_params=_cparams(("arbitrary",)),
        name="mlstm_bidir",
    )(qk, qk, p, p, col, col, row, row, scal, scal)


def _ml_post_kernel(hf_ref, hb_ref, o_ref, hg_ref, out_ref, *, dv):
    nh = ML_HEADS
    for h in range(nh):
        sl = slice(h * dv, (h + 1) * dv)
        hs = hf_ref[:, sl] + hb_ref[:, sl]
        hs = hs * lax.rsqrt(jnp.mean(hs * hs, axis=-1, keepdims=True) + EPS)
        og = jax.nn.sigmoid(o_ref[:, sl].astype(F32))
        out_ref[:, sl] = (hs * hg_ref[:, sl] * og).astype(out_ref.dtype)


def ml_post(hf, hb, p, head_g, dv, tm=512):
    t, vd = hf.shape
    oblk = p.shape[1] // vd - 1
    spec = pl.BlockSpec((tm, vd), lambda i: (i, 0))
    return pl.pallas_call(
        functools.partial(_ml_post_kernel, dv=dv),
        out_shape=jax.ShapeDtypeStruct((t, vd), BF16),
        grid=(t // tm,),
        in_specs=[spec, spec, pl.BlockSpec((tm, vd), lambda i: (i, oblk)),
                  pl.BlockSpec((1, vd), lambda i: (0, 0))],
        out_specs=spec,
        compiler_params=_cparams(("parallel",)),
        name="ml_post",
    )(hf, hb, p, head_g.reshape(1, vd))


def mlstm_mixer(hm, x, gate, w_in, conv_w, conv_b, gate_b, head_g, w_out):
    seq, d = hm.shape
    nh = ML_HEADS
    qk_dim = d // 2
    dqk = qk_dim // nh
    dv = d // nh
    n_main = 2 * qk_dim + 2 * d
    cs = min(ML_CHUNK, seq)
    p = matmul_cols(hm, w_in, n_main)
    col, row, scal = ml_gates(hm, w_in[:, n_main:], gate_b, cs)
    qk = qk_prep(p, conv_w, conv_b, qk_dim, dqk)
    hf, hb = mlstm_bidir(qk, p, col, row, scal, cs, dqk, dv)
    a = ml_post(hf, hb, p, head_g, dv)
    return outproj_residual(a, x, gate, w_out)


def _pack_halves(x):
    n = x.shape[1] // 2
    lo = lax.bitcast_convert_type(x[:, :n].astype(BF16).astype(F32), jnp.uint32)
    hi = lax.bitcast_convert_type(x[:, n:].astype(BF16).astype(F32), jnp.uint32)
    return (lo >> 16) | hi


def _unpack_halves(w):
    lo = lax.bitcast_convert_type(w << 16, F32)
    hi = lax.bitcast_convert_type(w & jnp.uint32(0xFFFF0000), F32)
    return lo, hi


def _unpack_bf16(w):
    lo, hi = _unpack_halves(w)
    return jnp.concatenate([lo, hi], axis=1).astype(BF16)


def _first_argmax(vals, axis, n):
    m = jnp.max(vals, axis=axis, keepdims=True)
    iota = lax.broadcasted_iota(jnp.int32, vals.shape, axis)
    idx = jnp.min(jnp.where(vals == m, iota, n), axis=axis, keepdims=True)
    return m, idx, iota


def _router_kernel(x_ref, g_ref, sc_ref, sh_ref, rwt_ref, rb_ref,
                   hf_ref, eidx_ref, rank_ref, wk_ref, cnt_ref, carry_ref):
    ne, ng = N_EXPERTS, N_GROUPS
    per = ne // ng
    tm = x_ref.shape[0]

    @pl.when(pl.program_id(0) == 0)
    def _():
        carry_ref[...] = jnp.zeros_like(carry_ref)

    hf = _normmod(x_ref[...], g_ref[...], sc_ref[...], sh_ref[...])
    hf_ref[...] = _pack_halves(hf)
    logits = lax.dot_general(rwt_ref[...], hf, (((1,), (1,)), ((), ())),
                             precision=lax.Precision.HIGHEST, preferred_element_type=F32)
    scores = jax.nn.sigmoid(logits)
    sel = scores + rb_ref[...]
    sel3 = sel.reshape(ng, per, tm)
    m1, i1, io3 = _first_argmax(sel3, 1, per)
    m2 = jnp.max(jnp.where(io3 == i1, -jnp.inf, sel3), axis=1, keepdims=True)
    gs = (m1 + m2).reshape(ng, tm)
    gsel = jnp.zeros((ng, tm), F32)
    for _ in range(TOPK_GROUPS):
        _, gi, iog = _first_argmax(gs, 0, ng)
        hit = iog == gi
        gsel = jnp.where(hit, 1.0, gsel)
        gs = jnp.where(hit, -jnp.inf, gs)
    gmask = jnp.broadcast_to(gsel.reshape(ng, 1, tm), (ng, per, tm)).reshape(ne, tm)
    cand = jnp.where(gmask > 0.5, sel, -jnp.inf)
    picked = []
    chosen = jnp.zeros((ne, tm), F32)
    for _ in range(TOP_K):
        _, ei, ioe = _first_argmax(cand, 0, ne)
        hit = ioe == ei
        picked.append((ei, hit))
        chosen = jnp.where(hit, 1.0, chosen)
        cand = jnp.where(hit, -jnp.inf, cand)
    r = lax.broadcasted_iota(jnp.int32, (tm, tm), 0)
    c = lax.broadcasted_iota(jnp.int32, (tm, tm), 1)
    before = (r < c).astype(BF16)
    ranks = jnp.dot(chosen.astype(BF16), before, preferred_element_type=F32) + carry_ref[:, 0:1]
    carry_ref[...] = carry_ref[...] + jnp.sum(chosen, axis=1, keepdims=True)
    cnt_ref[...] = carry_ref[...]
    wks = [jnp.sum(jnp.where(hit, scores, 0.0), axis=0, keepdims=True) for _, hit in picked]
    wsum = functools.reduce(lambda a, b: a + b, wks)
    eidx_ref[...] = jnp.zeros_like(eidx_ref)
    rank_ref[...] = jnp.zeros_like(rank_ref)
    wk_ref[...] = jnp.zeros_like(wk_ref)
    for j, ((ei, hit), wk) in enumerate(zip(picked, wks)):
        eidx_ref[j:j + 1, :] = ei
        rank_ref[j:j + 1, :] = jnp.sum(jnp.where(hit, ranks, 0.0), axis=0, keepdims=True).astype(jnp.int32)
        wk_ref[j:j + 1, :] = wk / wsum * ROUTED_SCALE


def moe_router(x, g, sc, sh, router_w, router_bias, tm=512):
    t, d = x.shape
    ne = N_EXPERTS
    vec = pl.BlockSpec((1, d), lambda i: (0, 0))
    lane = pl.BlockSpec((8, tm), lambda i: (0, i))
    return pl.pallas_call(
        _router_kernel,
        out_shape=(jax.ShapeDtypeStruct((t, d // 2), jnp.uint32), jax.ShapeDtypeStruct((8, t), jnp.int32),
                   jax.ShapeDtypeStruct((8, t), jnp.int32), jax.ShapeDtypeStruct((8, t), F32),
                   jax.ShapeDtypeStruct((ne, 128), F32)),
        grid=(t // tm,),
        in_specs=[pl.BlockSpec((tm, d), lambda i: (i, 0)), vec, vec, vec,
                  pl.BlockSpec((ne, d), lambda i: (0, 0)), pl.BlockSpec((ne, 1), lambda i: (0, 0))],
        out_specs=(pl.BlockSpec((tm, d // 2), lambda i: (i, 0)), lane, lane, lane,
                   pl.BlockSpec((ne, 128), lambda i: (0, 0))),
        scratch_shapes=[pltpu.VMEM((ne, 128), F32)],
        compiler_params=_cparams(("arbitrary",)),
        name="moe_router",
    )(x, g, sc, sh, router_w.T, router_bias.reshape(ne, 1))


def _plan_kernel(eidx_ref, rank_ref, ps_ref, dest_ref):
    ne = N_EXPERTS
    tm = eidx_ref.shape[1]
    io = lax.broadcasted_iota(jnp.int32, (ne, tm), 0)
    ps = ps_ref[...]
    dest_ref[...] = jnp.zeros_like(dest_ref)
    for j in range(TOP_K):
        hit = io == eidx_ref[j:j + 1, :]
        base = jnp.sum(jnp.where(hit, ps, 0.0), axis=0, keepdims=True)
        dest_ref[j:j + 1, :] = base.astype(jnp.int32) + rank_ref[j:j + 1, :]


def moe_plan(eidx, rank, pad_start, tm=2048):
    t = eidx.shape[1]
    tm = min(tm, t)
    lane = pl.BlockSpec((8, tm), lambda i: (0, i))
    return pl.pallas_call(
        _plan_kernel,
        out_shape=jax.ShapeDtypeStruct((8, t), jnp.int32),
        grid=(t // tm,),
        in_specs=[lane, lane, pl.BlockSpec((N_EXPERTS, 1), lambda i: (0, 0))],
        out_specs=lane,
        compiler_params=_cparams(("parallel",)),
        name="moe_plan",
    )(eidx, rank, pad_start.astype(F32).reshape(N_EXPERTS, 1))


SUBLANES = 8
_PAD_PIECES = tuple(1 << k for k in reversed(range(3, MOE_BLOCK.bit_length() - 1)))


def _dispatch_kernel(dest_ref, zs_ref, zn_ref, hf_ref, xb_ref, zbuf, sem, zsem, *, t_total):
    tm = hf_ref.shape[0]
    base = pl.program_id(0) * tm

    def row_copy(t, j):
        d = dest_ref[j * t_total + base + t]
        return pltpu.make_async_copy(hf_ref.at[pl.ds(t, 1)], xb_ref.at[pl.ds(d, 1)], sem)

    def start(t, carry):
        for j in range(TOP_K):
            row_copy(t, j).start()
        return carry

    def wait(t, carry):
        for j in range(TOP_K):
            row_copy(t, j).wait()
        return carry

    def fill(do_start):
        def body(e, carry):
            zs, zn = zs_ref[e], zn_ref[e]
            end = zs + zn

            def piece(src, dst):
                cp = pltpu.make_async_copy(src, dst, zsem)
                cp.start() if do_start else cp.wait()

            for r in range(SUBLANES - 1):
                @pl.when(r < (zn & (SUBLANES - 1)))
                def _():
                    piece(zbuf.at[pl.ds(0, 1)], xb_ref.at[pl.ds(zs + r, 1)])
            for p in _PAD_PIECES:
                @pl.when((zn & p) != 0)
                def _():
                    q = pl.multiple_of(end - (zn & ~(p - 1)), SUBLANES)
                    piece(zbuf.at[pl.ds(0, p)], xb_ref.at[pl.ds(q, p)])
            return carry
        lax.fori_loop(0, N_EXPERTS, body, 0)

    first = pl.program_id(0) == 0

    @pl.when(first)
    def _():
        zbuf[...] = jnp.zeros_like(zbuf)
        fill(True)

    lax.fori_loop(0, tm, start, 0)

    @pl.when(first)
    def _():
        fill(False)

    lax.fori_loop(0, tm, wait, 0)


def moe_dispatch(dest, zero_start, zero_len, hf, n_slots, tm=256):
    t, d = hf.shape
    return pl.pallas_call(
        functools.partial(_dispatch_kernel, t_total=t),
        out_shape=jax.ShapeDtypeStruct((n_slots, d), hf.dtype),
        grid_spec=pltpu.PrefetchScalarGridSpec(
            num_scalar_prefetch=3,
            grid=(t // tm,),
            in_specs=[pl.BlockSpec((tm, d), lambda i, *_: (i, 0))],
            out_specs=pl.BlockSpec(memory_space=pl.ANY),
            scratch_shapes=[pltpu.VMEM((MOE_BLOCK // 2, d), hf.dtype),
                            pltpu.SemaphoreType.DMA(()), pltpu.SemaphoreType.DMA(())]),
        compiler_params=_cparams(("arbitrary",)),
        name="moe_dispatch",
    )(dest, zero_start, zero_len, hf)


def _ffn(x, wg, wu, wd):
    hg = jnp.dot(x, wg, preferred_element_type=F32)
    hu = jnp.dot(x, wu, preferred_element_type=F32)
    h = (hg * jax.nn.sigmoid(hg) * hu).astype(BF16)
    return jnp.dot(h, wd, preferred_element_type=F32)


def _expert_kernel(be_ref, na_ref, x_ref, wg_ref, wu_ref, wd_ref, o_ref, wgb, wub, wdb):
    b = pl.program_id(0)
    prev = be_ref[jnp.maximum(b - 1, 0)]
    active = b < na_ref[0]

    @pl.when(jnp.logical_and(active, jnp.logical_or(b == 0, be_ref[b] != prev)))
    def _():
        wgb[...] = wg_ref[0, 0].astype(BF16)
        wub[...] = wu_ref[0, 0].astype(BF16)
        wdb[...] = wd_ref[0, 0].astype(BF16)

    @pl.when(active)
    def _():
        o_ref[...] = _pack_halves(_ffn(_unpack_bf16(x_ref[...]), wgb[...], wub[...], wdb[...]))


def moe_experts(blk_e, n_active, xb, w_gate, w_up, w_down, layer):
    p, dh = xb.shape
    d = 2 * dh
    ff = w_gate.shape[3]
    nblk = p // MOE_BLOCK
    row = lambda b, be, na: (jnp.minimum(b, na[0] - 1), 0)
    wsel = lambda b, be, na: (layer, be[b], 0, 0)
    return pl.pallas_call(
        _expert_kernel,
        out_shape=jax.ShapeDtypeStruct((p, dh), jnp.uint32),
        grid_spec=pltpu.PrefetchScalarGridSpec(
            num_scalar_prefetch=2,
            grid=(nblk,),
            in_specs=[pl.BlockSpec((MOE_BLOCK, dh), row),
                      pl.BlockSpec((1, 1, d, ff), wsel), pl.BlockSpec((1, 1, d, ff), wsel),
                      pl.BlockSpec((1, 1, ff, d), wsel)],
            out_specs=pl.BlockSpec((MOE_BLOCK, dh), row),
            scratch_shapes=[pltpu.VMEM((d, ff), BF16), pltpu.VMEM((d, ff), BF16),
                            pltpu.VMEM((ff, d), BF16)]),
        compiler_params=_cparams(("arbitrary",)),
        name="moe_experts",
    )(blk_e, n_active, xb, w_gate, w_up, w_down)


def _combine_kernel(dest_ref, yb_ref, x_ref, hf_ref, wk_ref, gate_ref, wg_ref, wu_ref, wd_ref, o_ref,
                    buf, wgb, wub, wdb, sem, *, t_total):
    tm, d = x_ref.shape
    dh = d // 2
    base = pl.program_id(0) * tm

    @pl.when(pl.program_id(0) == 0)
    def _():
        wgb[...] = wg_ref[...].astype(BF16)
        wub[...] = wu_ref[...].astype(BF16)
        wdb[...] = wd_ref[...].astype(BF16)

    def row_copy(t, j):
        dst = dest_ref[j * t_total + base + t]
        return pltpu.make_async_copy(yb_ref.at[pl.ds(dst, 1)], buf.at[j, pl.ds(t, 1)], sem)

    def start(t, carry):
        for j in range(TOP_K):
            row_copy(t, j).start()
        return carry

    def wait(t, carry):
        for j in range(TOP_K):
            row_copy(t, j).wait()
        return carry

    lax.fori_loop(0, tm, start, 0)
    shared = _ffn(_unpack_bf16(hf_ref[...]), wgb[...], wub[...], wdb[...])
    lax.fori_loop(0, tm, wait, 0)
    acc_lo, acc_hi = shared[:, :dh], shared[:, dh:]
    for j in range(TOP_K):
        lo, hi = _unpack_halves(buf[j])
        w = wk_ref[:, j:j + 1]
        acc_lo = acc_lo + w * lo
        acc_hi = acc_hi + w * hi
    o_ref[:, :dh] = x_ref[:, :dh] + gate_ref[:, :dh] * acc_lo
    o_ref[:, dh:] = x_ref[:, dh:] + gate_ref[:, dh:] * acc_hi


def moe_combine(dest, yb, x, hf, wk_t, gate, sh_gate, sh_up, sh_down, tm=256):
    t, d = x.shape
    dh = d // 2
    ff = sh_gate.shape[1]
    tm = min(tm, t)
    tile = lambda i, dest: (i, 0)
    full = lambda shape: pl.BlockSpec(shape, lambda i, dest: (0, 0))
    return pl.pallas_call(
        functools.partial(_combine_kernel, t_total=t),
        out_shape=jax.ShapeDtypeStruct((t, d), F32),
        grid_spec=pltpu.PrefetchScalarGridSpec(
            num_scalar_prefetch=1,
            grid=(t // tm,),
            in_specs=[pl.BlockSpec(memory_space=pl.ANY),
                      pl.BlockSpec((tm, d), tile), pl.BlockSpec((tm, dh), tile),
                      pl.BlockSpec((tm, 8), tile), full((1, d)),
                      full((d, ff)), full((d, ff)), full((ff, d))],
            out_specs=pl.BlockSpec((tm, d), tile),
            scratch_shapes=[pltpu.VMEM((TOP_K, tm, dh), jnp.uint32),
                            pltpu.VMEM((d, ff), BF16), pltpu.VMEM((d, ff), BF16), pltpu.VMEM((ff, d), BF16),
                            pltpu.SemaphoreType.DMA(())]),
        compiler_params=_cparams(("arbitrary",)),
        name="moe_combine",
    )(dest, yb, x, hf, wk_t, gate, sh_gate, sh_up, sh_down)


def moe_layer(x, g, sc, sh, gate, router_w, router_bias, w_gate, w_up, w_down, sh_gate, sh_up, sh_down,
              layer):
    t, d = x.shape
    ne = N_EXPERTS
    hf, eidx, rank, wk, cnt = moe_router(x, g, sc, sh, router_w, router_bias)
    counts = cnt[:, 0].astype(jnp.int32)
    padded = (counts + MOE_BLOCK - 1) // MOE_BLOCK * MOE_BLOCK
    pad_end = jnp.cumsum(padded)
    pad_start = pad_end - padded
    n_slots = (t * TOP_K + ne * (MOE_BLOCK - 1) + MOE_BLOCK - 1) // MOE_BLOCK * MOE_BLOCK
    nblk = n_slots // MOE_BLOCK
    dest = moe_plan(eidx, rank, pad_start)[:TOP_K].reshape(TOP_K * t)
    blk_start = jnp.arange(nblk, dtype=jnp.int32) * MOE_BLOCK
    n_active = (pad_end[-1] // MOE_BLOCK).astype(jnp.int32).reshape(1)
    blk_e = jnp.minimum(jnp.sum(blk_start[:, None] >= pad_end[None, :], axis=1), ne - 1).astype(jnp.int32)
    last_e = jnp.max(jnp.where(counts > 0, jnp.arange(ne, dtype=jnp.int32), 0))
    blk_e = jnp.where(blk_start < pad_end[-1], blk_e, last_e)

    xb = moe_dispatch(dest, pad_start + counts, padded - counts, hf, n_slots)
    yb = moe_experts(blk_e, n_active, xb, w_gate, w_up, w_down, layer)
    return moe_combine(dest, yb, x, hf, wk.T, gate, sh_gate, sh_up, sh_down)


def kernel(x, c, ada_w, ada_b, norm_mix_g, norm_ffn_g, hy_w_in, hy_conv_w, hy_conv_b, hy_f_w0, hy_f_b0, hy_f_w1, hy_f_b1, hy_f_w2, hy_f_b2, hy_f_freq, hy_f_wout, hy_skip, hy_w_out, ml_w_in, ml_conv_w, ml_conv_b, ml_gate_b, ml_head_g, ml_w_out, moe_router_w, moe_router_bias, moe_w_gate, moe_w_up, moe_w_down, sh_w_gate, sh_w_up, sh_w_down, final_g):
    bsz, seq, d = x.shape
    assert bsz == 1, "kernels are written for a single sequence"
    depth = ada_w.shape[0]
    xs = x.reshape(seq, d)
    ada = ada_proj(c, ada_w, ada_b)
    for layer in range(depth):
        sh_m, sc_m, g_m, sh_f, sc_f, g_f = (ada[layer, :, k * d:(k + 1) * d] for k in range(6))
        hm = normmod(xs, norm_mix_g[layer].reshape(1, d), sc_m, sh_m)
        j = layer // 2
        if layer % 2 == 0:
            xs = hyena_mixer(hm, xs, g_m, hy_w_in[j], hy_conv_w[j], hy_conv_b[j], hy_f_w0[j], hy_f_b0[j],
                             hy_f_w1[j], hy_f_b1[j], hy_f_w2[j], hy_f_b2[j], hy_f_freq[j], hy_f_wout[j],
                             hy_skip[j], hy_w_out[j])
        else:
            xs = mlstm_mixer(hm, xs, g_m, ml_w_in[j], ml_conv_w[j], ml_conv_b[j], ml_gate_b[j],
                             ml_head_g[j], ml_w_out[j])
        xs = moe_layer(xs, norm_ffn_g[layer].reshape(1, d), sc_f, sh_f, g_f, moe_router_w[layer],
                       moe_router_bias[layer], moe_w_gate, moe_w_up, moe_w_down,
                       sh_w_gate[layer], sh_w_up[layer], sh_w_down[layer], layer)
    return rmsnorm(xs, final_g.reshape(1, d)).reshape(bsz, seq, d)
```

```python
import functools
import math

import numpy as np
import jax
import jax.numpy as jnp
from jax import lax
from jax.experimental import pallas as pl
from jax.experimental.pallas import tpu as pltpu

F32 = jnp.float32
BF16 = jnp.bfloat16
EPS = 1e-6

V7X_VMEM_BYTES = 64 * 1024 * 1024
VMEM_LIMIT = V7X_VMEM_BYTES - 8 * 1024 * 1024

SHORT_CONV = 3
HY_BANDS = 16
HY_DECAY_TARGET = 1e-2
HY_FAST_DECAY = 0.3
HY_SLOW_DECAY = 1.5
ML_HEADS = 8
ML_CHUNK = 256
N_EXPERTS = 64
TOP_K = 6
N_GROUPS = 8
TOPK_GROUPS = 4
ROUTED_SCALE = 2.5
MOE_BLOCK = 256
DFT_NB = 128


def _cparams(sem):
    return pltpu.CompilerParams(dimension_semantics=sem, vmem_limit_bytes=VMEM_LIMIT)


ADA_SPLIT = 4


def _ada_kernel(c_ref, *refs, kc):
    w_refs, (b_ref, o_ref, cs_ref) = refs[:ADA_SPLIT], refs[ADA_SPLIT:]
    tn = o_ref.shape[-1]
    rows = w_refs[0].shape[1]

    @pl.when(jnp.logical_and(pl.program_id(0) == 0, pl.program_id(1) == 0))
    def _():
        c = c_ref[...]
        cs_ref[...] = c * jax.nn.sigmoid(c)

    acc = jnp.zeros((8, tn), F32)
    for s, w_ref in enumerate(w_refs):
        def body(i, acc, s=s, w_ref=w_ref):
            r = pl.multiple_of(i * kc, kc)
            wc = w_ref[0, pl.ds(r, kc), :] * cs_ref[pl.ds(s * rows + r, kc), :]
            return acc + wc.reshape(kc // 8, 8, tn).sum(axis=0)
        acc = lax.fori_loop(0, rows // kc, body, acc)
    o_ref[0] = jnp.sum(acc, axis=0, keepdims=True) + b_ref[0]


def ada_proj(c, ada_w, ada_b):
    depth, d, n = ada_w.shape
    tn = 1024
    rows = d // ADA_SPLIT
    w_specs = [pl.BlockSpec((1, rows, tn), lambda l, j, s=s: (l, s, j)) for s in range(ADA_SPLIT)]
    return pl.pallas_call(
        functools.partial(_ada_kernel, kc=256),
        out_shape=jax.ShapeDtypeStruct((depth, 1, n), F32),
        grid=(depth, n // tn),
        in_specs=[pl.BlockSpec((d, 1), lambda l, j: (0, 0))] + w_specs + [
            pl.BlockSpec((1, 1, tn), lambda l, j: (l, 0, j))],
        out_specs=pl.BlockSpec((1, 1, tn), lambda l, j: (l, 0, j)),
        scratch_shapes=[pltpu.VMEM((d, 1), F32)],
        compiler_params=_cparams(("arbitrary", "arbitrary")),
        name="ada_proj",
    )(c.reshape(d, 1), *([ada_w] * ADA_SPLIT), ada_b.reshape(depth, 1, n))


def _normmod(x, g, sc, sh):
    r = lax.rsqrt(jnp.mean(x * x, axis=-1, keepdims=True) + EPS)
    return (x * r) * g * (1.0 + sc) + sh


def _normmod_kernel(x_ref, g_ref, sc_ref, sh_ref, o_ref):
    o_ref[...] = _normmod(x_ref[...], g_ref[...], sc_ref[...], sh_ref[...]).astype(o_ref.dtype)


def normmod(x, g, sc, sh, out_dtype=BF16, tm=512):
    t, d = x.shape
    vec = pl.BlockSpec((1, d), lambda i: (0, 0))
    return pl.pallas_call(
        _normmod_kernel,
        out_shape=jax.ShapeDtypeStruct((t, d), out_dtype),
        grid=(t // tm,),
        in_specs=[pl.BlockSpec((tm, d), lambda i: (i, 0)), vec, vec, vec],
        out_specs=pl.BlockSpec((tm, d), lambda i: (i, 0)),
        compiler_params=_cparams(("parallel",)),
        name="normmod",
    )(x, g, sc, sh)


def _rmsnorm_kernel(x_ref, g_ref, o_ref):
    x = x_ref[...]
    r = lax.rsqrt(jnp.mean(x * x, axis=-1, keepdims=True) + EPS)
    o_ref[...] = (x * r) * g_ref[...]


def rmsnorm(x, g, tm=512):
    t, d = x.shape
    return pl.pallas_call(
        _rmsnorm_kernel,
        out_shape=jax.ShapeDtypeStruct((t, d), F32),
        grid=(t // tm,),
        in_specs=[pl.BlockSpec((tm, d), lambda i: (i, 0)), pl.BlockSpec((1, d), lambda i: (0, 0))],
        out_specs=pl.BlockSpec((tm, d), lambda i: (i, 0)),
        compiler_params=_cparams(("parallel",)),
        name="final_rmsnorm",
    )(x, g)


def _mm_kernel(a_ref, w_ref, o_ref, wb_ref):
    @pl.when(pl.program_id(1) == 0)
    def _():
        wb_ref[...] = w_ref[0].astype(BF16)

    o_ref[...] = jnp.dot(a_ref[...], wb_ref[...], preferred_element_type=F32).astype(o_ref.dtype)


def matmul_cols(a, w, layer, n_out, out_dtype=BF16, tm=1024, tn=512):
    m, k = a.shape
    tm = min(tm, m)
    return pl.pallas_call(
        _mm_kernel,
        out_shape=jax.ShapeDtypeStruct((m, n_out), out_dtype),
        grid=(n_out // tn, m // tm),
        in_specs=[pl.BlockSpec((tm, k), lambda j, i: (i, 0)),
                  pl.BlockSpec((1, k, tn), lambda j, i: (layer, 0, j))],
        out_specs=pl.BlockSpec((tm, tn), lambda j, i: (i, j)),
        scratch_shapes=[pltpu.VMEM((k, tn), BF16)],
        compiler_params=_cparams(("arbitrary", "arbitrary")),
        name="matmul_cols",
    )(a, w)


def _outproj_kernel(a_ref, x_ref, gate_ref, w_ref, o_ref, wb_ref):
    @pl.when(pl.program_id(1) == 0)
    def _():
        wb_ref[...] = w_ref[...].astype(BF16)

    y = jnp.dot(a_ref[...], wb_ref[...], preferred_element_type=F32)
    o_ref[...] = x_ref[...] + gate_ref[...] * y


def outproj_residual(a, x, gate, w, tm=1024, tn=512):
    t, k = a.shape
    d = w.shape[1]
    tm = min(tm, t)
    return pl.pallas_call(
        _outproj_kernel,
        out_shape=jax.ShapeDtypeStruct((t, d), F32),
        grid=(d // tn, t // tm),
        in_specs=[pl.BlockSpec((tm, k), lambda j, i: (i, 0)),
                  pl.BlockSpec((tm, tn), lambda j, i: (i, j)),
                  pl.BlockSpec((1, tn), lambda j, i: (0, j)),
                  pl.BlockSpec((k, tn), lambda j, i: (0, j))],
        out_specs=pl.BlockSpec((tm, tn), lambda j, i: (i, j)),
        scratch_shapes=[pltpu.VMEM((k, tn), BF16)],
        compiler_params=_cparams(("arbitrary", "arbitrary")),
        name="outproj_residual",
    )(a, x, gate, w)


HALO = 16


def _conv3(main, prev, nxt, w, b, i, nt):
    tm = main.shape[0]
    rows = lax.broadcasted_iota(jnp.int32, main.shape, 0)
    pr = jnp.where(i > 0, prev[HALO - 1:HALO, :], 0.0)
    nx = jnp.where(i < nt - 1, nxt[0:1, :], 0.0)
    up = jnp.where(rows == 0, pr, pltpu.roll(main, 1, 0))
    dn = jnp.where(rows == tm - 1, nx, pltpu.roll(main, tm - 1, 0))
    return up * w[0:1, :] + main * w[1:2, :] + dn * w[2:3, :] + b


def _halo_specs(tm, tc, t, col_fn):
    nh = t // HALO
    r = tm // HALO
    return [
        pl.BlockSpec((tm, tc), lambda i, j: (i, col_fn(j))),
        pl.BlockSpec((HALO, tc), lambda i, j: (jnp.maximum(i * r - 1, 0), col_fn(j))),
        pl.BlockSpec((HALO, tc), lambda i, j: (jnp.minimum((i + 1) * r, nh - 1), col_fn(j))),
    ]


HY_GROUP = 8


def _normmod_slab_kernel(x_ref, g_ref, sc_ref, sh_ref, perm_ref, o_ref):
    s1, grp, d = x_ref.shape
    x = x_ref[...].reshape(s1 * grp, d)
    y = _normmod(x, g_ref[...], sc_ref[...], sh_ref[...]).astype(BF16)
    o_ref[...] = jnp.dot(perm_ref[...], y, preferred_element_type=F32).astype(o_ref.dtype)


def normmod_slab(x, g, sc, sh):
    t, d = x.shape
    s1 = t // DFT_NB
    grp = HY_GROUP
    n = s1 * grp
    perm = np.zeros((n, n), np.float32)
    src = np.arange(n)
    perm[(src % grp) * s1 + src // grp, src] = 1.0
    vec = pl.BlockSpec((1, d), lambda i: (0, 0))
    return pl.pallas_call(
        _normmod_slab_kernel,
        out_shape=jax.ShapeDtypeStruct((t, d), BF16),
        grid=(DFT_NB // grp,),
        in_specs=[pl.BlockSpec((s1, grp, d), lambda i: (0, i, 0)), vec, vec, vec,
                  pl.BlockSpec((n, n), lambda i: (0, 0))],
        out_specs=pl.BlockSpec((n, d), lambda i: (i, 0)),
        compiler_params=_cparams(("parallel",)),
        name="normmod_slab",
    )(x.reshape(s1, DFT_NB, d), g, sc, sh, jnp.asarray(perm).astype(BF16))


def _shift_rows(x, down):
    n = x.shape[0]
    rows = lax.broadcasted_iota(jnp.int32, x.shape, 0)
    if down:
        return jnp.where(rows == 0, 0.0, pltpu.roll(x, 1, 0))
    return jnp.where(rows == n - 1, 0.0, pltpu.roll(x, n - 1, 0))


def _hy_gate_kernel(*refs):
    mains, prevs, nexts = refs[0:3], refs[3:6], refs[6:9]
    ws, bs = refs[9:12], refs[12:15]
    vg_ref, x0_ref = refs[15:17]
    i = pl.program_id(0)
    nt = pl.num_programs(0)
    s1 = prevs[0].shape[0]
    d = mains[0].shape[1]
    grp = mains[0].shape[0] // s1

    def halo(ref, edge, down):
        h = ref[...].astype(F32)
        return jnp.where(edge, _shift_rows(h, down), h)

    ups = [halo(prevs[g], i == 0, True) for g in range(3)]
    dns = [halo(nexts[g], i == nt - 1, False) for g in range(3)]
    for j in range(grp):
        out = []
        for g in range(3):
            slab = lambda k: mains[g][k * s1:(k + 1) * s1, :].astype(F32)
            up = ups[g] if j == 0 else slab(j - 1)
            dn = dns[g] if j == grp - 1 else slab(j + 1)
            w = ws[g][...]
            out.append(up * w[0:1, :] + slab(j) * w[1:2, :] + dn * w[2:3, :] + bs[g][...])
        x0, x1, v = out
        vg_ref[:, j * d:(j + 1) * d] = (v * x1).astype(vg_ref.dtype)
        x0_ref[:, j * d:(j + 1) * d] = x0.astype(x0_ref.dtype)


def hy_gate(p, conv_w, conv_b):
    t, c3 = p.shape
    d = c3 // 3
    s1 = t // DFT_NB
    grp = HY_GROUP
    main = lambda g: pl.BlockSpec((grp * s1, d), lambda i: (i, g))
    prev = lambda g: pl.BlockSpec((s1, d), lambda i: ((i * grp + DFT_NB - 1) % DFT_NB, g))
    nxt = lambda g: pl.BlockSpec((s1, d), lambda i: (((i + 1) * grp) % DFT_NB, g))
    specs = [f(g) for f in (main, prev, nxt) for g in range(3)]
    specs += [pl.BlockSpec((SHORT_CONV, d), lambda i, g=g: (0, g)) for g in range(3)]
    specs += [pl.BlockSpec((1, d), lambda i, g=g: (0, g)) for g in range(3)]
    out_spec = pl.BlockSpec((s1, grp * d), lambda i: (0, i))
    cb = conv_b.reshape(1, c3)
    out = jax.ShapeDtypeStruct((s1, DFT_NB * d), BF16)
    return pl.pallas_call(
        _hy_gate_kernel,
        out_shape=(out, out),
        grid=(DFT_NB // grp,),
        in_specs=specs,
        out_specs=(out_spec, out_spec),
        compiler_params=_cparams(("parallel",)),
        name="hy_gate",
    )(*([p] * 9), conv_w, conv_w, conv_w, cb, cb, cb)


def _hy_features(seq):
    t = np.linspace(0.0, 1.0, seq, dtype=np.float64)[:, None]
    ang = 2.0 * math.pi * np.arange(seq, dtype=np.float64)[:, None] / seq
    bands = np.linspace(1e-4, HY_BANDS - 1, HY_BANDS, dtype=np.float64)
    z = np.concatenate([t, np.cos(ang * bands), -np.sin(ang * bands)], axis=-1)
    zp = np.zeros((seq, 128), np.float32)
    zp[:, :z.shape[1]] = z
    s1 = seq // DFT_NB
    return zp.reshape(s1, DFT_NB, 128).transpose(1, 0, 2).reshape(seq, 128)


def _hy_deltas(d):
    return np.abs(np.linspace(math.log(HY_DECAY_TARGET) / HY_FAST_DECAY,
                              math.log(HY_DECAY_TARGET) / HY_SLOW_DECAY, d,
                              dtype=np.float64)).astype(np.float32)[None, :]


def _hy_filter_kernel(z_ref, w0_ref, b0_ref, w1_ref, b1_ref, w2_ref, b2_ref, fr_ref, wo_ref,
                      dl_ref, o_ref, *, seq, grp):
    hi = lax.Precision.HIGHEST
    fr = fr_ref[...]
    a = jnp.sin(fr * (jnp.dot(z_ref[...], w0_ref[...], precision=hi, preferred_element_type=F32)
                      + b0_ref[...]))
    a = jnp.sin(fr * (jnp.dot(a, w1_ref[...], precision=hi, preferred_element_type=F32) + b1_ref[...]))
    a = jnp.sin(fr * (jnp.dot(a, w2_ref[...], precision=hi, preferred_element_type=F32) + b2_ref[...]))
    k = jnp.dot(a.astype(BF16), wo_ref[...].astype(BF16), preferred_element_type=F32)
    d2 = k.shape[1]
    d = d2 // 2
    s1 = k.shape[0] // grp
    for j in range(grp):
        s2 = pl.program_id(0) * grp + j
        tt = lax.broadcasted_iota(jnp.int32, (s1, 1), 0) * DFT_NB + s2
        window = jnp.exp(-(tt.astype(F32) * (1.0 / (seq - 1))) * dl_ref[...])
        kj = k[j * s1:(j + 1) * s1]
        o_ref[:, j * d2:j * d2 + d] = (kj[:, :d] * window).astype(o_ref.dtype)
        o_ref[:, j * d2 + d:(j + 1) * d2] = jnp.where(tt == 0, 0.0, kj[:, d:] * window).astype(o_ref.dtype)


def hy_filter(seq, w0, b0, w1, b1, w2, b2, freq, wout, grp=4):
    fw = w1.shape[0]
    d2 = wout.shape[1]
    s1 = seq // DFT_NB
    z = jnp.asarray(_hy_features(seq))
    w0p = jnp.zeros((128, fw), F32).at[:w0.shape[0]].set(w0)
    full = lambda shape: pl.BlockSpec(shape, lambda i: (0,) * len(shape))
    r = lambda v: v.reshape(1, -1)
    return pl.pallas_call(
        functools.partial(_hy_filter_kernel, seq=seq, grp=grp),
        out_shape=jax.ShapeDtypeStruct((s1, DFT_NB * d2), BF16),
        grid=(DFT_NB // grp,),
        in_specs=[pl.BlockSpec((grp * s1, 128), lambda i: (i, 0)),
                  full((128, fw)), full((1, fw)), full((fw, fw)), full((1, fw)),
                  full((fw, fw)), full((1, fw)), full((1, fw)), full((fw, d2)), full((1, d2 // 2))],
        out_specs=pl.BlockSpec((s1, grp * d2), lambda i: (0, i)),
        compiler_params=_cparams(("parallel",)),
        name="hy_filter",
    )(z, w0p, r(b0), w1, r(b1), w2, r(b2), r(freq), wout, jnp.asarray(_hy_deltas(d2 // 2)))


DFT_ROW_ALIGN = 16


def _dft_slabs(seq):
    nf = (2 * seq // DFT_NB) // 2 + 1
    return nf, -(-nf // DFT_ROW_ALIGN) * DFT_ROW_ALIGN


@functools.lru_cache(maxsize=None)
def _dft_tables(seq):
    n = 2 * seq
    nb = DFT_NB
    na = n // nb
    nf, nfp = _dft_slabs(seq)
    s1 = np.arange(na // 2)
    f1 = np.arange(nf)
    ph = 2.0 * math.pi * np.outer(f1, s1) / na
    fs = np.zeros((2 * nfp, na // 2))
    fs[:nf] = np.cos(ph)
    fs[nfp:nfp + nf] = -np.sin(ph)
    f2 = np.arange(nb)
    s2 = np.arange(nb)
    freq = f1[:, None, None] + na * f2[None, :, None]
    th = 2.0 * math.pi * ((freq * s2[None, None, :]) % n) / n
    gr, gi = np.cos(th), -np.sin(th)
    g = np.concatenate([np.concatenate([gr, -gi], axis=2),
                        np.concatenate([gi, gr], axis=2)], axis=1)
    ginv = np.transpose(g, (0, 2, 1))
    t1 = np.arange(na // 2)
    ph2 = 2.0 * math.pi * np.outer(t1, f1) / na
    wgt = np.full((nf,), 2.0)
    wgt[0] = wgt[nf - 1] = 1.0
    finv = np.zeros((na // 2, 2 * nfp))
    finv[:, :nf] = np.cos(ph2) * wgt
    finv[:, nfp:nfp + nf] = -np.sin(ph2) * wgt
    return tuple(a.astype(np.float32) for a in (fs, g, ginv, finv))


def _dft1_kernel(f_ref, x_ref, o_ref):
    y = jnp.dot(f_ref[...].astype(BF16), x_ref[...], preferred_element_type=F32)
    o_ref[...] = y.reshape(o_ref.shape).astype(o_ref.dtype)


def dft_stage1(x2d, fs, tn=8192):
    k, cols = x2d.shape
    na = fs.shape[0] // 2
    return pl.pallas_call(
        _dft1_kernel,
        out_shape=jax.ShapeDtypeStruct((2, na, cols), BF16),
        grid=(cols // tn,),
        in_specs=[pl.BlockSpec((2 * na, k), lambda c: (0, 0)),
                  pl.BlockSpec((k, tn), lambda c: (0, c))],
        out_specs=pl.BlockSpec((2, na, tn), lambda c: (0, 0, c)),
        compiler_params=_cparams(("parallel",)),
        name="dft_stage1",
    )(fs, x2d)


def _hspec_kernel(g_ref, af_ref, ab_ref, o_ref, *, scale):
    nb2, dt = o_ref.shape[1], o_ref.shape[2]
    g = g_ref[0].astype(BF16)
    hf = jnp.dot(g, af_ref[...].reshape(nb2, dt), preferred_element_type=F32)
    hb = jnp.dot(g, ab_ref[...].reshape(nb2, dt), preferred_element_type=F32)
    nb = nb2 // 2
    o_ref[0, :nb, :] = ((hf[:nb] + hb[:nb]) * scale).astype(o_ref.dtype)
    o_ref[0, nb:, :] = ((hf[nb:] - hb[nb:]) * scale).astype(o_ref.dtype)


def filter_spectrum(ak, g, d, n, dt=2048):
    nb = ak.shape[2]
    nf = g.shape[0]
    nd = d // dt
    scale = 1.0 / n
    return pl.pallas_call(
        functools.partial(_hspec_kernel, scale=scale),
        out_shape=jax.ShapeDtypeStruct((nf, 2 * nb, d), BF16),
        grid=(nf, nd),
        in_specs=[pl.BlockSpec((1, 2 * nb, 2 * nb), lambda f, j: (f, 0, 0)),
                  pl.BlockSpec((2, 1, nb, dt), lambda f, j: (0, f, 0, j)),
                  pl.BlockSpec((2, 1, nb, dt), lambda f, j: (0, f, 0, nd + j))],
        out_specs=pl.BlockSpec((1, 2 * nb, dt), lambda f, j: (f, 0, j)),
        compiler_params=_cparams(("parallel", "parallel")),
        name="filter_spectrum",
    )(g, ak, ak)


def _xspec_kernel(g_ref, gi_ref, a_ref, h_ref, o_ref, *, nf):
    nb2, dt = h_ref.shape[1], h_ref.shape[2]
    nb = nb2 // 2
    live = pl.program_id(0) < nf

    @pl.when(live)
    def _():
        x = jnp.dot(g_ref[0].astype(BF16), a_ref[...].reshape(nb2, dt), preferred_element_type=F32)
        h = h_ref[0].astype(F32)
        xr, xi, hr, hi = x[:nb], x[nb:], h[:nb], h[nb:]
        y = jnp.concatenate([xr * hr - xi * hi, xr * hi + xi * hr], axis=0).astype(BF16)
        b = jnp.dot(gi_ref[0].astype(BF16), y, preferred_element_type=F32)
        o_ref[...] = b.reshape(o_ref.shape).astype(o_ref.dtype)

    @pl.when(jnp.logical_not(live))
    def _():
        o_ref[...] = jnp.zeros_like(o_ref)


def spectrum_product(a, h, g, ginv, dt=2048):
    _, nfp, nb, d = a.shape
    nf = g.shape[0]
    slab = lambda f: jnp.minimum(f, nf - 1)
    gspec = pl.BlockSpec((1, 2 * nb, 2 * nb), lambda f, j: (slab(f), 0, 0))
    aspec = pl.BlockSpec((2, 1, nb, dt), lambda f, j: (0, f, 0, j))
    return pl.pallas_call(
        functools.partial(_xspec_kernel, nf=nf),
        out_shape=jax.ShapeDtypeStruct((2, nfp, nb, d), BF16),
        grid=(nfp, d // dt),
        in_specs=[gspec, gspec, aspec, pl.BlockSpec((1, 2 * nb, dt), lambda f, j: (slab(f), 0, j))],
        out_specs=aspec,
        compiler_params=_cparams(("parallel", "parallel")),
        name="spectrum_product",
    )(g, ginv, a, h)


def _idft2_kernel(f_ref, b_ref, vg_ref, x0_ref, skip_ref, o_ref):
    conv = jnp.dot(f_ref[...].astype(BF16), b_ref[...], preferred_element_type=F32)
    vg = vg_ref[...].astype(F32)
    o_ref[...] = ((conv + vg * skip_ref[...]) * x0_ref[...].astype(F32)).astype(o_ref.dtype)


def idft_stage2_mix(b2d, finv, vg2d, x02d, skip_t, tn=4096):
    k, cols = b2d.shape
    rows = finv.shape[0]
    return pl.pallas_call(
        _idft2_kernel,
        out_shape=jax.ShapeDtypeStruct((rows, cols), BF16),
        grid=(cols // tn,),
        in_specs=[pl.BlockSpec((rows, k), lambda c: (0, 0)),
                  pl.BlockSpec((k, tn), lambda c: (0, c)),
                  pl.BlockSpec((rows, tn), lambda c: (0, c)),
                  pl.BlockSpec((rows, tn), lambda c: (0, c)),
                  pl.BlockSpec((1, tn), lambda c: (0, 0))],
        out_specs=pl.BlockSpec((rows, tn), lambda c: (0, c)),
        compiler_params=_cparams(("parallel",)),
        name="idft_stage2_mix",
    )(finv, b2d, vg2d, x02d, skip_t)


def _outproj_slab_kernel(a_ref, x_ref, gate_ref, w_ref, o_ref, wb_ref):
    @pl.when(pl.program_id(1) == 0)
    def _():
        wb_ref[...] = w_ref[...].astype(BF16)

    s1, grp = x_ref.shape[0], x_ref.shape[1]
    k = w_ref.shape[0]
    a = jnp.concatenate([a_ref[:, j * k:(j + 1) * k] for j in range(grp)], axis=0)
    y = jnp.dot(a, wb_ref[...], preferred_element_type=F32)
    for j in range(grp):
        o_ref[:, j, :] = x_ref[:, j, :] + gate_ref[...] * y[j * s1:(j + 1) * s1]


def outproj_residual_slab(a2d, x, gate, w, tn=512):
    t, d = x.shape
    k = w.shape[0]
    s1 = t // DFT_NB
    grp = HY_GROUP
    xspec = pl.BlockSpec((s1, grp, tn), lambda j, i: (0, i, j))
    out = pl.pallas_call(
        _outproj_slab_kernel,
        out_shape=jax.ShapeDtypeStruct((s1, DFT_NB, d), F32),
        grid=(d // tn, DFT_NB // grp),
        in_specs=[pl.BlockSpec((s1, grp * k), lambda j, i: (0, i)), xspec,
                  pl.BlockSpec((1, tn), lambda j, i: (0, j)),
                  pl.BlockSpec((k, tn), lambda j, i: (0, j))],
        out_specs=xspec,
        scratch_shapes=[pltpu.VMEM((k, tn), BF16)],
        compiler_params=_cparams(("arbitrary", "arbitrary")),
        name="outproj_residual_slab",
    )(a2d, x.reshape(s1, DFT_NB, d), gate, w)
    return out.reshape(t, d)


def hyena_mixer(hm, x, gate, w_in, layer, conv_w, conv_b, f_w0, f_b0, f_w1, f_b1, f_w2, f_b2, f_freq,
                f_wout, skip, w_out):
    seq, d = hm.shape
    nb = DFT_NB
    fs, g, ginv, finv = _dft_tables(seq)
    _, nfp = _dft_slabs(seq)
    p = matmul_cols(hm, w_in, layer, 3 * d)
    vg, x0 = hy_gate(p, conv_w, conv_b)
    kf = hy_filter(seq, f_w0, f_b0, f_w1, f_b1, f_w2, f_b2, f_freq, f_wout)
    ak = dft_stage1(kf, fs)
    h = filter_spectrum(ak.reshape(2, nfp, nb, 2 * d), g, d, 2 * seq)
    a = dft_stage1(vg, fs)
    b = spectrum_product(a.reshape(2, nfp, nb, d), h, g, ginv)
    tn = 2 * d
    skip_t = jnp.tile(skip.reshape(1, d), (1, tn // d))
    y = idft_stage2_mix(b.reshape(2 * nfp, nb * d), finv, vg, x0, skip_t, tn=tn)
    return outproj_residual_slab(y, x, gate, w_out)


def _qk_prep_kernel(m_ref, p_ref, n_ref, w_ref, b_ref, s_ref, o_ref):
    i = pl.program_id(0)
    nt = pl.num_programs(0)
    f = lambda r: r[...].astype(F32)
    u = _conv3(f(m_ref), f(p_ref), f(n_ref), w_ref[...], b_ref[...], i, nt)
    o_ref[...] = (u * jax.nn.sigmoid(u) * s_ref[...]).astype(o_ref.dtype)


def qk_prep(p, conv_w, conv_b, qk_dim, dqk, tm=512, tc=512):
    t = p.shape[0]
    c = 2 * qk_dim
    scale = np.ones((1, c), np.float32)
    scale[:, qk_dim:] = 1.0 / math.sqrt(dqk)
    return pl.pallas_call(
        _qk_prep_kernel,
        out_shape=jax.ShapeDtypeStruct((t, c), BF16),
        grid=(t // tm, c // tc),
        in_specs=_halo_specs(tm, tc, t, lambda j: j) + [
            pl.BlockSpec((SHORT_CONV, tc), lambda i, j: (0, j)),
            pl.BlockSpec((1, tc), lambda i, j: (0, j)),
            pl.BlockSpec((1, tc), lambda i, j: (0, j))],
        out_specs=pl.BlockSpec((tm, tc), lambda i, j: (i, j)),
        compiler_params=_cparams(("parallel", "parallel")),
        name="qk_prep",
    )(p, p, p, conv_w, conv_b.reshape(1, c), jnp.asarray(scale))


def _log_sigmoid(x):
    return jnp.minimum(x, 0.0) - jnp.log1p(jnp.exp(-jnp.abs(x)))


def _gates_kernel(hm_ref, w_ref, wt_ref, b_ref, bt_ref, col_ref, row_ref, scal_ref):
    hi = lax.Precision.HIGHEST
    nh = ML_HEADS
    hm = hm_ref[...]
    cs = hm.shape[0]
    gt = jnp.dot(hm, w_ref[...].astype(BF16), preferred_element_type=F32) + b_ref[...]
    gtt = lax.dot_general(wt_ref[...].astype(BF16), hm, (((1,), (1,)), ((), ())),
                          preferred_element_type=F32) + bt_ref[...]
    r = lax.broadcasted_iota(jnp.int32, (cs, cs), 0)
    c = lax.broadcasted_iota(jnp.int32, (cs, cs), 1)
    lower = (r >= c).astype(F32)
    upper = (r <= c).astype(F32)
    i_f, f_f, i_b, f_b = (gt[:, k * nh:(k + 1) * nh] for k in range(4))
    lf_f, lf_b = _log_sigmoid(f_f), _log_sigmoid(f_b)
    b_f = jnp.dot(lower, lf_f, precision=hi, preferred_element_type=F32)
    b_b = jnp.dot(upper, lf_b, precision=hi, preferred_element_type=F32)
    g_f = jnp.sum(lf_f, axis=0, keepdims=True)
    g_b = jnp.sum(lf_b, axis=0, keepdims=True)
    a_f = g_f - b_f + i_f
    a_b = g_b - b_b + i_b
    col_ref[...] = jnp.concatenate([b_f, b_b, a_f, a_b], axis=1)
    scal_ref[0] = jnp.concatenate([g_f, g_b, jnp.max(a_f, axis=0, keepdims=True),
                                   jnp.max(a_b, axis=0, keepdims=True)], axis=1)
    i_ft, f_ft, i_bt, f_bt = (gtt[k * nh:(k + 1) * nh, :] for k in range(4))
    b_ft = jnp.dot(_log_sigmoid(f_ft), upper, precision=hi, preferred_element_type=F32)
    b_bt = jnp.dot(_log_sigmoid(f_bt), lower, precision=hi, preferred_element_type=F32)
    row_ref[...] = jnp.concatenate([b_ft, b_bt, i_ft, i_bt], axis=0)


def ml_gates(hm, w_g, gate_b, cs):
    t, d = hm.shape
    g4 = w_g.shape[1]
    nc = t // cs
    return pl.pallas_call(
        _gates_kernel,
        out_shape=(jax.ShapeDtypeStruct((t, g4), F32), jax.ShapeDtypeStruct((g4, t), F32),
                   jax.ShapeDtypeStruct((nc, 1, g4), F32)),
        grid=(nc,),
        in_specs=[pl.BlockSpec((cs, d), lambda c: (c, 0)),
                  pl.BlockSpec((d, g4), lambda c: (0, 0)),
                  pl.BlockSpec((g4, d), lambda c: (0, 0)),
                  pl.BlockSpec((1, g4), lambda c: (0, 0)),
                  pl.BlockSpec((g4, 1), lambda c: (0, 0))],
        out_specs=(pl.BlockSpec((cs, g4), lambda c: (c, 0)),
                   pl.BlockSpec((g4, cs), lambda c: (0, c)),
                   pl.BlockSpec((1, 1, g4), lambda c: (c, 0, 0))),
        compiler_params=_cparams(("parallel",)),
        name="ml_gates",
    )(hm, w_g, w_g.T, gate_b.reshape(1, g4), gate_b.reshape(g4, 1))


def _mlstm_head(q, k, v, b_col, a_col, b_row, li_row, g, m_loc, c_ref, n_ref, m_ref, idx, causal):
    cs = q.shape[0]
    c_st = c_ref[idx]
    n_st = n_ref[idx]
    m_st = m_ref[idx][:, 0:1]
    r = lax.broadcasted_iota(jnp.int32, (cs, cs), 0)
    s = lax.broadcasted_iota(jnp.int32, (cs, cs), 1)
    mask = (s <= r) if causal else (s >= r)
    dlog = jnp.where(mask, b_col - b_row + li_row, -jnp.inf)
    m_inter = b_col + m_st
    m_t = jnp.maximum(m_inter, jnp.max(dlog, axis=-1, keepdims=True))
    qk = lax.dot_general(q, k, (((1,), (1,)), ((), ())), preferred_element_type=F32)
    p = jnp.exp(dlog - m_t) * qk
    s_inter = jnp.exp(m_inter - m_t)
    qf = q.astype(F32)
    num = (s_inter * jnp.dot(q, c_st.astype(BF16), preferred_element_type=F32)
           + jnp.dot(p.astype(BF16), v, preferred_element_type=F32))
    den = s_inter * jnp.sum(qf * n_st, axis=-1, keepdims=True) + jnp.sum(p, axis=-1, keepdims=True)
    hout = num / jnp.maximum(jnp.abs(den), jnp.exp(-m_t))
    wgt = jnp.exp(a_col - m_loc)
    kw = k.astype(F32) * wgt
    c_loc = lax.dot_general(kw.astype(BF16), v, (((0,), (0,)), ((), ())), preferred_element_type=F32)
    n_loc = jnp.sum(kw, axis=0, keepdims=True)
    m_new = jnp.maximum(g + m_st, m_loc)
    s_prev = jnp.exp(g + m_st - m_new)
    s_loc = jnp.exp(m_loc - m_new)
    c_ref[idx] = s_prev * c_st + s_loc * c_loc
    n_ref[idx] = s_prev * n_st + s_loc * n_loc
    m_ref[idx] = jnp.broadcast_to(m_new, m_ref.shape[1:])
    return hout


def _mlstm_kernel(qkf_ref, qkb_ref, vf_ref, vb_ref, colf_ref, colb_ref, rowf_ref, rowb_ref,
                  scf_ref, scb_ref, of_ref, ob_ref, c_ref, n_ref, m_ref, *, dqk, dv):
    nh = ML_HEADS

    @pl.when(pl.program_id(0) == 0)
    def _():
        c_ref[...] = jnp.zeros_like(c_ref)
        n_ref[...] = jnp.zeros_like(n_ref)
        m_ref[...] = jnp.zeros_like(m_ref)

    qkd = nh * dqk
    for direction, (qk_ref, v_ref, col_ref, row_ref, sc_ref, o_ref) in enumerate(
            ((qkf_ref, vf_ref, colf_ref, rowf_ref, scf_ref, of_ref),
             (qkb_ref, vb_ref, colb_ref, rowb_ref, scb_ref, ob_ref))):
        for h in range(nh):
            gi = direction * nh + h
            q = qk_ref[:, h * dqk:(h + 1) * dqk]
            k = qk_ref[:, qkd + h * dqk:qkd + (h + 1) * dqk]
            v = v_ref[:, h * dv:(h + 1) * dv]
            b_col = col_ref[:, gi:gi + 1]
            a_col = col_ref[:, 2 * nh + gi:2 * nh + gi + 1]
            b_row = row_ref[gi:gi + 1, :]
            li_row = row_ref[2 * nh + gi:2 * nh + gi + 1, :]
            g = sc_ref[0, :, gi:gi + 1]
            m_loc = sc_ref[0, :, 2 * nh + gi:2 * nh + gi + 1]
            o_ref[:, h * dv:(h + 1) * dv] = _mlstm_head(
                q, k, v, b_col, a_col, b_row, li_row, g, m_loc, c_ref, n_ref, m_ref, gi,
                causal=(direction == 0))


def mlstm_bidir(qk, p, col, row, scal, cs, dqk, dv):
    t = qk.shape[0]
    nh = ML_HEADS
    nc = t // cs
    qkd2 = 2 * nh * dqk
    vd = nh * dv
    vblk = qkd2 // vd
    g4 = col.shape[1]
    fwd = lambda c: c
    bwd = lambda c: nc - 1 - c
    mk = lambda fn: dict(
        qk=pl.BlockSpec((cs, qkd2), lambda c: (fn(c), 0)),
        v=pl.BlockSpec((cs, vd), lambda c: (fn(c), vblk)),
        col=pl.BlockSpec((cs, g4), lambda c: (fn(c), 0)),
        row=pl.BlockSpec((g4, cs), lambda c: (0, fn(c))),
        sc=pl.BlockSpec((1, 1, g4), lambda c: (fn(c), 0, 0)),
        o=pl.BlockSpec((cs, vd), lambda c: (fn(c), 0)))
    sf, sb = mk(fwd), mk(bwd)
    return pl.pallas_call(
        functools.partial(_mlstm_kernel, dqk=dqk, dv=dv),
        out_shape=(jax.ShapeDtypeStruct((t, vd), F32), jax.ShapeDtypeStruct((t, vd), F32)),
        grid=(nc,),
        in_specs=[sf["qk"], sb["qk"], sf["v"], sb["v"], sf["col"], sb["col"], sf["row"], sb["row"],
                  sf["sc"], sb["sc"]],
        out_specs=(sf["o"], sb["o"]),
        scratch_shapes=[pltpu.VMEM((2 * nh, dqk, dv), F32), pltpu.VMEM((2 * nh, 1, dqk), F32),
                        pltpu.VMEM((2 * nh, 1, 128), F32)],
        compiler_params=_cparams(("arbitrary",)),
        name="mlstm_bidir",
    )(qk, qk, p, p, col, col, row, row, scal, scal)


def _ml_post_kernel(hf_ref, hb_ref, o_ref, hg_ref, out_ref, *, dv):
    nh = ML_HEADS
    for h in range(nh):
        sl = slice(h * dv, (h + 1) * dv)
        hs = hf_ref[:, sl] + hb_ref[:, sl]
        hs = hs * lax.rsqrt(jnp.mean(hs * hs, axis=-1, keepdims=True) + EPS)
        og = jax.nn.sigmoid(o_ref[:, sl].astype(F32))
        out_ref[:, sl] = (hs * hg_ref[:, sl] * og).astype(out_ref.dtype)


def ml_post(hf, hb, p, head_g, dv, tm=512):
    t, vd = hf.shape
    oblk = p.shape[1] // vd - 1
    spec = pl.BlockSpec((tm, vd), lambda i: (i, 0))
    return pl.pallas_call(
        functools.partial(_ml_post_kernel, dv=dv),
        out_shape=jax.ShapeDtypeStruct((t, vd), BF16),
        grid=(t // tm,),
        in_specs=[spec, spec, pl.BlockSpec((tm, vd), lambda i: (i, oblk)),
                  pl.BlockSpec((1, vd), lambda i: (0, 0))],
        out_specs=spec,
        compiler_params=_cparams(("parallel",)),
        name="ml_post",
    )(hf, hb, p, head_g.reshape(1, vd))


def mlstm_mixer(hm, x, gate, w_in, layer, conv_w, conv_b, gate_b, head_g, w_out):
    seq, d = hm.shape
    nh = ML_HEADS
    qk_dim = d // 2
    dqk = qk_dim // nh
    dv = d // nh
    n_main = 2 * qk_dim + 2 * d
    cs = min(ML_CHUNK, seq)
    p = matmul_cols(hm, w_in, layer, n_main)
    col, row, scal = ml_gates(hm, w_in[layer, :, n_main:], gate_b, cs)
    qk = qk_prep(p, conv_w, conv_b, qk_dim, dqk)
    hf, hb = mlstm_bidir(qk, p, col, row, scal, cs, dqk, dv)
    a = ml_post(hf, hb, p, head_g, dv)
    return outproj_residual(a, x, gate, w_out)


def _first_argmax(vals, axis, n):
    m = jnp.max(vals, axis=axis, keepdims=True)
    iota = lax.broadcasted_iota(jnp.int32, vals.shape, axis)
    idx = jnp.min(jnp.where(vals == m, iota, n), axis=axis, keepdims=True)
    return m, idx, iota


def _router_kernel(x_ref, g_ref, sc_ref, sh_ref, rwt_ref, rb_ref,
                   hf_ref, eidx_ref, rank_ref, wk_ref, cnt_ref, carry_ref):
    ne, ng = N_EXPERTS, N_GROUPS
    per = ne // ng
    tm = x_ref.shape[0]

    @pl.when(pl.program_id(0) == 0)
    def _():
        carry_ref[...] = jnp.zeros_like(carry_ref)

    hf = _normmod(x_ref[...], g_ref[...], sc_ref[...], sh_ref[...])
    hf_ref[...] = hf
    logits = lax.dot_general(rwt_ref[...], hf, (((1,), (1,)), ((), ())),
                             precision=lax.Precision.HIGHEST, preferred_element_type=F32)
    scores = jax.nn.sigmoid(logits)
    sel = scores + rb_ref[...]
    sel3 = sel.reshape(ng, per, tm)
    m1, i1, io3 = _first_argmax(sel3, 1, per)
    m2 = jnp.max(jnp.where(io3 == i1, -jnp.inf, sel3), axis=1, keepdims=True)
    gs = (m1 + m2).reshape(ng, tm)
    gsel = jnp.zeros((ng, tm), F32)
    for _ in range(TOPK_GROUPS):
        _, gi, iog = _first_argmax(gs, 0, ng)
        hit = iog == gi
        gsel = jnp.where(hit, 1.0, gsel)
        gs = jnp.where(hit, -jnp.inf, gs)
    gmask = jnp.broadcast_to(gsel.reshape(ng, 1, tm), (ng, per, tm)).reshape(ne, tm)
    cand = jnp.where(gmask > 0.5, sel, -jnp.inf)
    picked = []
    chosen = jnp.zeros((ne, tm), F32)
    for _ in range(TOP_K):
        _, ei, ioe = _first_argmax(cand, 0, ne)
        hit = ioe == ei
        picked.append((ei, hit))
        chosen = jnp.where(hit, 1.0, chosen)
        cand = jnp.where(hit, -jnp.inf, cand)
    r = lax.broadcasted_iota(jnp.int32, (tm, tm), 0)
    c = lax.broadcasted_iota(jnp.int32, (tm, tm), 1)
    before = (r < c).astype(BF16)
    ranks = jnp.dot(chosen.astype(BF16), before, preferred_element_type=F32) + carry_ref[:, 0:1]
    carry_ref[...] = carry_ref[...] + jnp.sum(chosen, axis=1, keepdims=True)
    cnt_ref[...] = carry_ref[...]
    wks = [jnp.sum(jnp.where(hit, scores, 0.0), axis=0, keepdims=True) for _, hit in picked]
    wsum = functools.reduce(lambda a, b: a + b, wks)
    eidx_ref[...] = jnp.zeros_like(eidx_ref)
    rank_ref[...] = jnp.zeros_like(rank_ref)
    wk_ref[...] = jnp.zeros_like(wk_ref)
    for j, ((ei, hit), wk) in enumerate(zip(picked, wks)):
        eidx_ref[j:j + 1, :] = ei
        rank_ref[j:j + 1, :] = jnp.sum(jnp.where(hit, ranks, 0.0), axis=0, keepdims=True).astype(jnp.int32)
        wk_ref[j:j + 1, :] = wk / wsum * ROUTED_SCALE


def moe_router(x, g, sc, sh, router_w, router_bias, tm=512):
    t, d = x.shape
    ne = N_EXPERTS
    vec = pl.BlockSpec((1, d), lambda i: (0, 0))
    lane = pl.BlockSpec((8, tm), lambda i: (0, i))
    return pl.pallas_call(
        _router_kernel,
        out_shape=(jax.ShapeDtypeStruct((t, d), F32), jax.ShapeDtypeStruct((8, t), jnp.int32),
                   jax.ShapeDtypeStruct((8, t), jnp.int32), jax.ShapeDtypeStruct((8, t), F32),
                   jax.ShapeDtypeStruct((ne, 128), F32)),
        grid=(t // tm,),
        in_specs=[pl.BlockSpec((tm, d), lambda i: (i, 0)), vec, vec, vec,
                  pl.BlockSpec((ne, d), lambda i: (0, 0)), pl.BlockSpec((ne, 1), lambda i: (0, 0))],
        out_specs=(pl.BlockSpec((tm, d), lambda i: (i, 0)), lane, lane, lane,
                   pl.BlockSpec((ne, 128), lambda i: (0, 0))),
        scratch_shapes=[pltpu.VMEM((ne, 128), F32)],
        compiler_params=_cparams(("arbitrary",)),
        name="moe_router",
    )(x, g, sc, sh, router_w.T, router_bias.reshape(ne, 1))


def _plan_kernel(eidx_ref, rank_ref, ps_ref, dest_ref):
    ne = N_EXPERTS
    tm = eidx_ref.shape[1]
    io = lax.broadcasted_iota(jnp.int32, (ne, tm), 0)
    ps = ps_ref[...]
    dest_ref[...] = jnp.zeros_like(dest_ref)
    for j in range(TOP_K):
        hit = io == eidx_ref[j:j + 1, :]
        base = jnp.sum(jnp.where(hit, ps, 0.0), axis=0, keepdims=True)
        dest_ref[j:j + 1, :] = base.astype(jnp.int32) + rank_ref[j:j + 1, :]


def moe_plan(eidx, rank, pad_start, tm=2048):
    t = eidx.shape[1]
    tm = min(tm, t)
    lane = pl.BlockSpec((8, tm), lambda i: (0, i))
    return pl.pallas_call(
        _plan_kernel,
        out_shape=jax.ShapeDtypeStruct((8, t), jnp.int32),
        grid=(t // tm,),
        in_specs=[lane, lane, pl.BlockSpec((N_EXPERTS, 1), lambda i: (0, 0))],
        out_specs=lane,
        compiler_params=_cparams(("parallel",)),
        name="moe_plan",
    )(eidx, rank, pad_start.astype(F32).reshape(N_EXPERTS, 1))


SUBLANES = 8
_PAD_PIECES = tuple(1 << k for k in reversed(range(3, MOE_BLOCK.bit_length() - 1)))


def _dispatch_kernel(dest_ref, zs_ref, zn_ref, hf_ref, xb_ref, zbuf, sem, zsem, *, t_total):
    tm = hf_ref.shape[0]
    base = pl.program_id(0) * tm

    def row_copy(t, j):
        d = dest_ref[j * t_total + base + t]
        return pltpu.make_async_copy(hf_ref.at[pl.ds(t, 1)], xb_ref.at[pl.ds(d, 1)], sem)

    def start(t, carry):
        for j in range(TOP_K):
            row_copy(t, j).start()
        return carry

    def wait_all():
        for _ in range(TOP_K):
            pltpu.make_async_copy(hf_ref, xb_ref.at[pl.ds(0, tm)], sem).wait()

    def fill(do_start):
        def body(e, carry):
            zs, zn = zs_ref[e], zn_ref[e]
            end = zs + zn

            def piece(src, dst):
                cp = pltpu.make_async_copy(src, dst, zsem)
                cp.start() if do_start else cp.wait()

            for r in range(SUBLANES - 1):
                @pl.when(r < (zn & (SUBLANES - 1)))
                def _():
                    piece(zbuf.at[pl.ds(0, 1)], xb_ref.at[pl.ds(zs + r, 1)])
            for p in _PAD_PIECES:
                @pl.when((zn & p) != 0)
                def _():
                    q = pl.multiple_of(end - (zn & ~(p - 1)), SUBLANES)
                    piece(zbuf.at[pl.ds(0, p)], xb_ref.at[pl.ds(q, p)])
            return carry
        lax.fori_loop(0, N_EXPERTS, body, 0)

    first = pl.program_id(0) == 0

    @pl.when(first)
    def _():
        zbuf[...] = jnp.zeros_like(zbuf)
        fill(True)

    lax.fori_loop(0, tm, start, 0)

    @pl.when(first)
    def _():
        fill(False)

    wait_all()


def moe_dispatch(dest, zero_start, zero_len, hf, n_slots, tm=256):
    t, d = hf.shape
    return pl.pallas_call(
        functools.partial(_dispatch_kernel, t_total=t),
        out_shape=jax.ShapeDtypeStruct((n_slots, d), hf.dtype),
        grid_spec=pltpu.PrefetchScalarGridSpec(
            num_scalar_prefetch=3,
            grid=(t // tm,),
            in_specs=[pl.BlockSpec((tm, d), lambda i, *_: (i, 0))],
            out_specs=pl.BlockSpec(memory_space=pl.ANY),
            scratch_shapes=[pltpu.VMEM((MOE_BLOCK // 2, d), hf.dtype),
                            pltpu.SemaphoreType.DMA(()), pltpu.SemaphoreType.DMA(())]),
        compiler_params=_cparams(("arbitrary",)),
        name="moe_dispatch",
    )(dest, zero_start, zero_len, hf)


def _ffn(x, wg, wu, wd):
    hg = jnp.dot(x, wg, preferred_element_type=F32)
    hu = jnp.dot(x, wu, preferred_element_type=F32)
    h = (hg * jax.nn.sigmoid(hg) * hu).astype(BF16)
    return jnp.dot(h, wd, preferred_element_type=F32)


def _expert_kernel(be_ref, na_ref, first_ref, next_ref, slot_ref, x_ref, wg_hbm, wu_hbm, wd_hbm, o_ref,
                   wgf, wuf, wdf, wgb, wub, wdb, sems, *, layer):
    b = pl.program_id(0)
    active = b < na_ref[0]

    def fetch(e, s):
        return (pltpu.make_async_copy(wg_hbm.at[layer, e], wgf.at[s], sems.at[s, 0]),
                pltpu.make_async_copy(wu_hbm.at[layer, e], wuf.at[s], sems.at[s, 1]),
                pltpu.make_async_copy(wd_hbm.at[layer, e], wdf.at[s], sems.at[s, 2]))

    @pl.when(jnp.logical_and(active, first_ref[b] == 1))
    def _():
        s = slot_ref[b]

        @pl.when(b == 0)
        def _():
            for cp in fetch(be_ref[b], s):
                cp.start()

        for cp in fetch(be_ref[b], s):
            cp.wait()
        nxt = next_ref[b]

        @pl.when(nxt >= 0)
        def _():
            for cp in fetch(nxt, 1 - s):
                cp.start()

        wgb[...] = wgf[s].astype(BF16)
        wub[...] = wuf[s].astype(BF16)
        wdb[...] = wdf[s].astype(BF16)

    @pl.when(active)
    def _():
        o_ref[...] = _ffn(x_ref[...].astype(BF16), wgb[...], wub[...], wdb[...])


def moe_experts(blk_e, n_active, first, nxt, slot, xb, w_gate, w_up, w_down, layer):
    p, d = xb.shape
    ff = w_gate.shape[3]
    nblk = p // MOE_BLOCK
    row = lambda b, be, na, *_: (jnp.minimum(b, na[0] - 1), 0)
    hbm = pl.BlockSpec(memory_space=pl.ANY)
    return pl.pallas_call(
        functools.partial(_expert_kernel, layer=layer),
        out_shape=jax.ShapeDtypeStruct((p, d), F32),
        grid_spec=pltpu.PrefetchScalarGridSpec(
            num_scalar_prefetch=5,
            grid=(nblk,),
            in_specs=[pl.BlockSpec((MOE_BLOCK, d), row), hbm, hbm, hbm],
            out_specs=pl.BlockSpec((MOE_BLOCK, d), row),
            scratch_shapes=[pltpu.VMEM((2, d, ff), F32), pltpu.VMEM((2, d, ff), F32),
                            pltpu.VMEM((2, ff, d), F32),
                            pltpu.VMEM((d, ff), BF16), pltpu.VMEM((d, ff), BF16),
                            pltpu.VMEM((ff, d), BF16), pltpu.SemaphoreType.DMA((2, 3))]),
        compiler_params=_cparams(("arbitrary",)),
        name="moe_experts",
    )(blk_e, n_active, first, nxt, slot, xb, w_gate, w_up, w_down)


def _combine_kernel(dest_ref, yb_ref, x_ref, hf_ref, wk_ref, gate_ref, wg_ref, wu_ref, wd_ref, o_ref,
                    buf, wgb, wub, wdb, sem, *, t_total):
    tm, d = x_ref.shape
    base = pl.program_id(0) * tm

    @pl.when(pl.program_id(0) == 0)
    def _():
        wgb[...] = wg_ref[...].astype(BF16)
        wub[...] = wu_ref[...].astype(BF16)
        wdb[...] = wd_ref[...].astype(BF16)

    def row_copy(t, j):
        dst = dest_ref[j * t_total + base + t]
        return pltpu.make_async_copy(yb_ref.at[pl.ds(dst, 1)], buf.at[j, pl.ds(t, 1)], sem)

    def start(t, carry):
        for j in range(TOP_K):
            row_copy(t, j).start()
        return carry

    lax.fori_loop(0, tm, start, 0)
    acc = _ffn(hf_ref[...].astype(BF16), wgb[...], wub[...], wdb[...])
    for j in range(TOP_K):
        pltpu.make_async_copy(yb_ref.at[pl.ds(0, tm)], buf.at[j], sem).wait()
    for j in range(TOP_K):
        acc = acc + wk_ref[:, j:j + 1] * buf[j]
    o_ref[...] = x_ref[...] + gate_ref[...] * acc


def moe_combine(dest, yb, x, hf, wk_t, gate, sh_gate, sh_up, sh_down, tm=128):
    t, d = x.shape
    ff = sh_gate.shape[1]
    tm = min(tm, t)
    tile = lambda i, dest: (i, 0)
    full = lambda shape: pl.BlockSpec(shape, lambda i, dest: (0, 0))
    return pl.pallas_call(
        functools.partial(_combine_kernel, t_total=t),
        out_shape=jax.ShapeDtypeStruct((t, d), F32),
        grid_spec=pltpu.PrefetchScalarGridSpec(
            num_scalar_prefetch=1,
            grid=(t // tm,),
            in_specs=[pl.BlockSpec(memory_space=pl.ANY),
                      pl.BlockSpec((tm, d), tile), pl.BlockSpec((tm, d), tile),
                      pl.BlockSpec((tm, 8), tile), full((1, d)),
                      full((d, ff)), full((d, ff)), full((ff, d))],
            out_specs=pl.BlockSpec((tm, d), tile),
            scratch_shapes=[pltpu.VMEM((TOP_K, tm, d), F32),
                            pltpu.VMEM((d, ff), BF16), pltpu.VMEM((d, ff), BF16), pltpu.VMEM((ff, d), BF16),
                            pltpu.SemaphoreType.DMA(())]),
        compiler_params=_cparams(("arbitrary",)),
        name="moe_combine",
    )(dest, yb, x, hf, wk_t, gate, sh_gate, sh_up, sh_down)


def moe_layer(x, g, sc, sh, gate, router_w, router_bias, w_gate, w_up, w_down, sh_gate, sh_up, sh_down,
              layer):
    t, d = x.shape
    ne = N_EXPERTS
    hf, eidx, rank, wk, cnt = moe_router(x, g, sc, sh, router_w, router_bias)
    counts = cnt[:, 0].astype(jnp.int32)
    padded = (counts + MOE_BLOCK - 1) // MOE_BLOCK * MOE_BLOCK
    pad_end = jnp.cumsum(padded)
    pad_start = pad_end - padded
    n_slots = (t * TOP_K + ne * (MOE_BLOCK - 1) + MOE_BLOCK - 1) // MOE_BLOCK * MOE_BLOCK
    nblk = n_slots // MOE_BLOCK
    dest = moe_plan(eidx, rank, pad_start)[:TOP_K].reshape(TOP_K * t)
    blk_start = jnp.arange(nblk, dtype=jnp.int32) * MOE_BLOCK
    n_active = (pad_end[-1] // MOE_BLOCK).astype(jnp.int32).reshape(1)
    blk_e = jnp.minimum(jnp.sum(blk_start[:, None] >= pad_end[None, :], axis=1), ne - 1).astype(jnp.int32)
    last_e = jnp.max(jnp.where(counts > 0, jnp.arange(ne, dtype=jnp.int32), 0))
    live = blk_start < pad_end[-1]
    blk_e = jnp.where(live, blk_e, last_e)
    first = jnp.logical_and(live, jnp.concatenate([jnp.ones((1,), bool), blk_e[1:] != blk_e[:-1]]))
    slot = ((jnp.cumsum(first.astype(jnp.int32)) - 1) % 2).astype(jnp.int32)
    ids = jnp.arange(ne, dtype=jnp.int32)
    later = jnp.logical_and(ids[None, :] > ids[:, None], counts[None, :] > 0)
    next_e = jnp.min(jnp.where(later, ids[None, :], ne), axis=1)
    next_e = jnp.where(next_e < ne, next_e, -1)
    nxt = jnp.sum(jnp.where(blk_e[:, None] == ids[None, :], next_e[None, :], 0), axis=1).astype(jnp.int32)

    xb = moe_dispatch(dest, pad_start + counts, padded - counts, hf, n_slots)
    yb = moe_experts(blk_e, n_active, first.astype(jnp.int32), nxt, slot, xb, w_gate, w_up, w_down, layer)
    return moe_combine(dest, yb, x, hf, wk.T, gate, sh_gate, sh_up, sh_down)


def kernel(x, c, ada_w, ada_b, norm_mix_g, norm_ffn_g, hy_w_in, hy_conv_w, hy_conv_b, hy_f_w0, hy_f_b0, hy_f_w1, hy_f_b1, hy_f_w2, hy_f_b2, hy_f_freq, hy_f_wout, hy_skip, hy_w_out, ml_w_in, ml_conv_w, ml_conv_b, ml_gate_b, ml_head_g, ml_w_out, moe_router_w, moe_router_bias, moe_w_gate, moe_w_up, moe_w_down, sh_w_gate, sh_w_up, sh_w_down, final_g):
    bsz, seq, d = x.shape
    assert bsz == 1, "kernels are written for a single sequence"
    depth = ada_w.shape[0]
    xs = x.reshape(seq, d)
    ada = ada_proj(c, ada_w, ada_b)
    for layer in range(depth):
        sh_m, sc_m, g_m, sh_f, sc_f, g_f = (ada[layer, :, k * d:(k + 1) * d] for k in range(6))
        gm = norm_mix_g[layer].reshape(1, d)
        j = layer // 2
        if layer % 2 == 0:
            hm = normmod_slab(xs, gm, sc_m, sh_m)
            xs = hyena_mixer(hm, xs, g_m, hy_w_in, j, hy_conv_w[j], hy_conv_b[j], hy_f_w0[j], hy_f_b0[j],
                             hy_f_w1[j], hy_f_b1[j], hy_f_w2[j], hy_f_b2[j], hy_f_freq[j], hy_f_wout[j],
                             hy_skip[j], hy_w_out[j])
        else:
            hm = normmod(xs, gm, sc_m, sh_m)
            xs = mlstm_mixer(hm, xs, g_m, ml_w_in, j, ml_conv_w[j], ml_conv_b[j], ml_gate_b[j],
                             ml_head_g[j], ml_w_out[j])
        xs = moe_layer(xs, norm_ffn_g[layer].reshape(1, d), sc_f, sh_f, g_f, moe_router_w[layer],
                       moe_router_bias[layer], moe_w_gate, moe_w_up, moe_w_down,
                       sh_w_gate[layer], sh_w_up[layer], sh_w_down[layer], layer)
    return rmsnorm(xs, final_g.reshape(1, d)).reshape(bsz, seq, d)
```

```python
import functools
import math

import numpy as np
import jax
import jax.numpy as jnp
from jax import lax
from jax.experimental import pallas as pl
from jax.experimental.pallas import tpu as pltpu

F32 = jnp.float32
BF16 = jnp.bfloat16
EPS = 1e-6

V7X_VMEM_BYTES = 64 * 1024 * 1024
VMEM_LIMIT = V7X_VMEM_BYTES - 8 * 1024 * 1024

SHORT_CONV = 3
HY_BANDS = 16
HY_DECAY_TARGET = 1e-2
HY_FAST_DECAY = 0.3
HY_SLOW_DECAY = 1.5
ML_HEADS = 8
ML_CHUNK = 256
N_EXPERTS = 64
TOP_K = 6
N_GROUPS = 8
TOPK_GROUPS = 4
ROUTED_SCALE = 2.5
MOE_BLOCK = 256
DFT_NB = 128


def _cparams(sem):
    return pltpu.CompilerParams(dimension_semantics=sem, vmem_limit_bytes=VMEM_LIMIT)


def _ada_kernel(c_ref, w_ref, b_ref, o_ref, *, tc):
    rows, n = w_ref.shape[1], w_ref.shape[2]

    @pl.when(pl.program_id(1) == 0)
    def _():
        o_ref[...] = b_ref[...]

    c = c_ref[...]
    cs = c * jax.nn.sigmoid(c)
    for j in range(n // tc):
        part = (w_ref[0, :, j * tc:(j + 1) * tc] * cs).reshape(rows // 8, 8, tc).sum(axis=0)
        o_ref[0, :, j * tc:(j + 1) * tc] += jnp.sum(part, axis=0, keepdims=True)


def ada_proj(c, ada_w, ada_b, rows=128):
    depth, d, n = ada_w.shape
    return pl.pallas_call(
        functools.partial(_ada_kernel, tc=2048),
        out_shape=jax.ShapeDtypeStruct((depth, 1, n), F32),
        grid=(depth, d // rows),
        in_specs=[pl.BlockSpec((rows, 1), lambda l, r: (r, 0)),
                  pl.BlockSpec((1, rows, n), lambda l, r: (l, r, 0)),
                  pl.BlockSpec((1, 1, n), lambda l, r: (l, 0, 0))],
        out_specs=pl.BlockSpec((1, 1, n), lambda l, r: (l, 0, 0)),
        compiler_params=_cparams(("arbitrary", "arbitrary")),
        name="ada_proj",
    )(c.reshape(d, 1), ada_w, ada_b.reshape(depth, 1, n))


def _normmod(x, g, sc, sh):
    r = lax.rsqrt(jnp.mean(x * x, axis=-1, keepdims=True) + EPS)
    return (x * r) * g * (1.0 + sc) + sh


def _normmod_kernel(x_ref, g_ref, sc_ref, sh_ref, o_ref):
    o_ref[...] = _normmod(x_ref[...], g_ref[...], sc_ref[...], sh_ref[...]).astype(o_ref.dtype)


def normmod(x, g, sc, sh, out_dtype=BF16, tm=512):
    t, d = x.shape
    vec = pl.BlockSpec((1, d), lambda i: (0, 0))
    return pl.pallas_call(
        _normmod_kernel,
        out_shape=jax.ShapeDtypeStruct((t, d), out_dtype),
        grid=(t // tm,),
        in_specs=[pl.BlockSpec((tm, d), lambda i: (i, 0)), vec, vec, vec],
        out_specs=pl.BlockSpec((tm, d), lambda i: (i, 0)),
        compiler_params=_cparams(("parallel",)),
        name="normmod",
    )(x, g, sc, sh)


def _rmsnorm_kernel(x_ref, g_ref, o_ref):
    x = x_ref[...]
    r = lax.rsqrt(jnp.mean(x * x, axis=-1, keepdims=True) + EPS)
    o_ref[...] = (x * r) * g_ref[...]


def rmsnorm(x, g, tm=512):
    t, d = x.shape
    return pl.pallas_call(
        _rmsnorm_kernel,
        out_shape=jax.ShapeDtypeStruct((t, d), F32),
        grid=(t // tm,),
        in_specs=[pl.BlockSpec((tm, d), lambda i: (i, 0)), pl.BlockSpec((1, d), lambda i: (0, 0))],
        out_specs=pl.BlockSpec((tm, d), lambda i: (i, 0)),
        compiler_params=_cparams(("parallel",)),
        name="final_rmsnorm",
    )(x, g)


def _mm_kernel(a_ref, w_ref, o_ref, wb_ref):
    @pl.when(pl.program_id(1) == 0)
    def _():
        wb_ref[...] = w_ref[0].astype(BF16)

    o_ref[...] = jnp.dot(a_ref[...], wb_ref[...], preferred_element_type=F32).astype(o_ref.dtype)


def matmul_cols(a, w, layer, n_out, out_dtype=BF16, tm=1024, tn=512):
    m, k = a.shape
    tm = min(tm, m)
    return pl.pallas_call(
        _mm_kernel,
        out_shape=jax.ShapeDtypeStruct((m, n_out), out_dtype),
        grid=(n_out // tn, m // tm),
        in_specs=[pl.BlockSpec((tm, k), lambda j, i: (i, 0)),
                  pl.BlockSpec((1, k, tn), lambda j, i: (layer, 0, j))],
        out_specs=pl.BlockSpec((tm, tn), lambda j, i: (i, j)),
        scratch_shapes=[pltpu.VMEM((k, tn), BF16)],
        compiler_params=_cparams(("arbitrary", "arbitrary")),
        name="matmul_cols",
    )(a, w)


def _outproj_kernel(a_ref, x_ref, gate_ref, w_ref, o_ref, wb_ref):
    @pl.when(pl.program_id(1) == 0)
    def _():
        wb_ref[...] = w_ref[...].astype(BF16)

    y = jnp.dot(a_ref[...], wb_ref[...], preferred_element_type=F32)
    o_ref[...] = x_ref[...] + gate_ref[...] * y


def outproj_residual(a, x, gate, w, tm=1024, tn=512):
    t, k = a.shape
    d = w.shape[1]
    tm = min(tm, t)
    return pl.pallas_call(
        _outproj_kernel,
        out_shape=jax.ShapeDtypeStruct((t, d), F32),
        grid=(d // tn, t // tm),
        in_specs=[pl.BlockSpec((tm, k), lambda j, i: (i, 0)),
                  pl.BlockSpec((tm, tn), lambda j, i: (i, j)),
                  pl.BlockSpec((1, tn), lambda j, i: (0, j)),
                  pl.BlockSpec((k, tn), lambda j, i: (0, j))],
        out_specs=pl.BlockSpec((tm, tn), lambda j, i: (i, j)),
        scratch_shapes=[pltpu.VMEM((k, tn), BF16)],
        compiler_params=_cparams(("arbitrary", "arbitrary")),
        name="outproj_residual",
    )(a, x, gate, w)


HALO = 16


def _conv3(main, prev, nxt, w, b, i, nt):
    tm = main.shape[0]
    rows = lax.broadcasted_iota(jnp.int32, main.shape, 0)
    pr = jnp.where(i > 0, prev[HALO - 1:HALO, :], 0.0)
    nx = jnp.where(i < nt - 1, nxt[0:1, :], 0.0)
    up = jnp.where(rows == 0, pr, pltpu.roll(main, 1, 0))
    dn = jnp.where(rows == tm - 1, nx, pltpu.roll(main, tm - 1, 0))
    return up * w[0:1, :] + main * w[1:2, :] + dn * w[2:3, :] + b


def _halo_specs(tm, tc, t, col_fn):
    nh = t // HALO
    r = tm // HALO
    return [
        pl.BlockSpec((tm, tc), lambda i, j: (i, col_fn(j))),
        pl.BlockSpec((HALO, tc), lambda i, j: (jnp.maximum(i * r - 1, 0), col_fn(j))),
        pl.BlockSpec((HALO, tc), lambda i, j: (jnp.minimum((i + 1) * r, nh - 1), col_fn(j))),
    ]


HY_GROUP = 8


def _normmod_slab_kernel(x_ref, g_ref, sc_ref, sh_ref, perm_ref, o_ref):
    s1, grp, d = x_ref.shape
    x = x_ref[...].reshape(s1 * grp, d)
    y = _normmod(x, g_ref[...], sc_ref[...], sh_ref[...]).astype(BF16)
    o_ref[...] = jnp.dot(perm_ref[...], y, preferred_element_type=F32).astype(o_ref.dtype)


def normmod_slab(x, g, sc, sh):
    t, d = x.shape
    s1 = t // DFT_NB
    grp = HY_GROUP
    n = s1 * grp
    perm = np.zeros((n, n), np.float32)
    src = np.arange(n)
    perm[(src % grp) * s1 + src // grp, src] = 1.0
    vec = pl.BlockSpec((1, d), lambda i: (0, 0))
    return pl.pallas_call(
        _normmod_slab_kernel,
        out_shape=jax.ShapeDtypeStruct((t, d), BF16),
        grid=(DFT_NB // grp,),
        in_specs=[pl.BlockSpec((s1, grp, d), lambda i: (0, i, 0)), vec, vec, vec,
                  pl.BlockSpec((n, n), lambda i: (0, 0))],
        out_specs=pl.BlockSpec((n, d), lambda i: (i, 0)),
        compiler_params=_cparams(("parallel",)),
        name="normmod_slab",
    )(x.reshape(s1, DFT_NB, d), g, sc, sh, jnp.asarray(perm).astype(BF16))


def _shift_rows(x, down):
    n = x.shape[0]
    rows = lax.broadcasted_iota(jnp.int32, x.shape, 0)
    if down:
        return jnp.where(rows == 0, 0.0, pltpu.roll(x, 1, 0))
    return jnp.where(rows == n - 1, 0.0, pltpu.roll(x, n - 1, 0))


def _hy_gate_kernel(*refs):
    mains, prevs, nexts = refs[0:3], refs[3:6], refs[6:9]
    ws, bs = refs[9:12], refs[12:15]
    vg_ref, x0_ref = refs[15:17]
    i = pl.program_id(0)
    nt = pl.num_programs(0)
    s1 = prevs[0].shape[0]
    d = mains[0].shape[1]
    grp = mains[0].shape[0] // s1

    def halo(ref, edge, down):
        h = ref[...].astype(F32)
        return jnp.where(edge, _shift_rows(h, down), h)

    ups = [halo(prevs[g], i == 0, True) for g in range(3)]
    dns = [halo(nexts[g], i == nt - 1, False) for g in range(3)]
    for j in range(grp):
        out = []
        for g in range(3):
            slab = lambda k: mains[g][k * s1:(k + 1) * s1, :].astype(F32)
            up = ups[g] if j == 0 else slab(j - 1)
            dn = dns[g] if j == grp - 1 else slab(j + 1)
            w = ws[g][...]
            out.append(up * w[0:1, :] + slab(j) * w[1:2, :] + dn * w[2:3, :] + bs[g][...])
        x0, x1, v = out
        vg_ref[:, j * d:(j + 1) * d] = (v * x1).astype(vg_ref.dtype)
        x0_ref[:, j * d:(j + 1) * d] = x0.astype(x0_ref.dtype)


def hy_gate(p, conv_w, conv_b):
    t, c3 = p.shape
    d = c3 // 3
    s1 = t // DFT_NB
    grp = HY_GROUP
    main = lambda g: pl.BlockSpec((grp * s1, d), lambda i: (i, g))
    prev = lambda g: pl.BlockSpec((s1, d), lambda i: ((i * grp + DFT_NB - 1) % DFT_NB, g))
    nxt = lambda g: pl.BlockSpec((s1, d), lambda i: (((i + 1) * grp) % DFT_NB, g))
    specs = [f(g) for f in (main, prev, nxt) for g in range(3)]
    specs += [pl.BlockSpec((SHORT_CONV, d), lambda i, g=g: (0, g)) for g in range(3)]
    specs += [pl.BlockSpec((1, d), lambda i, g=g: (0, g)) for g in range(3)]
    out_spec = pl.BlockSpec((s1, grp * d), lambda i: (0, i))
    cb = conv_b.reshape(1, c3)
    out = jax.ShapeDtypeStruct((s1, DFT_NB * d), BF16)
    return pl.pallas_call(
        _hy_gate_kernel,
        out_shape=(out, out),
        grid=(DFT_NB // grp,),
        in_specs=specs,
        out_specs=(out_spec, out_spec),
        compiler_params=_cparams(("parallel",)),
        name="hy_gate",
    )(*([p] * 9), conv_w, conv_w, conv_w, cb, cb, cb)


def _hy_features(seq):
    t = np.linspace(0.0, 1.0, seq, dtype=np.float64)[:, None]
    ang = 2.0 * math.pi * np.arange(seq, dtype=np.float64)[:, None] / seq
    bands = np.linspace(1e-4, HY_BANDS - 1, HY_BANDS, dtype=np.float64)
    z = np.concatenate([t, np.cos(ang * bands), -np.sin(ang * bands)], axis=-1)
    zp = np.zeros((seq, 128), np.float32)
    zp[:, :z.shape[1]] = z
    s1 = seq // DFT_NB
    return zp.reshape(s1, DFT_NB, 128).transpose(1, 0, 2).reshape(seq, 128)


def _hy_deltas(d):
    return np.abs(np.linspace(math.log(HY_DECAY_TARGET) / HY_FAST_DECAY,
                              math.log(HY_DECAY_TARGET) / HY_SLOW_DECAY, d,
                              dtype=np.float64)).astype(np.float32)[None, :]


def _hy_filter_kernel(z_ref, w0_ref, b0_ref, w1_ref, b1_ref, w2_ref, b2_ref, fr_ref, wo_ref,
                      dl_ref, o_ref, *, seq, grp):
    hi = lax.Precision.HIGHEST
    fr = fr_ref[...]
    a = jnp.sin(fr * (jnp.dot(z_ref[...], w0_ref[...], precision=hi, preferred_element_type=F32)
                      + b0_ref[...]))
    a = jnp.sin(fr * (jnp.dot(a, w1_ref[...], precision=hi, preferred_element_type=F32) + b1_ref[...]))
    a = jnp.sin(fr * (jnp.dot(a, w2_ref[...], precision=hi, preferred_element_type=F32) + b2_ref[...]))
    k = jnp.dot(a.astype(BF16), wo_ref[...].astype(BF16), preferred_element_type=F32)
    d2 = k.shape[1]
    d = d2 // 2
    s1 = k.shape[0] // grp
    for j in range(grp):
        s2 = pl.program_id(0) * grp + j
        tt = lax.broadcasted_iota(jnp.int32, (s1, 1), 0) * DFT_NB + s2
        window = jnp.exp(-(tt.astype(F32) * (1.0 / (seq - 1))) * dl_ref[...])
        kj = k[j * s1:(j + 1) * s1]
        o_ref[:, j * d2:j * d2 + d] = (kj[:, :d] * window).astype(o_ref.dtype)
        o_ref[:, j * d2 + d:(j + 1) * d2] = jnp.where(tt == 0, 0.0, kj[:, d:] * window).astype(o_ref.dtype)


def hy_filter(seq, w0, b0, w1, b1, w2, b2, freq, wout, grp=4):
    fw = w1.shape[0]
    d2 = wout.shape[1]
    s1 = seq // DFT_NB
    z = jnp.asarray(_hy_features(seq))
    w0p = jnp.zeros((128, fw), F32).at[:w0.shape[0]].set(w0)
    full = lambda shape: pl.BlockSpec(shape, lambda i: (0,) * len(shape))
    r = lambda v: v.reshape(1, -1)
    return pl.pallas_call(
        functools.partial(_hy_filter_kernel, seq=seq, grp=grp),
        out_shape=jax.ShapeDtypeStruct((s1, DFT_NB * d2), BF16),
        grid=(DFT_NB // grp,),
        in_specs=[pl.BlockSpec((grp * s1, 128), lambda i: (i, 0)),
                  full((128, fw)), full((1, fw)), full((fw, fw)), full((1, fw)),
                  full((fw, fw)), full((1, fw)), full((1, fw)), full((fw, d2)), full((1, d2 // 2))],
        out_specs=pl.BlockSpec((s1, grp * d2), lambda i: (0, i)),
        compiler_params=_cparams(("parallel",)),
        name="hy_filter",
    )(z, w0p, r(b0), w1, r(b1), w2, r(b2), r(freq), wout, jnp.asarray(_hy_deltas(d2 // 2)))


DFT_ROW_ALIGN = 16


def _dft_slabs(seq):
    nf = (2 * seq // DFT_NB) // 2 + 1
    return nf, -(-nf // DFT_ROW_ALIGN) * DFT_ROW_ALIGN


@functools.lru_cache(maxsize=None)
def _dft_tables(seq):
    n = 2 * seq
    nb = DFT_NB
    na = n // nb
    nf, nfp = _dft_slabs(seq)
    s1 = np.arange(na // 2)
    f1 = np.arange(nf)
    ph = 2.0 * math.pi * np.outer(f1, s1) / na
    fs = np.zeros((2 * nfp, na // 2))
    fs[:nf] = np.cos(ph)
    fs[nfp:nfp + nf] = -np.sin(ph)
    f2 = np.arange(nb)
    s2 = np.arange(nb)
    freq = f1[:, None, None] + na * f2[None, :, None]
    th = 2.0 * math.pi * ((freq * s2[None, None, :]) % n) / n
    gr, gi = np.cos(th), -np.sin(th)
    g = np.concatenate([np.concatenate([gr, -gi], axis=2),
                        np.concatenate([gi, gr], axis=2)], axis=1)
    ginv = np.transpose(g, (0, 2, 1))
    t1 = np.arange(na // 2)
    ph2 = 2.0 * math.pi * np.outer(t1, f1) / na
    wgt = np.full((nf,), 2.0)
    wgt[0] = wgt[nf - 1] = 1.0
    finv = np.zeros((na // 2, 2 * nfp))
    finv[:, :nf] = np.cos(ph2) * wgt
    finv[:, nfp:nfp + nf] = -np.sin(ph2) * wgt
    return tuple(a.astype(np.float32) for a in (fs, g, ginv, finv))


def _dft1_kernel(f_ref, x_ref, o_ref):
    y = jnp.dot(f_ref[...].astype(BF16), x_ref[...], preferred_element_type=F32)
    o_ref[...] = y.reshape(o_ref.shape).astype(o_ref.dtype)


def dft_stage1(x2d, fs, tn=8192):
    k, cols = x2d.shape
    na = fs.shape[0] // 2
    return pl.pallas_call(
        _dft1_kernel,
        out_shape=jax.ShapeDtypeStruct((2, na, cols), BF16),
        grid=(cols // tn,),
        in_specs=[pl.BlockSpec((2 * na, k), lambda c: (0, 0)),
                  pl.BlockSpec((k, tn), lambda c: (0, c))],
        out_specs=pl.BlockSpec((2, na, tn), lambda c: (0, 0, c)),
        compiler_params=_cparams(("parallel",)),
        name="dft_stage1",
    )(fs, x2d)


def _hspec_kernel(g_ref, af_ref, ab_ref, o_ref, *, scale):
    nb2, dt = o_ref.shape[1], o_ref.shape[2]
    g = g_ref[0].astype(BF16)
    hf = jnp.dot(g, af_ref[...].reshape(nb2, dt), preferred_element_type=F32)
    hb = jnp.dot(g, ab_ref[...].reshape(nb2, dt), preferred_element_type=F32)
    nb = nb2 // 2
    o_ref[0, :nb, :] = ((hf[:nb] + hb[:nb]) * scale).astype(o_ref.dtype)
    o_ref[0, nb:, :] = ((hf[nb:] - hb[nb:]) * scale).astype(o_ref.dtype)


def filter_spectrum(ak, g, d, n, dt=2048):
    nb = ak.shape[2]
    nf = g.shape[0]
    nd = d // dt
    scale = 1.0 / n
    return pl.pallas_call(
        functools.partial(_hspec_kernel, scale=scale),
        out_shape=jax.ShapeDtypeStruct((nf, 2 * nb, d), BF16),
        grid=(nf, nd),
        in_specs=[pl.BlockSpec((1, 2 * nb, 2 * nb), lambda f, j: (f, 0, 0)),
                  pl.BlockSpec((2, 1, nb, dt), lambda f, j: (0, f, 0, j)),
                  pl.BlockSpec((2, 1, nb, dt), lambda f, j: (0, f, 0, nd + j))],
        out_specs=pl.BlockSpec((1, 2 * nb, dt), lambda f, j: (f, 0, j)),
        compiler_params=_cparams(("parallel", "parallel")),
        name="filter_spectrum",
    )(g, ak, ak)


def _xspec_kernel(g_ref, gi_ref, a_ref, h_ref, o_ref, *, nf):
    nb2, dt = h_ref.shape[1], h_ref.shape[2]
    nb = nb2 // 2
    live = pl.program_id(0) < nf

    @pl.when(live)
    def _():
        x = jnp.dot(g_ref[0].astype(BF16), a_ref[...].reshape(nb2, dt), preferred_element_type=F32)
        h = h_ref[0].astype(F32)
        xr, xi, hr, hi = x[:nb], x[nb:], h[:nb], h[nb:]
        y = jnp.concatenate([xr * hr - xi * hi, xr * hi + xi * hr], axis=0).astype(BF16)
        b = jnp.dot(gi_ref[0].astype(BF16), y, preferred_element_type=F32)
        o_ref[...] = b.reshape(o_ref.shape).astype(o_ref.dtype)

    @pl.when(jnp.logical_not(live))
    def _():
        o_ref[...] = jnp.zeros_like(o_ref)


def spectrum_product(a, h, g, ginv, dt=2048):
    _, nfp, nb, d = a.shape
    nf = g.shape[0]
    slab = lambda f: jnp.minimum(f, nf - 1)
    gspec = pl.BlockSpec((1, 2 * nb, 2 * nb), lambda f, j: (slab(f), 0, 0))
    aspec = pl.BlockSpec((2, 1, nb, dt), lambda f, j: (0, f, 0, j))
    return pl.pallas_call(
        functools.partial(_xspec_kernel, nf=nf),
        out_shape=jax.ShapeDtypeStruct((2, nfp, nb, d), BF16),
        grid=(nfp, d // dt),
        in_specs=[gspec, gspec, aspec, pl.BlockSpec((1, 2 * nb, dt), lambda f, j: (slab(f), 0, j))],
        out_specs=aspec,
        compiler_params=_cparams(("parallel", "parallel")),
        name="spectrum_product",
    )(g, ginv, a, h)


def _idft2_kernel(f_ref, b_ref, vg_ref, x0_ref, skip_ref, o_ref):
    conv = jnp.dot(f_ref[...].astype(BF16), b_ref[...], preferred_element_type=F32)
    vg = vg_ref[...].astype(F32)
    o_ref[...] = ((conv + vg * skip_ref[...]) * x0_ref[...].astype(F32)).astype(o_ref.dtype)


def idft_stage2_mix(b2d, finv, vg2d, x02d, skip_t, tn=4096):
    k, cols = b2d.shape
    rows = finv.shape[0]
    return pl.pallas_call(
        _idft2_kernel,
        out_shape=jax.ShapeDtypeStruct((rows, cols), BF16),
        grid=(cols // tn,),
        in_specs=[pl.BlockSpec((rows, k), lambda c: (0, 0)),
                  pl.BlockSpec((k, tn), lambda c: (0, c)),
                  pl.BlockSpec((rows, tn), lambda c: (0, c)),
                  pl.BlockSpec((rows, tn), lambda c: (0, c)),
                  pl.BlockSpec((1, tn), lambda c: (0, 0))],
        out_specs=pl.BlockSpec((rows, tn), lambda c: (0, c)),
        compiler_params=_cparams(("parallel",)),
        name="idft_stage2_mix",
    )(finv, b2d, vg2d, x02d, skip_t)


def _outproj_slab_kernel(a_ref, x_ref, gate_ref, w_ref, o_ref, wb_ref):
    @pl.when(pl.program_id(1) == 0)
    def _():
        wb_ref[...] = w_ref[...].astype(BF16)

    s1, grp = x_ref.shape[0], x_ref.shape[1]
    k = w_ref.shape[0]
    a = jnp.concatenate([a_ref[:, j * k:(j + 1) * k] for j in range(grp)], axis=0)
    y = jnp.dot(a, wb_ref[...], preferred_element_type=F32)
    for j in range(grp):
        o_ref[:, j, :] = x_ref[:, j, :] + gate_ref[...] * y[j * s1:(j + 1) * s1]


def outproj_residual_slab(a2d, x, gate, w, tn=512):
    t, d = x.shape
    k = w.shape[0]
    s1 = t // DFT_NB
    grp = HY_GROUP
    xspec = pl.BlockSpec((s1, grp, tn), lambda j, i: (0, i, j))
    out = pl.pallas_call(
        _outproj_slab_kernel,
        out_shape=jax.ShapeDtypeStruct((s1, DFT_NB, d), F32),
        grid=(d // tn, DFT_NB // grp),
        in_specs=[pl.BlockSpec((s1, grp * k), lambda j, i: (0, i)), xspec,
                  pl.BlockSpec((1, tn), lambda j, i: (0, j)),
                  pl.BlockSpec((k, tn), lambda j, i: (0, j))],
        out_specs=xspec,
        scratch_shapes=[pltpu.VMEM((k, tn), BF16)],
        compiler_params=_cparams(("arbitrary", "arbitrary")),
        name="outproj_residual_slab",
    )(a2d, x.reshape(s1, DFT_NB, d), gate, w)
    return out.reshape(t, d)


def hyena_mixer(hm, x, gate, w_in, layer, conv_w, conv_b, f_w0, f_b0, f_w1, f_b1, f_w2, f_b2, f_freq,
                f_wout, skip, w_out):
    seq, d = hm.shape
    nb = DFT_NB
    fs, g, ginv, finv = _dft_tables(seq)
    _, nfp = _dft_slabs(seq)
    p = matmul_cols(hm, w_in, layer, 3 * d)
    vg, x0 = hy_gate(p, conv_w, conv_b)
    kf = hy_filter(seq, f_w0, f_b0, f_w1, f_b1, f_w2, f_b2, f_freq, f_wout)
    ak = dft_stage1(kf, fs)
    h = filter_spectrum(ak.reshape(2, nfp, nb, 2 * d), g, d, 2 * seq)
    a = dft_stage1(vg, fs)
    b = spectrum_product(a.reshape(2, nfp, nb, d), h, g, ginv)
    tn = 2 * d
    skip_t = jnp.tile(skip.reshape(1, d), (1, tn // d))
    y = idft_stage2_mix(b.reshape(2 * nfp, nb * d), finv, vg, x0, skip_t, tn=tn)
    return outproj_residual_slab(y, x, gate, w_out)


def _qk_prep_kernel(m_ref, p_ref, n_ref, w_ref, b_ref, s_ref, o_ref):
    i = pl.program_id(0)
    nt = pl.num_programs(0)
    f = lambda r: r[...].astype(F32)
    u = _conv3(f(m_ref), f(p_ref), f(n_ref), w_ref[...], b_ref[...], i, nt)
    o_ref[...] = (u * jax.nn.sigmoid(u) * s_ref[...]).astype(o_ref.dtype)


def qk_prep(p, conv_w, conv_b, qk_dim, dqk, tm=512, tc=512):
    t = p.shape[0]
    c = 2 * qk_dim
    scale = np.ones((1, c), np.float32)
    scale[:, qk_dim:] = 1.0 / math.sqrt(dqk)
    return pl.pallas_call(
        _qk_prep_kernel,
        out_shape=jax.ShapeDtypeStruct((t, c), BF16),
        grid=(t // tm, c // tc),
        in_specs=_halo_specs(tm, tc, t, lambda j: j) + [
            pl.BlockSpec((SHORT_CONV, tc), lambda i, j: (0, j)),
            pl.BlockSpec((1, tc), lambda i, j: (0, j)),
            pl.BlockSpec((1, tc), lambda i, j: (0, j))],
        out_specs=pl.BlockSpec((tm, tc), lambda i, j: (i, j)),
        compiler_params=_cparams(("parallel", "parallel")),
        name="qk_prep",
    )(p, p, p, conv_w, conv_b.reshape(1, c), jnp.asarray(scale))


def _log_sigmoid(x):
    return jnp.minimum(x, 0.0) - jnp.log1p(jnp.exp(-jnp.abs(x)))


def _gates_kernel(hm_ref, w_ref, b_ref, col_ref, row_ref, scal_ref):
    hi = lax.Precision.HIGHEST
    nh = ML_HEADS
    hm = hm_ref[...]
    cs = hm.shape[0]
    gt = jnp.dot(hm, w_ref[...].astype(BF16), preferred_element_type=F32) + b_ref[...]
    gtt = gt.T
    r = lax.broadcasted_iota(jnp.int32, (cs, cs), 0)
    c = lax.broadcasted_iota(jnp.int32, (cs, cs), 1)
    lower = (r >= c).astype(F32)
    upper = (r <= c).astype(F32)
    i_f, f_f, i_b, f_b = (gt[:, k * nh:(k + 1) * nh] for k in range(4))
    lf_f, lf_b = _log_sigmoid(f_f), _log_sigmoid(f_b)
    b_f = jnp.dot(lower, lf_f, precision=hi, preferred_element_type=F32)
    b_b = jnp.dot(upper, lf_b, precision=hi, preferred_element_type=F32)
    g_f = jnp.sum(lf_f, axis=0, keepdims=True)
    g_b = jnp.sum(lf_b, axis=0, keepdims=True)
    a_f = g_f - b_f + i_f
    a_b = g_b - b_b + i_b
    col_ref[...] = jnp.concatenate([b_f, b_b, a_f, a_b], axis=1)
    scal_ref[0] = jnp.concatenate([g_f, g_b, jnp.max(a_f, axis=0, keepdims=True),
                                   jnp.max(a_b, axis=0, keepdims=True)], axis=1)
    i_ft, f_ft, i_bt, f_bt = (gtt[k * nh:(k + 1) * nh, :] for k in range(4))
    b_ft = jnp.dot(_log_sigmoid(f_ft), upper, precision=hi, preferred_element_type=F32)
    b_bt = jnp.dot(_log_sigmoid(f_bt), lower, precision=hi, preferred_element_type=F32)
    row_ref[...] = jnp.concatenate([b_ft, b_bt, i_ft, i_bt], axis=0)


def ml_gates(hm, w_g, gate_b, cs):
    t, d = hm.shape
    g4 = w_g.shape[1]
    nc = t // cs
    lanes = 128
    w_pad = jnp.pad(w_g, ((0, 0), (0, lanes - g4)))
    b_pad = jnp.pad(gate_b.reshape(1, g4), ((0, 0), (0, lanes - g4)))
    return pl.pallas_call(
        _gates_kernel,
        out_shape=(jax.ShapeDtypeStruct((t, g4), F32), jax.ShapeDtypeStruct((g4, t), F32),
                   jax.ShapeDtypeStruct((nc, 1, g4), F32)),
        grid=(nc,),
        in_specs=[pl.BlockSpec((cs, d), lambda c: (c, 0)),
                  pl.BlockSpec((d, lanes), lambda c: (0, 0)),
                  pl.BlockSpec((1, lanes), lambda c: (0, 0))],
        out_specs=(pl.BlockSpec((cs, g4), lambda c: (c, 0)),
                   pl.BlockSpec((g4, cs), lambda c: (0, c)),
                   pl.BlockSpec((1, 1, g4), lambda c: (c, 0, 0))),
        compiler_params=_cparams(("parallel",)),
        name="ml_gates",
    )(hm, w_pad, b_pad)


def _mlstm_head(q, k, v, b_col, a_col, b_row, li_row, g, m_loc, c_ref, n_ref, m_ref, idx, causal):
    cs = q.shape[0]
    c_st = c_ref[idx]
    n_st = n_ref[idx]
    m_st = m_ref[idx][:, 0:1]
    r = lax.broadcasted_iota(jnp.int32, (cs, cs), 0)
    s = lax.broadcasted_iota(jnp.int32, (cs, cs), 1)
    mask = (s <= r) if causal else (s >= r)
    dlog = jnp.where(mask, b_col - b_row + li_row, -jnp.inf)
    m_inter = b_col + m_st
    m_t = jnp.maximum(m_inter, jnp.max(dlog, axis=-1, keepdims=True))
    qk = lax.dot_general(q, k, (((1,), (1,)), ((), ())), preferred_element_type=F32)
    p = jnp.exp(dlog - m_t) * qk
    s_inter = jnp.exp(m_inter - m_t)
    qf = q.astype(F32)
    num = (s_inter * jnp.dot(q, c_st.astype(BF16), preferred_element_type=F32)
           + jnp.dot(p.astype(BF16), v, preferred_element_type=F32))
    den = s_inter * jnp.sum(qf * n_st, axis=-1, keepdims=True) + jnp.sum(p, axis=-1, keepdims=True)
    hout = num / jnp.maximum(jnp.abs(den), jnp.exp(-m_t))
    wgt = jnp.exp(a_col - m_loc)
    kw = k.astype(F32) * wgt
    c_loc = lax.dot_general(kw.astype(BF16), v, (((0,), (0,)), ((), ())), preferred_element_type=F32)
    n_loc = jnp.sum(kw, axis=0, keepdims=True)
    m_new = jnp.maximum(g + m_st, m_loc)
    s_prev = jnp.exp(g + m_st - m_new)
    s_loc = jnp.exp(m_loc - m_new)
    c_ref[idx] = s_prev * c_st + s_loc * c_loc
    n_ref[idx] = s_prev * n_st + s_loc * n_loc
    m_ref[idx] = jnp.broadcast_to(m_new, m_ref.shape[1:])
    return hout


def _mlstm_kernel(qkf_ref, qkb_ref, vf_ref, vb_ref, colf_ref, colb_ref, rowf_ref, rowb_ref,
                  scf_ref, scb_ref, of_ref, ob_ref, c_ref, n_ref, m_ref, *, dqk, dv):
    nh = ML_HEADS

    @pl.when(pl.program_id(0) == 0)
    def _():
        c_ref[...] = jnp.zeros_like(c_ref)
        n_ref[...] = jnp.zeros_like(n_ref)
        m_ref[...] = jnp.zeros_like(m_ref)

    qkd = nh * dqk
    for direction, (qk_ref, v_ref, col_ref, row_ref, sc_ref, o_ref) in enumerate(
            ((qkf_ref, vf_ref, colf_ref, rowf_ref, scf_ref, of_ref),
             (qkb_ref, vb_ref, colb_ref, rowb_ref, scb_ref, ob_ref))):
        for h in range(nh):
            gi = direction * nh + h
            q = qk_ref[:, h * dqk:(h + 1) * dqk]
            k = qk_ref[:, qkd + h * dqk:qkd + (h + 1) * dqk]
            v = v_ref[:, h * dv:(h + 1) * dv]
            b_col = col_ref[:, gi:gi + 1]
            a_col = col_ref[:, 2 * nh + gi:2 * nh + gi + 1]
            b_row = row_ref[gi:gi + 1, :]
            li_row = row_ref[2 * nh + gi:2 * nh + gi + 1, :]
            g = sc_ref[0, :, gi:gi + 1]
            m_loc = sc_ref[0, :, 2 * nh + gi:2 * nh + gi + 1]
            o_ref[:, h * dv:(h + 1) * dv] = _mlstm_head(
                q, k, v, b_col, a_col, b_row, li_row, g, m_loc, c_ref, n_ref, m_ref, gi,
                causal=(direction == 0))


def mlstm_bidir(qk, p, col, row, scal, cs, dqk, dv):
    t = qk.shape[0]
    nh = ML_HEADS
    nc = t // cs
    qkd2 = 2 * nh * dqk
    vd = nh * dv
    vblk = qkd2 // vd
    g4 = col.shape[1]
    fwd = lambda c: c
    bwd = lambda c: nc - 1 - c
    mk = lambda fn: dict(
        qk=pl.BlockSpec((cs, qkd2), lambda c: (fn(c), 0)),
        v=pl.BlockSpec((cs, vd), lambda c: (fn(c), vblk)),
        col=pl.BlockSpec((cs, g4), lambda c: (fn(c), 0)),
        row=pl.BlockSpec((g4, cs), lambda c: (0, fn(c))),
        sc=pl.BlockSpec((1, 1, g4), lambda c: (fn(c), 0, 0)),
        o=pl.BlockSpec((cs, vd), lambda c: (fn(c), 0)))
    sf, sb = mk(fwd), mk(bwd)
    return pl.pallas_call(
        functools.partial(_mlstm_kernel, dqk=dqk, dv=dv),
        out_shape=(jax.ShapeDtypeStruct((t, vd), F32), jax.ShapeDtypeStruct((t, vd), F32)),
        grid=(nc,),
        in_specs=[sf["qk"], sb["qk"], sf["v"], sb["v"], sf["col"], sb["col"], sf["row"], sb["row"],
                  sf["sc"], sb["sc"]],
        out_specs=(sf["o"], sb["o"]),
        scratch_shapes=[pltpu.VMEM((2 * nh, dqk, dv), F32), pltpu.VMEM((2 * nh, 1, dqk), F32),
                        pltpu.VMEM((2 * nh, 1, 128), F32)],
        compiler_params=_cparams(("arbitrary",)),
        name="mlstm_bidir",
    )(qk, qk, p, p, col, col, row, row, scal, scal)


def _ml_post_kernel(hf_ref, hb_ref, o_ref, hg_ref, out_ref, *, dv):
    nh = ML_HEADS
    for h in range(nh):
        sl = slice(h * dv, (h + 1) * dv)
        hs = hf_ref[:, sl] + hb_ref[:, sl]
        hs = hs * lax.rsqrt(jnp.mean(hs * hs, axis=-1, keepdims=True) + EPS)
        og = jax.nn.sigmoid(o_ref[:, sl].astype(F32))
        out_ref[:, sl] = (hs * hg_ref[:, sl] * og).astype(out_ref.dtype)


def ml_post(hf, hb, p, head_g, dv, tm=512):
    t, vd = hf.shape
    oblk = p.shape[1] // vd - 1
    spec = pl.BlockSpec((tm, vd), lambda i: (i, 0))
    return pl.pallas_call(
        functools.partial(_ml_post_kernel, dv=dv),
        out_shape=jax.ShapeDtypeStruct((t, vd), BF16),
        grid=(t // tm,),
        in_specs=[spec, spec, pl.BlockSpec((tm, vd), lambda i: (i, oblk)),
                  pl.BlockSpec((1, vd), lambda i: (0, 0))],
        out_specs=spec,
        compiler_params=_cparams(("parallel",)),
        name="ml_post",
    )(hf, hb, p, head_g.reshape(1, vd))


def mlstm_mixer(hm, x, gate, w_in, layer, conv_w, conv_b, gate_b, head_g, w_out):
    seq, d = hm.shape
    nh = ML_HEADS
    qk_dim = d // 2
    dqk = qk_dim // nh
    dv = d // nh
    n_main = 2 * qk_dim + 2 * d
    cs = min(ML_CHUNK, seq)
    p = matmul_cols(hm, w_in, layer, n_main)
    col, row, scal = ml_gates(hm, w_in[layer, :, n_main:], gate_b, cs)
    qk = qk_prep(p, conv_w, conv_b, qk_dim, dqk)
    hf, hb = mlstm_bidir(qk, p, col, row, scal, cs, dqk, dv)
    a = ml_post(hf, hb, p, head_g, dv)
    return outproj_residual(a, x, gate, w_out)


def _first_argmax(vals, axis, n):
    m = jnp.max(vals, axis=axis, keepdims=True)
    iota = lax.broadcasted_iota(jnp.int32, vals.shape, axis)
    idx = jnp.min(jnp.where(vals == m, iota, n), axis=axis, keepdims=True)
    return m, idx, iota


def _router_kernel(x_ref, g_ref, sc_ref, sh_ref, rwt_ref, rb_ref,
                   hf_ref, eidx_ref, rank_ref, wk_ref, cnt_ref, carry_ref):
    ne, ng = N_EXPERTS, N_GROUPS
    per = ne // ng
    tm = x_ref.shape[0]

    @pl.when(pl.program_id(0) == 0)
    def _():
        carry_ref[...] = jnp.zeros_like(carry_ref)

    hf = _normmod(x_ref[...], g_ref[...], sc_ref[...], sh_ref[...])
    hf_ref[...] = hf
    logits = lax.dot_general(rwt_ref[...], hf, (((1,), (1,)), ((), ())),
                             precision=lax.Precision.HIGHEST, preferred_element_type=F32)
    scores = jax.nn.sigmoid(logits)
    sel = scores + rb_ref[...]
    sel3 = sel.reshape(ng, per, tm)
    m1, i1, io3 = _first_argmax(sel3, 1, per)
    m2 = jnp.max(jnp.where(io3 == i1, -jnp.inf, sel3), axis=1, keepdims=True)
    gs = (m1 + m2).reshape(ng, tm)
    gsel = jnp.zeros((ng, tm), F32)
    for _ in range(TOPK_GROUPS):
        _, gi, iog = _first_argmax(gs, 0, ng)
        hit = iog == gi
        gsel = jnp.where(hit, 1.0, gsel)
        gs = jnp.where(hit, -jnp.inf, gs)
    gmask = jnp.broadcast_to(gsel.reshape(ng, 1, tm), (ng, per, tm)).reshape(ne, tm)
    cand = jnp.where(gmask > 0.5, sel, -jnp.inf)
    picked = []
    chosen = jnp.zeros((ne, tm), F32)
    for _ in range(TOP_K):
        _, ei, ioe = _first_argmax(cand, 0, ne)
        hit = ioe == ei
        picked.append((ei, hit))
        chosen = jnp.where(hit, 1.0, chosen)
        cand = jnp.where(hit, -jnp.inf, cand)
    r = lax.broadcasted_iota(jnp.int32, (tm, tm), 0)
    c = lax.broadcasted_iota(jnp.int32, (tm, tm), 1)
    before = (r < c).astype(BF16)
    ranks = jnp.dot(chosen.astype(BF16), before, preferred_element_type=F32) + carry_ref[:, 0:1]
    carry_ref[...] = carry_ref[...] + jnp.sum(chosen, axis=1, keepdims=True)
    cnt_ref[...] = carry_ref[...]
    wks = [jnp.sum(jnp.where(hit, scores, 0.0), axis=0, keepdims=True) for _, hit in picked]
    wsum = functools.reduce(lambda a, b: a + b, wks)
    eidx_ref[...] = jnp.zeros_like(eidx_ref)
    rank_ref[...] = jnp.zeros_like(rank_ref)
    wk_ref[...] = jnp.zeros_like(wk_ref)
    for j, ((ei, hit), wk) in enumerate(zip(picked, wks)):
        eidx_ref[j:j + 1, :] = ei
        rank_ref[j:j + 1, :] = jnp.sum(jnp.where(hit, ranks, 0.0), axis=0, keepdims=True).astype(jnp.int32)
        wk_ref[j:j + 1, :] = wk / wsum * ROUTED_SCALE


def moe_router(x, g, sc, sh, router_w, router_bias, tm=512):
    t, d = x.shape
    ne = N_EXPERTS
    vec = pl.BlockSpec((1, d), lambda i: (0, 0))
    lane = pl.BlockSpec((8, tm), lambda i: (0, i))
    return pl.pallas_call(
        _router_kernel,
        out_shape=(jax.ShapeDtypeStruct((t, d), F32), jax.ShapeDtypeStruct((8, t), jnp.int32),
                   jax.ShapeDtypeStruct((8, t), jnp.int32), jax.ShapeDtypeStruct((8, t), F32),
                   jax.ShapeDtypeStruct((ne, 128), F32)),
        grid=(t // tm,),
        in_specs=[pl.BlockSpec((tm, d), lambda i: (i, 0)), vec, vec, vec,
                  pl.BlockSpec((ne, d), lambda i: (0, 0)), pl.BlockSpec((ne, 1), lambda i: (0, 0))],
        out_specs=(pl.BlockSpec((tm, d), lambda i: (i, 0)), lane, lane, lane,
                   pl.BlockSpec((ne, 128), lambda i: (0, 0))),
        scratch_shapes=[pltpu.VMEM((ne, 128), F32)],
        compiler_params=_cparams(("arbitrary",)),
        name="moe_router",
    )(x, g, sc, sh, router_w.T, router_bias.reshape(ne, 1))


def _plan_kernel(eidx_ref, rank_ref, ps_ref, dest_ref):
    ne = N_EXPERTS
    tm = eidx_ref.shape[1]
    io = lax.broadcasted_iota(jnp.int32, (ne, tm), 0)
    ps = ps_ref[...]
    dest_ref[...] = jnp.zeros_like(dest_ref)
    for j in range(TOP_K):
        hit = io == eidx_ref[j:j + 1, :]
        base = jnp.sum(jnp.where(hit, ps, 0.0), axis=0, keepdims=True)
        dest_ref[j:j + 1, :] = base.astype(jnp.int32) + rank_ref[j:j + 1, :]


def moe_plan(eidx, rank, pad_start, tm=2048):
    t = eidx.shape[1]
    tm = min(tm, t)
    lane = pl.BlockSpec((8, tm), lambda i: (0, i))
    return pl.pallas_call(
        _plan_kernel,
        out_shape=jax.ShapeDtypeStruct((8, t), jnp.int32),
        grid=(t // tm,),
        in_specs=[lane, lane, pl.BlockSpec((N_EXPERTS, 1), lambda i: (0, 0))],
        out_specs=lane,
        compiler_params=_cparams(("parallel",)),
        name="moe_plan",
    )(eidx, rank, pad_start.astype(F32).reshape(N_EXPERTS, 1))


SUBLANES = 8
_PAD_PIECES = tuple(1 << k for k in reversed(range(3, MOE_BLOCK.bit_length() - 1)))


def _dispatch_kernel(dest_ref, zs_ref, zn_ref, hf_ref, xb_ref, zbuf, sem, zsem, *, t_total):
    tm = hf_ref.shape[0]
    base = pl.program_id(0) * tm

    def row_copy(t, j):
        d = dest_ref[j * t_total + base + t]
        return pltpu.make_async_copy(hf_ref.at[pl.ds(t, 1)], xb_ref.at[pl.ds(d, 1)], sem)

    def start(t, carry):
        for j in range(TOP_K):
            row_copy(t, j).start()
        return carry

    def wait_all():
        for _ in range(TOP_K):
            pltpu.make_async_copy(hf_ref, xb_ref.at[pl.ds(0, tm)], sem).wait()

    def fill(do_start):
        def body(e, carry):
            zs, zn = zs_ref[e], zn_ref[e]
            end = zs + zn

            def piece(src, dst):
                cp = pltpu.make_async_copy(src, dst, zsem)
                cp.start() if do_start else cp.wait()

            for r in range(SUBLANES - 1):
                @pl.when(r < (zn & (SUBLANES - 1)))
                def _():
                    piece(zbuf.at[pl.ds(0, 1)], xb_ref.at[pl.ds(zs + r, 1)])
            for p in _PAD_PIECES:
                @pl.when((zn & p) != 0)
                def _():
                    q = pl.multiple_of(end - (zn & ~(p - 1)), SUBLANES)
                    piece(zbuf.at[pl.ds(0, p)], xb_ref.at[pl.ds(q, p)])
            return carry
        lax.fori_loop(0, N_EXPERTS, body, 0)

    first = pl.program_id(0) == 0

    @pl.when(first)
    def _():
        zbuf[...] = jnp.zeros_like(zbuf)
        fill(True)

    lax.fori_loop(0, tm, start, 0)

    @pl.when(first)
    def _():
        fill(False)

    wait_all()


def moe_dispatch(dest, zero_start, zero_len, hf, n_slots, tm=256):
    t, d = hf.shape
    return pl.pallas_call(
        functools.partial(_dispatch_kernel, t_total=t),
        out_shape=jax.ShapeDtypeStruct((n_slots, d), hf.dtype),
        grid_spec=pltpu.PrefetchScalarGridSpec(
            num_scalar_prefetch=3,
            grid=(t // tm,),
            in_specs=[pl.BlockSpec((tm, d), lambda i, *_: (i, 0))],
            out_specs=pl.BlockSpec(memory_space=pl.ANY),
            scratch_shapes=[pltpu.VMEM((MOE_BLOCK // 2, d), hf.dtype),
                            pltpu.SemaphoreType.DMA(()), pltpu.SemaphoreType.DMA(())]),
        compiler_params=_cparams(("arbitrary",)),
        name="moe_dispatch",
    )(dest, zero_start, zero_len, hf)


def _ffn(x, wg, wu, wd):
    hg = jnp.dot(x, wg, preferred_element_type=F32)
    hu = jnp.dot(x, wu, preferred_element_type=F32)
    h = (hg * jax.nn.sigmoid(hg) * hu).astype(BF16)
    return jnp.dot(h, wd, preferred_element_type=F32)


def _expert_kernel(be_ref, na_ref, first_ref, next_ref, slot_ref, x_ref, wg_hbm, wu_hbm, wd_hbm, o_ref,
                   wgf, wuf, wdf, wgb, wub, wdb, sems, *, layer):
    b = pl.program_id(0)
    active = b < na_ref[0]

    def fetch(e, s):
        return (pltpu.make_async_copy(wg_hbm.at[layer, e], wgf.at[s], sems.at[s, 0]),
                pltpu.make_async_copy(wu_hbm.at[layer, e], wuf.at[s], sems.at[s, 1]),
                pltpu.make_async_copy(wd_hbm.at[layer, e], wdf.at[s], sems.at[s, 2]))

    @pl.when(jnp.logical_and(active, first_ref[b] == 1))
    def _():
        s = slot_ref[b]

        @pl.when(b == 0)
        def _():
            for cp in fetch(be_ref[b], s):
                cp.start()

        for cp in fetch(be_ref[b], s):
            cp.wait()
        nxt = next_ref[b]

        @pl.when(nxt >= 0)
        def _():
            for cp in fetch(nxt, 1 - s):
                cp.start()

        wgb[...] = wgf[s].astype(BF16)
        wub[...] = wuf[s].astype(BF16)
        wdb[...] = wdf[s].astype(BF16)

    @pl.when(active)
    def _():
        o_ref[...] = _ffn(x_ref[...].astype(BF16), wgb[...], wub[...], wdb[...])


def moe_experts(blk_e, n_active, first, nxt, slot, xb, w_gate, w_up, w_down, layer):
    p, d = xb.shape
    ff = w_gate.shape[3]
    nblk = p // MOE_BLOCK
    row = lambda b, be, na, *_: (jnp.minimum(b, na[0] - 1), 0)
    hbm = pl.BlockSpec(memory_space=pl.ANY)
    return pl.pallas_call(
        functools.partial(_expert_kernel, layer=layer),
        out_shape=jax.ShapeDtypeStruct((p, d), F32),
        grid_spec=pltpu.PrefetchScalarGridSpec(
            num_scalar_prefetch=5,
            grid=(nblk,),
            in_specs=[pl.BlockSpec((MOE_BLOCK, d), row), hbm, hbm, hbm],
            out_specs=pl.BlockSpec((MOE_BLOCK, d), row),
            scratch_shapes=[pltpu.VMEM((2, d, ff), F32), pltpu.VMEM((2, d, ff), F32),
                            pltpu.VMEM((2, ff, d), F32),
                            pltpu.VMEM((d, ff), BF16), pltpu.VMEM((d, ff), BF16),
                            pltpu.VMEM((ff, d), BF16), pltpu.SemaphoreType.DMA((2, 3))]),
        compiler_params=_cparams(("arbitrary",)),
        name="moe_experts",
    )(blk_e, n_active, first, nxt, slot, xb, w_gate, w_up, w_down)


def _combine_kernel(dest_ref, yb_ref, x_ref, hf_ref, wk_ref, gate_ref, wg_ref, wu_ref, wd_ref, fg_ref, o_ref,
                    buf, wgb, wub, wdb, sem, *, t_total, final_norm):
    tm, d = x_ref.shape
    base = pl.program_id(0) * tm

    @pl.when(pl.program_id(0) == 0)
    def _():
        wgb[...] = wg_ref[...].astype(BF16)
        wub[...] = wu_ref[...].astype(BF16)
        wdb[...] = wd_ref[...].astype(BF16)

    def row_copy(t, j):
        dst = dest_ref[j * t_total + base + t]
        return pltpu.make_async_copy(yb_ref.at[pl.ds(dst, 1)], buf.at[j, pl.ds(t, 1)], sem)

    def start(t, carry):
        for j in range(TOP_K):
            row_copy(t, j).start()
        return carry

    lax.fori_loop(0, tm, start, 0)
    acc = _ffn(hf_ref[...].astype(BF16), wgb[...], wub[...], wdb[...])
    for j in range(TOP_K):
        pltpu.make_async_copy(yb_ref.at[pl.ds(0, tm)], buf.at[j], sem).wait()
    for j in range(TOP_K):
        acc = acc + wk_ref[:, j:j + 1] * buf[j]
    y = x_ref[...] + gate_ref[...] * acc
    if final_norm:
        y = (y * lax.rsqrt(jnp.mean(y * y, axis=-1, keepdims=True) + EPS)) * fg_ref[...]
    o_ref[...] = y


def moe_combine(dest, yb, x, hf, wk_t, gate, sh_gate, sh_up, sh_down, final_g, final_norm, tm=128):
    t, d = x.shape
    ff = sh_gate.shape[1]
    tm = min(tm, t)
    tile = lambda i, dest: (i, 0)
    full = lambda shape: pl.BlockSpec(shape, lambda i, dest: (0, 0))
    return pl.pallas_call(
        functools.partial(_combine_kernel, t_total=t, final_norm=final_norm),
        out_shape=jax.ShapeDtypeStruct((t, d), F32),
        grid_spec=pltpu.PrefetchScalarGridSpec(
            num_scalar_prefetch=1,
            grid=(t // tm,),
            in_specs=[pl.BlockSpec(memory_space=pl.ANY),
                      pl.BlockSpec((tm, d), tile), pl.BlockSpec((tm, d), tile),
                      pl.BlockSpec((tm, 8), tile), full((1, d)),
                      full((d, ff)), full((d, ff)), full((ff, d)), full((1, d))],
            out_specs=pl.BlockSpec((tm, d), tile),
            scratch_shapes=[pltpu.VMEM((TOP_K, tm, d), F32),
                            pltpu.VMEM((d, ff), BF16), pltpu.VMEM((d, ff), BF16), pltpu.VMEM((ff, d), BF16),
                            pltpu.SemaphoreType.DMA(())]),
        compiler_params=_cparams(("arbitrary",)),
        name="moe_combine",
    )(dest, yb, x, hf, wk_t, gate, sh_gate, sh_up, sh_down, final_g)


def moe_layer(x, g, sc, sh, gate, router_w, router_bias, w_gate, w_up, w_down, sh_gate, sh_up, sh_down,
              layer, final_g, final_norm):
    t, d = x.shape
    ne = N_EXPERTS
    hf, eidx, rank, wk, cnt = moe_router(x, g, sc, sh, router_w, router_bias)
    counts = cnt[:, 0].astype(jnp.int32)
    padded = (counts + MOE_BLOCK - 1) // MOE_BLOCK * MOE_BLOCK
    pad_end = jnp.cumsum(padded)
    pad_start = pad_end - padded
    n_slots = (t * TOP_K + ne * (MOE_BLOCK - 1) + MOE_BLOCK - 1) // MOE_BLOCK * MOE_BLOCK
    nblk = n_slots // MOE_BLOCK
    dest = moe_plan(eidx, rank, pad_start)[:TOP_K].reshape(TOP_K * t)
    blk_start = jnp.arange(nblk, dtype=jnp.int32) * MOE_BLOCK
    n_active = (pad_end[-1] // MOE_BLOCK).astype(jnp.int32).reshape(1)
    blk_e = jnp.minimum(jnp.sum(blk_start[:, None] >= pad_end[None, :], axis=1), ne - 1).astype(jnp.int32)
    last_e = jnp.max(jnp.where(counts > 0, jnp.arange(ne, dtype=jnp.int32), 0))
    live = blk_start < pad_end[-1]
    blk_e = jnp.where(live, blk_e, last_e)
    first = jnp.logical_and(live, jnp.concatenate([jnp.ones((1,), bool), blk_e[1:] != blk_e[:-1]]))
    slot = ((jnp.cumsum(first.astype(jnp.int32)) - 1) % 2).astype(jnp.int32)
    ids = jnp.arange(ne, dtype=jnp.int32)
    later = jnp.logical_and(ids[None, :] > ids[:, None], counts[None, :] > 0)
    next_e = jnp.min(jnp.where(later, ids[None, :], ne), axis=1)
    next_e = jnp.where(next_e < ne, next_e, -1)
    nxt = jnp.sum(jnp.where(blk_e[:, None] == ids[None, :], next_e[None, :], 0), axis=1).astype(jnp.int32)

    xb = moe_dispatch(dest, pad_start + counts, padded - counts, hf, n_slots)
    yb = moe_experts(blk_e, n_active, first.astype(jnp.int32), nxt, slot, xb, w_gate, w_up, w_down, layer)
    return moe_combine(dest, yb, x, hf, wk.T, gate, sh_gate, sh_up, sh_down, final_g, final_norm)


def kernel(x, c, ada_w, ada_b, norm_mix_g, norm_ffn_g, hy_w_in, hy_conv_w, hy_conv_b, hy_f_w0, hy_f_b0, hy_f_w1, hy_f_b1, hy_f_w2, hy_f_b2, hy_f_freq, hy_f_wout, hy_skip, hy_w_out, ml_w_in, ml_conv_w, ml_conv_b, ml_gate_b, ml_head_g, ml_w_out, moe_router_w, moe_router_bias, moe_w_gate, moe_w_up, moe_w_down, sh_w_gate, sh_w_up, sh_w_down, final_g):
    bsz, seq, d = x.shape
    assert bsz == 1, "kernels are written for a single sequence"
    depth = ada_w.shape[0]
    xs = x.reshape(seq, d)
    ada = ada_proj(c, ada_w, ada_b)
    for layer in range(depth):
        sh_m, sc_m, g_m, sh_f, sc_f, g_f = (ada[layer, :, k * d:(k + 1) * d] for k in range(6))
        gm = norm_mix_g[layer].reshape(1, d)
        j = layer // 2
        if layer % 2 == 0:
            hm = normmod_slab(xs, gm, sc_m, sh_m)
            xs = hyena_mixer(hm, xs, g_m, hy_w_in, j, hy_conv_w[j], hy_conv_b[j], hy_f_w0[j], hy_f_b0[j],
                             hy_f_w1[j], hy_f_b1[j], hy_f_w2[j], hy_f_b2[j], hy_f_freq[j], hy_f_wout[j],
                             hy_skip[j], hy_w_out[j])
        else:
            hm = normmod(xs, gm, sc_m, sh_m)
            xs = mlstm_mixer(hm, xs, g_m, ml_w_in, j, ml_conv_w[j], ml_conv_b[j], ml_gate_b[j],
                             ml_head_g[j], ml_w_out[j])
        xs = moe_layer(xs, norm_ffn_g[layer].reshape(1, d), sc_f, sh_f, g_f, moe_router_w[layer],
                       moe_router_bias[layer], moe_w_gate, moe_w_up, moe_w_down,
                       sh_w_gate[layer], sh_w_up[layer], sh_w_down[layer], layer,
                       final_g.reshape(1, d), layer == depth - 1)
    return xs.reshape(bsz, seq, d)
```

```python
import functools
import math

import numpy as np
import jax
import jax.numpy as jnp
from jax import lax
from jax.experimental import pallas as pl
from jax.experimental.pallas import tpu as pltpu

F32 = jnp.float32
BF16 = jnp.bfloat16
EPS = 1e-6

V7X_VMEM_BYTES = 64 * 1024 * 1024
VMEM_LIMIT = V7X_VMEM_BYTES - 8 * 1024 * 1024

SHORT_CONV = 3
HY_BANDS = 16
HY_DECAY_TARGET = 1e-2
HY_FAST_DECAY = 0.3
HY_SLOW_DECAY = 1.5
ML_HEADS = 8
ML_CHUNK = 256
N_EXPERTS = 64
TOP_K = 6
N_GROUPS = 8
TOPK_GROUPS = 4
ROUTED_SCALE = 2.5
MOE_BLOCK = 256
DFT_NB = 128


def _cparams(sem):
    return pltpu.CompilerParams(dimension_semantics=sem, vmem_limit_bytes=VMEM_LIMIT)


def _ada_kernel(c_ref, w_ref, b_ref, o_ref, *, tc):
    rows, n = w_ref.shape[1], w_ref.shape[2]

    @pl.when(pl.program_id(0) == 0)
    def _():
        o_ref[...] = b_ref[0]

    c = c_ref[...]
    cs = c * jax.nn.sigmoid(c)
    for j in range(n // tc):
        part = (w_ref[0, :, j * tc:(j + 1) * tc] * cs).reshape(rows // 8, 8, tc).sum(axis=0)
        o_ref[:, j * tc:(j + 1) * tc] += jnp.sum(part, axis=0, keepdims=True)


def ada_proj(c, ada_w, ada_b, layer, rows=128):
    depth, d, n = ada_w.shape
    return pl.pallas_call(
        functools.partial(_ada_kernel, tc=2048),
        out_shape=jax.ShapeDtypeStruct((1, n), F32),
        grid=(d // rows,),
        in_specs=[pl.BlockSpec((rows, 1), lambda r: (r, 0)),
                  pl.BlockSpec((1, rows, n), lambda r: (layer, r, 0)),
                  pl.BlockSpec((1, 1, n), lambda r: (layer, 0, 0))],
        out_specs=pl.BlockSpec((1, n), lambda r: (0, 0)),
        compiler_params=_cparams(("arbitrary",)),
        name="ada_proj",
    )(c.reshape(d, 1), ada_w, ada_b.reshape(depth, 1, n))


def _normmod(x, g, sc, sh):
    r = lax.rsqrt(jnp.mean(x * x, axis=-1, keepdims=True) + EPS)
    return (x * r) * g * (1.0 + sc) + sh


def _normmod_kernel(x_ref, g_ref, sc_ref, sh_ref, o_ref):
    o_ref[...] = _normmod(x_ref[...], g_ref[...], sc_ref[...], sh_ref[...]).astype(o_ref.dtype)


def normmod(x, g, sc, sh, out_dtype=BF16, tm=512):
    t, d = x.shape
    vec = pl.BlockSpec((1, d), lambda i: (0, 0))
    return pl.pallas_call(
        _normmod_kernel,
        out_shape=jax.ShapeDtypeStruct((t, d), out_dtype),
        grid=(t // tm,),
        in_specs=[pl.BlockSpec((tm, d), lambda i: (i, 0)), vec, vec, vec],
        out_specs=pl.BlockSpec((tm, d), lambda i: (i, 0)),
        compiler_params=_cparams(("parallel",)),
        name="normmod",
    )(x, g, sc, sh)


def _mm_kernel(a_ref, w_ref, o_ref, wb_ref):
    @pl.when(pl.program_id(1) == 0)
    def _():
        wb_ref[...] = w_ref[0].astype(BF16)

    o_ref[...] = jnp.dot(a_ref[...], wb_ref[...], preferred_element_type=F32).astype(o_ref.dtype)


def matmul_cols(a, w, layer, n_out, out_dtype=BF16, tm=1024, tn=512):
    m, k = a.shape
    tm = min(tm, m)
    return pl.pallas_call(
        _mm_kernel,
        out_shape=jax.ShapeDtypeStruct((m, n_out), out_dtype),
        grid=(n_out // tn, m // tm),
        in_specs=[pl.BlockSpec((tm, k), lambda j, i: (i, 0)),
                  pl.BlockSpec((1, k, tn), lambda j, i: (layer, 0, j))],
        out_specs=pl.BlockSpec((tm, tn), lambda j, i: (i, j)),
        scratch_shapes=[pltpu.VMEM((k, tn), BF16)],
        compiler_params=_cparams(("arbitrary", "arbitrary")),
        name="matmul_cols",
    )(a, w)


def _outproj_kernel(a_ref, x_ref, gate_ref, w_ref, o_ref, wb_ref):
    @pl.when(pl.program_id(1) == 0)
    def _():
        wb_ref[...] = w_ref[...].astype(BF16)

    y = jnp.dot(a_ref[...], wb_ref[...], preferred_element_type=F32)
    o_ref[...] = x_ref[...] + gate_ref[...] * y


def outproj_residual(a, x, gate, w, tm=1024, tn=512):
    t, k = a.shape
    d = w.shape[1]
    tm = min(tm, t)
    return pl.pallas_call(
        _outproj_kernel,
        out_shape=jax.ShapeDtypeStruct((t, d), F32),
        grid=(d // tn, t // tm),
        in_specs=[pl.BlockSpec((tm, k), lambda j, i: (i, 0)),
                  pl.BlockSpec((tm, tn), lambda j, i: (i, j)),
                  pl.BlockSpec((1, tn), lambda j, i: (0, j)),
                  pl.BlockSpec((k, tn), lambda j, i: (0, j))],
        out_specs=pl.BlockSpec((tm, tn), lambda j, i: (i, j)),
        scratch_shapes=[pltpu.VMEM((k, tn), BF16)],
        compiler_params=_cparams(("arbitrary", "arbitrary")),
        name="outproj_residual",
    )(a, x, gate, w)


HALO = 16


def _conv3(main, prev, nxt, w, b, i, nt):
    tm = main.shape[0]
    rows = lax.broadcasted_iota(jnp.int32, main.shape, 0)
    pr = jnp.where(i > 0, prev[HALO - 1:HALO, :], 0.0)
    nx = jnp.where(i < nt - 1, nxt[0:1, :], 0.0)
    up = jnp.where(rows == 0, pr, pltpu.roll(main, 1, 0))
    dn = jnp.where(rows == tm - 1, nx, pltpu.roll(main, tm - 1, 0))
    return up * w[0:1, :] + main * w[1:2, :] + dn * w[2:3, :] + b


def _halo_specs(tm, tc, t, col_fn):
    nh = t // HALO
    r = tm // HALO
    return [
        pl.BlockSpec((tm, tc), lambda i, j: (i, col_fn(j))),
        pl.BlockSpec((HALO, tc), lambda i, j: (jnp.maximum(i * r - 1, 0), col_fn(j))),
        pl.BlockSpec((HALO, tc), lambda i, j: (jnp.minimum((i + 1) * r, nh - 1), col_fn(j))),
    ]


HY_GROUP = 8


def _normmod_slab_kernel(x_ref, g_ref, sc_ref, sh_ref, perm_ref, o_ref):
    s1, grp, d = x_ref.shape
    x = x_ref[...].reshape(s1 * grp, d)
    y = _normmod(x, g_ref[...], sc_ref[...], sh_ref[...]).astype(BF16)
    o_ref[...] = jnp.dot(perm_ref[...], y, preferred_element_type=F32).astype(o_ref.dtype)


def normmod_slab(x, g, sc, sh):
    t, d = x.shape
    s1 = t // DFT_NB
    grp = HY_GROUP
    n = s1 * grp
    perm = np.zeros((n, n), np.float32)
    src = np.arange(n)
    perm[(src % grp) * s1 + src // grp, src] = 1.0
    vec = pl.BlockSpec((1, d), lambda i: (0, 0))
    return pl.pallas_call(
        _normmod_slab_kernel,
        out_shape=jax.ShapeDtypeStruct((t, d), BF16),
        grid=(DFT_NB // grp,),
        in_specs=[pl.BlockSpec((s1, grp, d), lambda i: (0, i, 0)), vec, vec, vec,
                  pl.BlockSpec((n, n), lambda i: (0, 0))],
        out_specs=pl.BlockSpec((n, d), lambda i: (i, 0)),
        compiler_params=_cparams(("parallel",)),
        name="normmod_slab",
    )(x.reshape(s1, DFT_NB, d), g, sc, sh, jnp.asarray(perm).astype(BF16))


def _shift_rows(x, down):
    n = x.shape[0]
    rows = lax.broadcasted_iota(jnp.int32, x.shape, 0)
    if down:
        return jnp.where(rows == 0, 0.0, pltpu.roll(x, 1, 0))
    return jnp.where(rows == n - 1, 0.0, pltpu.roll(x, n - 1, 0))


def _hy_gate_kernel(*refs):
    mains, prevs, nexts = refs[0:3], refs[3:6], refs[6:9]
    ws, bs = refs[9:12], refs[12:15]
    vg_ref, x0_ref = refs[15:17]
    i = pl.program_id(0)
    nt = pl.num_programs(0)
    s1 = prevs[0].shape[0]
    d = mains[0].shape[1]
    grp = mains[0].shape[0] // s1

    def halo(ref, edge, down):
        h = ref[...].astype(F32)
        return jnp.where(edge, _shift_rows(h, down), h)

    ups = [halo(prevs[g], i == 0, True) for g in range(3)]
    dns = [halo(nexts[g], i == nt - 1, False) for g in range(3)]
    for j in range(grp):
        out = []
        for g in range(3):
            slab = lambda k: mains[g][k * s1:(k + 1) * s1, :].astype(F32)
            up = ups[g] if j == 0 else slab(j - 1)
            dn = dns[g] if j == grp - 1 else slab(j + 1)
            w = ws[g][...]
            out.append(up * w[0:1, :] + slab(j) * w[1:2, :] + dn * w[2:3, :] + bs[g][...])
        x0, x1, v = out
        vg_ref[:, j * d:(j + 1) * d] = (v * x1).astype(vg_ref.dtype)
        x0_ref[:, j * d:(j + 1) * d] = x0.astype(x0_ref.dtype)


def hy_gate(p, conv_w, conv_b):
    t, c3 = p.shape
    d = c3 // 3
    s1 = t // DFT_NB
    grp = HY_GROUP
    main = lambda g: pl.BlockSpec((grp * s1, d), lambda i: (i, g))
    prev = lambda g: pl.BlockSpec((s1, d), lambda i: ((i * grp + DFT_NB - 1) % DFT_NB, g))
    nxt = lambda g: pl.BlockSpec((s1, d), lambda i: (((i + 1) * grp) % DFT_NB, g))
    specs = [f(g) for f in (main, prev, nxt) for g in range(3)]
    specs += [pl.BlockSpec((SHORT_CONV, d), lambda i, g=g: (0, g)) for g in range(3)]
    specs += [pl.BlockSpec((1, d), lambda i, g=g: (0, g)) for g in range(3)]
    out_spec = pl.BlockSpec((s1, grp * d), lambda i: (0, i))
    cb = conv_b.reshape(1, c3)
    out = jax.ShapeDtypeStruct((s1, DFT_NB * d), BF16)
    return pl.pallas_call(
        _hy_gate_kernel,
        out_shape=(out, out),
        grid=(DFT_NB // grp,),
        in_specs=specs,
        out_specs=(out_spec, out_spec),
        compiler_params=_cparams(("parallel",)),
        name="hy_gate",
    )(*([p] * 9), conv_w, conv_w, conv_w, cb, cb, cb)


def _hy_features(seq):
    t = np.linspace(0.0, 1.0, seq, dtype=np.float64)[:, None]
    ang = 2.0 * math.pi * np.arange(seq, dtype=np.float64)[:, None] / seq
    bands = np.linspace(1e-4, HY_BANDS - 1, HY_BANDS, dtype=np.float64)
    z = np.concatenate([t, np.cos(ang * bands), -np.sin(ang * bands)], axis=-1)
    zp = np.zeros((seq, 128), np.float32)
    zp[:, :z.shape[1]] = z
    s1 = seq // DFT_NB
    return zp.reshape(s1, DFT_NB, 128).transpose(1, 0, 2).reshape(seq, 128)


def _hy_deltas(d):
    return np.abs(np.linspace(math.log(HY_DECAY_TARGET) / HY_FAST_DECAY,
                              math.log(HY_DECAY_TARGET) / HY_SLOW_DECAY, d,
                              dtype=np.float64)).astype(np.float32)[None, :]


def _hy_filter_kernel(z_ref, w0_ref, b0_ref, w1_ref, b1_ref, w2_ref, b2_ref, fr_ref, wo_ref,
                      dl_ref, o_ref, *, seq, grp):
    hi = lax.Precision.HIGHEST
    fr = fr_ref[...]
    a = jnp.sin(fr * (jnp.dot(z_ref[...], w0_ref[...], precision=hi, preferred_element_type=F32)
                      + b0_ref[...]))
    a = jnp.sin(fr * (jnp.dot(a, w1_ref[...], precision=hi, preferred_element_type=F32) + b1_ref[...]))
    a = jnp.sin(fr * (jnp.dot(a, w2_ref[...], precision=hi, preferred_element_type=F32) + b2_ref[...]))
    k = jnp.dot(a.astype(BF16), wo_ref[...].astype(BF16), preferred_element_type=F32)
    d2 = k.shape[1]
    d = d2 // 2
    s1 = k.shape[0] // grp
    for j in range(grp):
        s2 = pl.program_id(0) * grp + j
        tt = lax.broadcasted_iota(jnp.int32, (s1, 1), 0) * DFT_NB + s2
        window = jnp.exp(-(tt.astype(F32) * (1.0 / (seq - 1))) * dl_ref[...])
        kj = k[j * s1:(j + 1) * s1]
        o_ref[:, j * d2:j * d2 + d] = (kj[:, :d] * window).astype(o_ref.dtype)
        o_ref[:, j * d2 + d:(j + 1) * d2] = jnp.where(tt == 0, 0.0, kj[:, d:] * window).astype(o_ref.dtype)


def hy_filter(seq, w0, b0, w1, b1, w2, b2, freq, wout, grp=4):
    fw = w1.shape[0]
    d2 = wout.shape[1]
    s1 = seq // DFT_NB
    z = jnp.asarray(_hy_features(seq))
    w0p = jnp.zeros((128, fw), F32).at[:w0.shape[0]].set(w0)
    full = lambda shape: pl.BlockSpec(shape, lambda i: (0,) * len(shape))
    r = lambda v: v.reshape(1, -1)
    return pl.pallas_call(
        functools.partial(_hy_filter_kernel, seq=seq, grp=grp),
        out_shape=jax.ShapeDtypeStruct((s1, DFT_NB * d2), BF16),
        grid=(DFT_NB // grp,),
        in_specs=[pl.BlockSpec((grp * s1, 128), lambda i: (i, 0)),
                  full((128, fw)), full((1, fw)), full((fw, fw)), full((1, fw)),
                  full((fw, fw)), full((1, fw)), full((1, fw)), full((fw, d2)), full((1, d2 // 2))],
        out_specs=pl.BlockSpec((s1, grp * d2), lambda i: (0, i)),
        compiler_params=_cparams(("parallel",)),
        name="hy_filter",
    )(z, w0p, r(b0), w1, r(b1), w2, r(b2), r(freq), wout, jnp.asarray(_hy_deltas(d2 // 2)))


DFT_ROW_ALIGN = 16


def _dft_slabs(seq):
    nf = (2 * seq // DFT_NB) // 2 + 1
    return nf, -(-nf // DFT_ROW_ALIGN) * DFT_ROW_ALIGN


@functools.lru_cache(maxsize=None)
def _dft_tables(seq):
    n = 2 * seq
    nb = DFT_NB
    na = n // nb
    nf, nfp = _dft_slabs(seq)
    s1 = np.arange(na // 2)
    f1 = np.arange(nf)
    ph = 2.0 * math.pi * np.outer(f1, s1) / na
    fs = np.zeros((2 * nfp, na // 2))
    fs[:nf] = np.cos(ph)
    fs[nfp:nfp + nf] = -np.sin(ph)
    f2 = np.arange(nb)
    s2 = np.arange(nb)
    freq = f1[:, None, None] + na * f2[None, :, None]
    th = 2.0 * math.pi * ((freq * s2[None, None, :]) % n) / n
    gr, gi = np.cos(th), -np.sin(th)
    g = np.concatenate([np.concatenate([gr, -gi], axis=2),
                        np.concatenate([gi, gr], axis=2)], axis=1)
    ginv = np.transpose(g, (0, 2, 1))
    t1 = np.arange(na // 2)
    ph2 = 2.0 * math.pi * np.outer(t1, f1) / na
    wgt = np.full((nf,), 2.0)
    wgt[0] = wgt[nf - 1] = 1.0
    finv = np.zeros((na // 2, 2 * nfp))
    finv[:, :nf] = np.cos(ph2) * wgt
    finv[:, nfp:nfp + nf] = -np.sin(ph2) * wgt
    return tuple(a.astype(np.float32) for a in (fs, g, ginv, finv))


def _dft1_kernel(f_ref, x_ref, o_ref):
    y = jnp.dot(f_ref[...].astype(BF16), x_ref[...], preferred_element_type=F32)
    o_ref[...] = y.reshape(o_ref.shape).astype(o_ref.dtype)


def dft_stage1(x2d, fs, tn=8192):
    k, cols = x2d.shape
    na = fs.shape[0] // 2
    return pl.pallas_call(
        _dft1_kernel,
        out_shape=jax.ShapeDtypeStruct((2, na, cols), BF16),
        grid=(cols // tn,),
        in_specs=[pl.BlockSpec((2 * na, k), lambda c: (0, 0)),
                  pl.BlockSpec((k, tn), lambda c: (0, c))],
        out_specs=pl.BlockSpec((2, na, tn), lambda c: (0, 0, c)),
        compiler_params=_cparams(("parallel",)),
        name="dft_stage1",
    )(fs, x2d)


def _hspec_kernel(g_ref, af_ref, ab_ref, o_ref, *, scale):
    nb2, dt = o_ref.shape[1], o_ref.shape[2]
    g = g_ref[0].astype(BF16)
    hf = jnp.dot(g, af_ref[...].reshape(nb2, dt), preferred_element_type=F32)
    hb = jnp.dot(g, ab_ref[...].reshape(nb2, dt), preferred_element_type=F32)
    nb = nb2 // 2
    o_ref[0, :nb, :] = ((hf[:nb] + hb[:nb]) * scale).astype(o_ref.dtype)
    o_ref[0, nb:, :] = ((hf[nb:] - hb[nb:]) * scale).astype(o_ref.dtype)


def filter_spectrum(ak, g, d, n, dt=2048):
    nb = ak.shape[2]
    nf = g.shape[0]
    nd = d // dt
    scale = 1.0 / n
    return pl.pallas_call(
        functools.partial(_hspec_kernel, scale=scale),
        out_shape=jax.ShapeDtypeStruct((nf, 2 * nb, d), BF16),
        grid=(nf, nd),
        in_specs=[pl.BlockSpec((1, 2 * nb, 2 * nb), lambda f, j: (f, 0, 0)),
                  pl.BlockSpec((2, 1, nb, dt), lambda f, j: (0, f, 0, j)),
                  pl.BlockSpec((2, 1, nb, dt), lambda f, j: (0, f, 0, nd + j))],
        out_specs=pl.BlockSpec((1, 2 * nb, dt), lambda f, j: (f, 0, j)),
        compiler_params=_cparams(("parallel", "parallel")),
        name="filter_spectrum",
    )(g, ak, ak)


def _xspec_kernel(g_ref, gi_ref, a_ref, h_ref, o_ref, *, nf):
    nb2, dt = h_ref.shape[1], h_ref.shape[2]
    nb = nb2 // 2
    live = pl.program_id(0) < nf

    @pl.when(live)
    def _():
        x = jnp.dot(g_ref[0].astype(BF16), a_ref[...].reshape(nb2, dt), preferred_element_type=F32)
        h = h_ref[0].astype(F32)
        xr, xi, hr, hi = x[:nb], x[nb:], h[:nb], h[nb:]
        y = jnp.concatenate([xr * hr - xi * hi, xr * hi + xi * hr], axis=0).astype(BF16)
        b = jnp.dot(gi_ref[0].astype(BF16), y, preferred_element_type=F32)
        o_ref[...] = b.reshape(o_ref.shape).astype(o_ref.dtype)

    @pl.when(jnp.logical_not(live))
    def _():
        o_ref[...] = jnp.zeros_like(o_ref)


def spectrum_product(a, h, g, ginv, dt=2048):
    _, nfp, nb, d = a.shape
    nf = g.shape[0]
    slab = lambda f: jnp.minimum(f, nf - 1)
    gspec = pl.BlockSpec((1, 2 * nb, 2 * nb), lambda f, j: (slab(f), 0, 0))
    aspec = pl.BlockSpec((2, 1, nb, dt), lambda f, j: (0, f, 0, j))
    return pl.pallas_call(
        functools.partial(_xspec_kernel, nf=nf),
        out_shape=jax.ShapeDtypeStruct((2, nfp, nb, d), BF16),
        grid=(nfp, d // dt),
        in_specs=[gspec, gspec, aspec, pl.BlockSpec((1, 2 * nb, dt), lambda f, j: (slab(f), 0, j))],
        out_specs=aspec,
        compiler_params=_cparams(("parallel", "parallel")),
        name="spectrum_product",
    )(g, ginv, a, h)


def _idft2_kernel(f_ref, b_ref, vg_ref, x0_ref, skip_ref, o_ref):
    conv = jnp.dot(f_ref[...].astype(BF16), b_ref[...], preferred_element_type=F32)
    vg = vg_ref[...].astype(F32)
    o_ref[...] = ((conv + vg * skip_ref[...]) * x0_ref[...].astype(F32)).astype(o_ref.dtype)


def idft_stage2_mix(b2d, finv, vg2d, x02d, skip_t, tn=4096):
    k, cols = b2d.shape
    rows = finv.shape[0]
    return pl.pallas_call(
        _idft2_kernel,
        out_shape=jax.ShapeDtypeStruct((rows, cols), BF16),
        grid=(cols // tn,),
        in_specs=[pl.BlockSpec((rows, k), lambda c: (0, 0)),
                  pl.BlockSpec((k, tn), lambda c: (0, c)),
                  pl.BlockSpec((rows, tn), lambda c: (0, c)),
                  pl.BlockSpec((rows, tn), lambda c: (0, c)),
                  pl.BlockSpec((1, tn), lambda c: (0, 0))],
        out_specs=pl.BlockSpec((rows, tn), lambda c: (0, c)),
        compiler_params=_cparams(("parallel",)),
        name="idft_stage2_mix",
    )(finv, b2d, vg2d, x02d, skip_t)


def _outproj_slab_kernel(a_ref, x_ref, gate_ref, w_ref, o_ref, wb_ref):
    @pl.when(pl.program_id(1) == 0)
    def _():
        wb_ref[...] = w_ref[...].astype(BF16)

    s1, grp = x_ref.shape[0], x_ref.shape[1]
    k = w_ref.shape[0]
    a = jnp.concatenate([a_ref[:, j * k:(j + 1) * k] for j in range(grp)], axis=0)
    y = jnp.dot(a, wb_ref[...], preferred_element_type=F32)
    for j in range(grp):
        o_ref[:, j, :] = x_ref[:, j, :] + gate_ref[...] * y[j * s1:(j + 1) * s1]


def outproj_residual_slab(a2d, x, gate, w, tn=512):
    t, d = x.shape
    k = w.shape[0]
    s1 = t // DFT_NB
    grp = HY_GROUP
    xspec = pl.BlockSpec((s1, grp, tn), lambda j, i: (0, i, j))
    out = pl.pallas_call(
        _outproj_slab_kernel,
        out_shape=jax.ShapeDtypeStruct((s1, DFT_NB, d), F32),
        grid=(d // tn, DFT_NB // grp),
        in_specs=[pl.BlockSpec((s1, grp * k), lambda j, i: (0, i)), xspec,
                  pl.BlockSpec((1, tn), lambda j, i: (0, j)),
                  pl.BlockSpec((k, tn), lambda j, i: (0, j))],
        out_specs=xspec,
        scratch_shapes=[pltpu.VMEM((k, tn), BF16)],
        compiler_params=_cparams(("arbitrary", "arbitrary")),
        name="outproj_residual_slab",
    )(a2d, x.reshape(s1, DFT_NB, d), gate, w)
    return out.reshape(t, d)


def hyena_mixer(hm, x, gate, w_in, layer, conv_w, conv_b, f_w0, f_b0, f_w1, f_b1, f_w2, f_b2, f_freq,
                f_wout, skip, w_out):
    seq, d = hm.shape
    nb = DFT_NB
    fs, g, ginv, finv = _dft_tables(seq)
    _, nfp = _dft_slabs(seq)
    p = matmul_cols(hm, w_in, layer, 3 * d)
    vg, x0 = hy_gate(p, conv_w, conv_b)
    kf = hy_filter(seq, f_w0, f_b0, f_w1, f_b1, f_w2, f_b2, f_freq, f_wout)
    ak = dft_stage1(kf, fs)
    h = filter_spectrum(ak.reshape(2, nfp, nb, 2 * d), g, d, 2 * seq)
    a = dft_stage1(vg, fs)
    b = spectrum_product(a.reshape(2, nfp, nb, d), h, g, ginv)
    tn = 2 * d
    skip_t = jnp.tile(skip.reshape(1, d), (1, tn // d))
    y = idft_stage2_mix(b.reshape(2 * nfp, nb * d), finv, vg, x0, skip_t, tn=tn)
    return outproj_residual_slab(y, x, gate, w_out)


def _qk_prep_kernel(m_ref, p_ref, n_ref, w_ref, b_ref, s_ref, o_ref):
    i = pl.program_id(0)
    nt = pl.num_programs(0)
    f = lambda r: r[...].astype(F32)
    u = _conv3(f(m_ref), f(p_ref), f(n_ref), w_ref[...], b_ref[...], i, nt)
    o_ref[...] = (u * jax.nn.sigmoid(u) * s_ref[...]).astype(o_ref.dtype)


def qk_prep(p, conv_w, conv_b, qk_dim, dqk, tm=512, tc=512):
    t = p.shape[0]
    c = 2 * qk_dim
    scale = np.ones((1, c), np.float32)
    scale[:, qk_dim:] = 1.0 / math.sqrt(dqk)
    return pl.pallas_call(
        _qk_prep_kernel,
        out_shape=jax.ShapeDtypeStruct((t, c), BF16),
        grid=(t // tm, c // tc),
        in_specs=_halo_specs(tm, tc, t, lambda j: j) + [
            pl.BlockSpec((SHORT_CONV, tc), lambda i, j: (0, j)),
            pl.BlockSpec((1, tc), lambda i, j: (0, j)),
            pl.BlockSpec((1, tc), lambda i, j: (0, j))],
        out_specs=pl.BlockSpec((tm, tc), lambda i, j: (i, j)),
        compiler_params=_cparams(("parallel", "parallel")),
        name="qk_prep",
    )(p, p, p, conv_w, conv_b.reshape(1, c), jnp.asarray(scale))


def _log_sigmoid(x):
    return jnp.minimum(x, 0.0) - jnp.log1p(jnp.exp(-jnp.abs(x)))


def _gates_kernel(hm_ref, w_ref, b_ref, col_ref, row_ref, scal_ref):
    hi = lax.Precision.HIGHEST
    nh = ML_HEADS
    hm = hm_ref[...]
    cs = hm.shape[0]
    gt = jnp.dot(hm, w_ref[...].astype(BF16), preferred_element_type=F32) + b_ref[...]
    gtt = gt.T
    r = lax.broadcasted_iota(jnp.int32, (cs, cs), 0)
    c = lax.broadcasted_iota(jnp.int32, (cs, cs), 1)
    lower = (r >= c).astype(F32)
    upper = (r <= c).astype(F32)
    i_f, f_f, i_b, f_b = (gt[:, k * nh:(k + 1) * nh] for k in range(4))
    lf_f, lf_b = _log_sigmoid(f_f), _log_sigmoid(f_b)
    b_f = jnp.dot(lower, lf_f, precision=hi, preferred_element_type=F32)
    b_b = jnp.dot(upper, lf_b, precision=hi, preferred_element_type=F32)
    g_f = jnp.sum(lf_f, axis=0, keepdims=True)
    g_b = jnp.sum(lf_b, axis=0, keepdims=True)
    a_f = g_f - b_f + i_f
    a_b = g_b - b_b + i_b
    col_ref[...] = jnp.concatenate([b_f, b_b, a_f, a_b], axis=1)
    scal_ref[0] = jnp.concatenate([g_f, g_b, jnp.max(a_f, axis=0, keepdims=True),
                                   jnp.max(a_b, axis=0, keepdims=True)], axis=1)
    i_ft, f_ft, i_bt, f_bt = (gtt[k * nh:(k + 1) * nh, :] for k in range(4))
    b_ft = jnp.dot(_log_sigmoid(f_ft), upper, precision=hi, preferred_element_type=F32)
    b_bt = jnp.dot(_log_sigmoid(f_bt), lower, precision=hi, preferred_element_type=F32)
    row_ref[...] = jnp.concatenate([b_ft, b_bt, i_ft, i_bt], axis=0)


def ml_gates(hm, w_g, gate_b, cs):
    t, d = hm.shape
    g4 = w_g.shape[1]
    nc = t // cs
    lanes = 128
    w_pad = jnp.pad(w_g, ((0, 0), (0, lanes - g4)))
    b_pad = jnp.pad(gate_b.reshape(1, g4), ((0, 0), (0, lanes - g4)))
    return pl.pallas_call(
        _gates_kernel,
        out_shape=(jax.ShapeDtypeStruct((t, g4), F32), jax.ShapeDtypeStruct((g4, t), F32),
                   jax.ShapeDtypeStruct((nc, 1, g4), F32)),
        grid=(nc,),
        in_specs=[pl.BlockSpec((cs, d), lambda c: (c, 0)),
                  pl.BlockSpec((d, lanes), lambda c: (0, 0)),
                  pl.BlockSpec((1, lanes), lambda c: (0, 0))],
        out_specs=(pl.BlockSpec((cs, g4), lambda c: (c, 0)),
                   pl.BlockSpec((g4, cs), lambda c: (0, c)),
                   pl.BlockSpec((1, 1, g4), lambda c: (c, 0, 0))),
        compiler_params=_cparams(("parallel",)),
        name="ml_gates",
    )(hm, w_pad, b_pad)


def _mlstm_direction(qk_ref, v_ref, col_ref, row_ref, sc_ref, o_ref, c_ref, n_ref, m_ref, base, causal,
                     dqk, dv):
    nh = ML_HEADS
    cs = qk_ref.shape[0]
    qkd = nh * dqk
    per_head = lambda f: jnp.stack([f(h) for h in range(nh)])
    q = per_head(lambda h: qk_ref[:, h * dqk:(h + 1) * dqk])
    k = per_head(lambda h: qk_ref[:, qkd + h * dqk:qkd + (h + 1) * dqk])
    v = per_head(lambda h: v_ref[:, h * dv:(h + 1) * dv])
    b_col = per_head(lambda h: col_ref[:, base + h:base + h + 1])
    a_col = per_head(lambda h: col_ref[:, 2 * nh + base + h:2 * nh + base + h + 1])
    b_row = per_head(lambda h: row_ref[base + h:base + h + 1, :])
    li_row = per_head(lambda h: row_ref[2 * nh + base + h:2 * nh + base + h + 1, :])
    g = per_head(lambda h: sc_ref[0, :, base + h:base + h + 1])
    m_loc = per_head(lambda h: sc_ref[0, :, 2 * nh + base + h:2 * nh + base + h + 1])
    c_st = c_ref[base:base + nh]
    n_st = n_ref[base:base + nh]
    m_st = m_ref[base:base + nh][:, :, 0:1]
    r = lax.broadcasted_iota(jnp.int32, (1, cs, cs), 1)
    s = lax.broadcasted_iota(jnp.int32, (1, cs, cs), 2)
    mask = (s <= r) if causal else (s >= r)
    bdot = lambda a, b, ca, cb: lax.dot_general(a, b, (((ca,), (cb,)), ((0,), (0,))),
                                                preferred_element_type=F32)
    dlog = jnp.where(mask, b_col - b_row + li_row, -jnp.inf)
    m_inter = b_col + m_st
    m_t = jnp.maximum(m_inter, jnp.max(dlog, axis=-1, keepdims=True))
    p = jnp.exp(dlog - m_t) * bdot(q, k, 2, 2)
    s_inter = jnp.exp(m_inter - m_t)
    num = s_inter * bdot(q, c_st.astype(BF16), 2, 1) + bdot(p.astype(BF16), v, 2, 1)
    den = (s_inter * jnp.sum(q.astype(F32) * n_st, axis=-1, keepdims=True)
           + jnp.sum(p, axis=-1, keepdims=True))
    hout = num / jnp.maximum(jnp.abs(den), jnp.exp(-m_t))
    for h in range(nh):
        o_ref[:, h * dv:(h + 1) * dv] = hout[h]
    kw = k.astype(F32) * jnp.exp(a_col - m_loc)
    kwb = kw.astype(BF16)
    c_loc = per_head(lambda h: lax.dot_general(kwb[h], v[h], (((0,), (0,)), ((), ())),
                                               preferred_element_type=F32))
    n_loc = jnp.sum(kw, axis=1, keepdims=True)
    m_new = jnp.maximum(g + m_st, m_loc)
    s_prev = jnp.exp(g + m_st - m_new)
    s_loc = jnp.exp(m_loc - m_new)
    c_ref[base:base + nh] = s_prev * c_st + s_loc * c_loc
    n_ref[base:base + nh] = s_prev * n_st + s_loc * n_loc
    m_ref[base:base + nh] = jnp.broadcast_to(m_new, (nh,) + m_ref.shape[1:])


def _mlstm_kernel(qkf_ref, qkb_ref, vf_ref, vb_ref, colf_ref, colb_ref, rowf_ref, rowb_ref,
                  scf_ref, scb_ref, of_ref, ob_ref, c_ref, n_ref, m_ref, *, dqk, dv):
    @pl.when(pl.program_id(0) == 0)
    def _():
        c_ref[...] = jnp.zeros_like(c_ref)
        n_ref[...] = jnp.zeros_like(n_ref)
        m_ref[...] = jnp.zeros_like(m_ref)

    _mlstm_direction(qkf_ref, vf_ref, colf_ref, rowf_ref, scf_ref, of_ref, c_ref, n_ref, m_ref,
                     0, True, dqk, dv)
    _mlstm_direction(qkb_ref, vb_ref, colb_ref, rowb_ref, scb_ref, ob_ref, c_ref, n_ref, m_ref,
                     ML_HEADS, False, dqk, dv)


def mlstm_bidir(qk, p, col, row, scal, cs, dqk, dv):
    t = qk.shape[0]
    nh = ML_HEADS
    nc = t // cs
    qkd2 = 2 * nh * dqk
    vd = nh * dv
    vblk = qkd2 // vd
    g4 = col.shape[1]
    fwd = lambda c: c
    bwd = lambda c: nc - 1 - c
    mk = lambda fn: dict(
        qk=pl.BlockSpec((cs, qkd2), lambda c: (fn(c), 0)),
        v=pl.BlockSpec((cs, vd), lambda c: (fn(c), vblk)),
        col=pl.BlockSpec((cs, g4), lambda c: (fn(c), 0)),
        row=pl.BlockSpec((g4, cs), lambda c: (0, fn(c))),
        sc=pl.BlockSpec((1, 1, g4), lambda c: (fn(c), 0, 0)),
        o=pl.BlockSpec((cs, vd), lambda c: (fn(c), 0)))
    sf, sb = mk(fwd), mk(bwd)
    return pl.pallas_call(
        functools.partial(_mlstm_kernel, dqk=dqk, dv=dv),
        out_shape=(jax.ShapeDtypeStruct((t, vd), F32), jax.ShapeDtypeStruct((t, vd), F32)),
        grid=(nc,),
        in_specs=[sf["qk"], sb["qk"], sf["v"], sb["v"], sf["col"], sb["col"], sf["row"], sb["row"],
                  sf["sc"], sb["sc"]],
        out_specs=(sf["o"], sb["o"]),
        scratch_shapes=[pltpu.VMEM((2 * nh, dqk, dv), F32), pltpu.VMEM((2 * nh, 1, dqk), F32),
                        pltpu.VMEM((2 * nh, 1, 128), F32)],
        compiler_params=_cparams(("arbitrary",)),
        name="mlstm_bidir",
    )(qk, qk, p, p, col, col, row, row, scal, scal)


def _ml_post_kernel(hf_ref, hb_ref, o_ref, hg_ref, out_ref, *, dv):
    nh = ML_HEADS
    for h in range(nh):
        sl = slice(h * dv, (h + 1) * dv)
        hs = hf_ref[:, sl] + hb_ref[:, sl]
        hs = hs * lax.rsqrt(jnp.mean(hs * hs, axis=-1, keepdims=True) + EPS)
        og = jax.nn.sigmoid(o_ref[:, sl].astype(F32))
        out_ref[:, sl] = (hs * hg_ref[:, sl] * og).astype(out_ref.dtype)


def ml_post(hf, hb, p, head_g, dv, tm=512):
    t, vd = hf.shape
    oblk = p.shape[1] // vd - 1
    spec = pl.BlockSpec((tm, vd), lambda i: (i, 0))
    return pl.pallas_call(
        functools.partial(_ml_post_kernel, dv=dv),
        out_shape=jax.ShapeDtypeStruct((t, vd), BF16),
        grid=(t // tm,),
        in_specs=[spec, spec, pl.BlockSpec((tm, vd), lambda i: (i, oblk)),
                  pl.BlockSpec((1, vd), lambda i: (0, 0))],
        out_specs=spec,
        compiler_params=_cparams(("parallel",)),
        name="ml_post",
    )(hf, hb, p, head_g.reshape(1, vd))


def mlstm_mixer(hm, x, gate, w_in, layer, conv_w, conv_b, gate_b, head_g, w_out):
    seq, d = hm.shape
    nh = ML_HEADS
    qk_dim = d // 2
    dqk = qk_dim // nh
    dv = d // nh
    n_main = 2 * qk_dim + 2 * d
    cs = min(ML_CHUNK, seq)
    p = matmul_cols(hm, w_in, layer, n_main)
    col, row, scal = ml_gates(hm, w_in[layer, :, n_main:], gate_b, cs)
    qk = qk_prep(p, conv_w, conv_b, qk_dim, dqk)
    hf, hb = mlstm_bidir(qk, p, col, row, scal, cs, dqk, dv)
    a = ml_post(hf, hb, p, head_g, dv)
    return outproj_residual(a, x, gate, w_out)


def _first_argmax(vals, axis, n):
    m = jnp.max(vals, axis=axis, keepdims=True)
    iota = lax.broadcasted_iota(jnp.int32, vals.shape, axis)
    idx = jnp.min(jnp.where(vals == m, iota, n), axis=axis, keepdims=True)
    return m, idx, iota


def _router_kernel(x_ref, g_ref, sc_ref, sh_ref, rwt_ref, rb_ref,
                   hf_ref, eidx_ref, rank_ref, wk_ref, cnt_ref, carry_ref):
    ne, ng = N_EXPERTS, N_GROUPS
    per = ne // ng
    tm = x_ref.shape[0]

    @pl.when(pl.program_id(0) == 0)
    def _():
        carry_ref[...] = jnp.zeros_like(carry_ref)

    hf = _normmod(x_ref[...], g_ref[...], sc_ref[...], sh_ref[...])
    hf_ref[...] = hf
    logits = lax.dot_general(rwt_ref[...], hf, (((1,), (1,)), ((), ())),
                             precision=lax.Precision.HIGHEST, preferred_element_type=F32)
    scores = jax.nn.sigmoid(logits)
    sel = scores + rb_ref[...]
    sel3 = sel.reshape(ng, per, tm)
    m1, i1, io3 = _first_argmax(sel3, 1, per)
    m2 = jnp.max(jnp.where(io3 == i1, -jnp.inf, sel3), axis=1, keepdims=True)
    gs = (m1 + m2).reshape(ng, tm)
    gsel = jnp.zeros((ng, tm), F32)
    for _ in range(TOPK_GROUPS):
        _, gi, iog = _first_argmax(gs, 0, ng)
        hit = iog == gi
        gsel = jnp.where(hit, 1.0, gsel)
        gs = jnp.where(hit, -jnp.inf, gs)
    gmask = jnp.broadcast_to(gsel.reshape(ng, 1, tm), (ng, per, tm)).reshape(ne, tm)
    cand = jnp.where(gmask > 0.5, sel, -jnp.inf)
    picked = []
    chosen = jnp.zeros((ne, tm), F32)
    for _ in range(TOP_K):
        _, ei, ioe = _first_argmax(cand, 0, ne)
        hit = ioe == ei
        picked.append((ei, hit))
        chosen = jnp.where(hit, 1.0, chosen)
        cand = jnp.where(hit, -jnp.inf, cand)
    r = lax.broadcasted_iota(jnp.int32, (tm, tm), 0)
    c = lax.broadcasted_iota(jnp.int32, (tm, tm), 1)
    before = (r < c).astype(BF16)
    ranks = jnp.dot(chosen.astype(BF16), before, preferred_element_type=F32) + carry_ref[:, 0:1]
    carry_ref[...] = carry_ref[...] + jnp.sum(chosen, axis=1, keepdims=True)
    cnt_ref[...] = carry_ref[...]
    wks = [jnp.sum(jnp.where(hit, scores, 0.0), axis=0, keepdims=True) for _, hit in picked]
    wsum = functools.reduce(lambda a, b: a + b, wks)
    eidx_ref[...] = jnp.zeros_like(eidx_ref)
    rank_ref[...] = jnp.zeros_like(rank_ref)
    wk_ref[...] = jnp.zeros_like(wk_ref)
    for j, ((ei, hit), wk) in enumerate(zip(picked, wks)):
        eidx_ref[j:j + 1, :] = ei
        rank_ref[j:j + 1, :] = jnp.sum(jnp.where(hit, ranks, 0.0), axis=0, keepdims=True).astype(jnp.int32)
        wk_ref[j:j + 1, :] = wk / wsum * ROUTED_SCALE


def moe_router(x, g, sc, sh, router_w, router_bias, tm=512):
    t, d = x.shape
    ne = N_EXPERTS
    vec = pl.BlockSpec((1, d), lambda i: (0, 0))
    lane = pl.BlockSpec((8, tm), lambda i: (0, i))
    return pl.pallas_call(
        _router_kernel,
        out_shape=(jax.ShapeDtypeStruct((t, d), F32), jax.ShapeDtypeStruct((8, t), jnp.int32),
                   jax.ShapeDtypeStruct((8, t), jnp.int32), jax.ShapeDtypeStruct((8, t), F32),
                   jax.ShapeDtypeStruct((ne, 128), F32)),
        grid=(t // tm,),
        in_specs=[pl.BlockSpec((tm, d), lambda i: (i, 0)), vec, vec, vec,
                  pl.BlockSpec((ne, d), lambda i: (0, 0)), pl.BlockSpec((ne, 1), lambda i: (0, 0))],
        out_specs=(pl.BlockSpec((tm, d), lambda i: (i, 0)), lane, lane, lane,
                   pl.BlockSpec((ne, 128), lambda i: (0, 0))),
        scratch_shapes=[pltpu.VMEM((ne, 128), F32)],
        compiler_params=_cparams(("arbitrary",)),
        name="moe_router",
    )(x, g, sc, sh, router_w.T, router_bias.reshape(ne, 1))


def _plan_kernel(eidx_ref, rank_ref, ps_ref, dest_ref):
    ne = N_EXPERTS
    tm = eidx_ref.shape[1]
    io = lax.broadcasted_iota(jnp.int32, (ne, tm), 0)
    ps = ps_ref[...]
    dest_ref[...] = jnp.zeros_like(dest_ref)
    for j in range(TOP_K):
        hit = io == eidx_ref[j:j + 1, :]
        base = jnp.sum(jnp.where(hit, ps, 0.0), axis=0, keepdims=True)
        dest_ref[j:j + 1, :] = base.astype(jnp.int32) + rank_ref[j:j + 1, :]


def moe_plan(eidx, rank, pad_start, tm=2048):
    t = eidx.shape[1]
    tm = min(tm, t)
    lane = pl.BlockSpec((8, tm), lambda i: (0, i))
    return pl.pallas_call(
        _plan_kernel,
        out_shape=jax.ShapeDtypeStruct((8, t), jnp.int32),
        grid=(t // tm,),
        in_specs=[lane, lane, pl.BlockSpec((N_EXPERTS, 1), lambda i: (0, 0))],
        out_specs=lane,
        compiler_params=_cparams(("parallel",)),
        name="moe_plan",
    )(eidx, rank, pad_start.astype(F32).reshape(N_EXPERTS, 1))


SUBLANES = 8
_PAD_PIECES = tuple(1 << k for k in reversed(range(3, MOE_BLOCK.bit_length() - 1)))


def _dispatch_kernel(dest_ref, zs_ref, zn_ref, hf_ref, xb_ref, zbuf, sem, zsem, *, t_total):
    tm = hf_ref.shape[0]
    base = pl.program_id(0) * tm

    def row_copy(t, j):
        d = dest_ref[j * t_total + base + t]
        return pltpu.make_async_copy(hf_ref.at[pl.ds(t, 1)], xb_ref.at[pl.ds(d, 1)], sem)

    def start(t, carry):
        for j in range(TOP_K):
            row_copy(t, j).start()
        return carry

    def wait_all():
        for _ in range(TOP_K):
            pltpu.make_async_copy(hf_ref, xb_ref.at[pl.ds(0, tm)], sem).wait()

    def fill(do_start):
        def body(e, carry):
            zs, zn = zs_ref[e], zn_ref[e]
            end = zs + zn

            def piece(src, dst):
                cp = pltpu.make_async_copy(src, dst, zsem)
                cp.start() if do_start else cp.wait()

            for r in range(SUBLANES - 1):
                @pl.when(r < (zn & (SUBLANES - 1)))
                def _():
                    piece(zbuf.at[pl.ds(0, 1)], xb_ref.at[pl.ds(zs + r, 1)])
            for p in _PAD_PIECES:
                @pl.when((zn & p) != 0)
                def _():
                    q = pl.multiple_of(end - (zn & ~(p - 1)), SUBLANES)
                    piece(zbuf.at[pl.ds(0, p)], xb_ref.at[pl.ds(q, p)])
            return carry
        lax.fori_loop(0, N_EXPERTS, body, 0)

    first = pl.program_id(0) == 0

    @pl.when(first)
    def _():
        zbuf[...] = jnp.zeros_like(zbuf)
        fill(True)

    lax.fori_loop(0, tm, start, 0)

    @pl.when(first)
    def _():
        fill(False)

    wait_all()


def moe_dispatch(dest, zero_start, zero_len, hf, n_slots, tm=256):
    t, d = hf.shape
    return pl.pallas_call(
        functools.partial(_dispatch_kernel, t_total=t),
        out_shape=jax.ShapeDtypeStruct((n_slots, d), hf.dtype),
        grid_spec=pltpu.PrefetchScalarGridSpec(
            num_scalar_prefetch=3,
            grid=(t // tm,),
            in_specs=[pl.BlockSpec((tm, d), lambda i, *_: (i, 0))],
            out_specs=pl.BlockSpec(memory_space=pl.ANY),
            scratch_shapes=[pltpu.VMEM((MOE_BLOCK // 2, d), hf.dtype),
                            pltpu.SemaphoreType.DMA(()), pltpu.SemaphoreType.DMA(())]),
        compiler_params=_cparams(("arbitrary",)),
        name="moe_dispatch",
    )(dest, zero_start, zero_len, hf)


def _ffn(x, wg, wu, wd):
    hg = jnp.dot(x, wg, preferred_element_type=F32)
    hu = jnp.dot(x, wu, preferred_element_type=F32)
    h = (hg * jax.nn.sigmoid(hg) * hu).astype(BF16)
    return jnp.dot(h, wd, preferred_element_type=F32)


def _expert_kernel(be_ref, na_ref, first_ref, next_ref, slot_ref, x_ref, wg_hbm, wu_hbm, wd_hbm, o_ref,
                   wgf, wuf, wdf, wgb, wub, wdb, sems, *, layer):
    b = pl.program_id(0)
    active = b < na_ref[0]

    def fetch(e, s):
        return (pltpu.make_async_copy(wg_hbm.at[layer, e], wgf.at[s], sems.at[s, 0]),
                pltpu.make_async_copy(wu_hbm.at[layer, e], wuf.at[s], sems.at[s, 1]),
                pltpu.make_async_copy(wd_hbm.at[layer, e], wdf.at[s], sems.at[s, 2]))

    @pl.when(jnp.logical_and(active, first_ref[b] == 1))
    def _():
        s = slot_ref[b]

        @pl.when(b == 0)
        def _():
            for cp in fetch(be_ref[b], s):
                cp.start()

        for cp in fetch(be_ref[b], s):
            cp.wait()
        nxt = next_ref[b]

        @pl.when(nxt >= 0)
        def _():
            for cp in fetch(nxt, 1 - s):
                cp.start()

        wgb[...] = wgf[s].astype(BF16)
        wub[...] = wuf[s].astype(BF16)
        wdb[...] = wdf[s].astype(BF16)

    @pl.when(active)
    def _():
        o_ref[...] = _ffn(x_ref[...].astype(BF16), wgb[...], wub[...], wdb[...])


def moe_experts(blk_e, n_active, first, nxt, slot, xb, w_gate, w_up, w_down, layer):
    p, d = xb.shape
    ff = w_gate.shape[3]
    nblk = p // MOE_BLOCK
    row = lambda b, be, na, *_: (jnp.minimum(b, na[0] - 1), 0)
    hbm = pl.BlockSpec(memory_space=pl.ANY)
    return pl.pallas_call(
        functools.partial(_expert_kernel, layer=layer),
        out_shape=jax.ShapeDtypeStruct((p, d), F32),
        grid_spec=pltpu.PrefetchScalarGridSpec(
            num_scalar_prefetch=5,
            grid=(nblk,),
            in_specs=[pl.BlockSpec((MOE_BLOCK, d), row), hbm, hbm, hbm],
            out_specs=pl.BlockSpec((MOE_BLOCK, d), row),
            scratch_shapes=[pltpu.VMEM((2, d, ff), F32), pltpu.VMEM((2, d, ff), F32),
                            pltpu.VMEM((2, ff, d), F32),
                            pltpu.VMEM((d, ff), BF16), pltpu.VMEM((d, ff), BF16),
                            pltpu.VMEM((ff, d), BF16), pltpu.SemaphoreType.DMA((2, 3))]),
        compiler_params=_cparams(("arbitrary",)),
        name="moe_experts",
    )(blk_e, n_active, first, nxt, slot, xb, w_gate, w_up, w_down)


def _combine_kernel(dest_ref, yb_ref, x_ref, hf_ref, wk_ref, gate_ref, wg_ref, wu_ref, wd_ref, fg_ref, o_ref,
                    buf, wgb, wub, wdb, sem, *, t_total, final_norm):
    tm, d = x_ref.shape
    base = pl.program_id(0) * tm

    @pl.when(pl.program_id(0) == 0)
    def _():
        wgb[...] = wg_ref[...].astype(BF16)
        wub[...] = wu_ref[...].astype(BF16)
        wdb[...] = wd_ref[...].astype(BF16)

    def row_copy(t, j):
        dst = dest_ref[j * t_total + base + t]
        return pltpu.make_async_copy(yb_ref.at[pl.ds(dst, 1)], buf.at[j, pl.ds(t, 1)], sem)

    def start(t, carry):
        for j in range(TOP_K):
            row_copy(t, j).start()
        return carry

    lax.fori_loop(0, tm, start, 0)
    acc = _ffn(hf_ref[...].astype(BF16), wgb[...], wub[...], wdb[...])
    for j in range(TOP_K):
        pltpu.make_async_copy(yb_ref.at[pl.ds(0, tm)], buf.at[j], sem).wait()
    for j in range(TOP_K):
        acc = acc + wk_ref[:, j:j + 1] * buf[j]
    y = x_ref[...] + gate_ref[...] * acc
    if final_norm:
        y = (y * lax.rsqrt(jnp.mean(y * y, axis=-1, keepdims=True) + EPS)) * fg_ref[...]
    o_ref[...] = y


def moe_combine(dest, yb, x, hf, wk_t, gate, sh_gate, sh_up, sh_down, final_g, final_norm, tm=128):
    t, d = x.shape
    ff = sh_gate.shape[1]
    tm = min(tm, t)
    tile = lambda i, dest: (i, 0)
    full = lambda shape: pl.BlockSpec(shape, lambda i, dest: (0, 0))
    return pl.pallas_call(
        functools.partial(_combine_kernel, t_total=t, final_norm=final_norm),
        out_shape=jax.ShapeDtypeStruct((t, d), F32),
        grid_spec=pltpu.PrefetchScalarGridSpec(
            num_scalar_prefetch=1,
            grid=(t // tm,),
            in_specs=[pl.BlockSpec(memory_space=pl.ANY),
                      pl.BlockSpec((tm, d), tile), pl.BlockSpec((tm, d), tile),
                      pl.BlockSpec((tm, 8), tile), full((1, d)),
                      full((d, ff)), full((d, ff)), full((ff, d)), full((1, d))],
            out_specs=pl.BlockSpec((tm, d), tile),
            scratch_shapes=[pltpu.VMEM((TOP_K, tm, d), F32),
                            pltpu.VMEM((d, ff), BF16), pltpu.VMEM((d, ff), BF16), pltpu.VMEM((ff, d), BF16),
                            pltpu.SemaphoreType.DMA(())]),
        compiler_params=_cparams(("arbitrary",)),
        name="moe_combine",
    )(dest, yb, x, hf, wk_t, gate, sh_gate, sh_up, sh_down, final_g)


def moe_layer(x, g, sc, sh, gate, router_w, router_bias, w_gate, w_up, w_down, sh_gate, sh_up, sh_down,
              layer, final_g, final_norm):
    t, d = x.shape
    ne = N_EXPERTS
    hf, eidx, rank, wk, cnt = moe_router(x, g, sc, sh, router_w, router_bias)
    counts = cnt[:, 0].astype(jnp.int32)
    padded = (counts + MOE_BLOCK - 1) // MOE_BLOCK * MOE_BLOCK
    pad_end = jnp.cumsum(padded)
    pad_start = pad_end - padded
    n_slots = (t * TOP_K + ne * (MOE_BLOCK - 1) + MOE_BLOCK - 1) // MOE_BLOCK * MOE_BLOCK
    nblk = n_slots // MOE_BLOCK
    dest = moe_plan(eidx, rank, pad_start)[:TOP_K].reshape(TOP_K * t)
    blk_start = jnp.arange(nblk, dtype=jnp.int32) * MOE_BLOCK
    n_active = (pad_end[-1] // MOE_BLOCK).astype(jnp.int32).reshape(1)
    blk_e = jnp.minimum(jnp.sum(blk_start[:, None] >= pad_end[None, :], axis=1), ne - 1).astype(jnp.int32)
    last_e = jnp.max(jnp.where(counts > 0, jnp.arange(ne, dtype=jnp.int32), 0))
    live = blk_start < pad_end[-1]
    blk_e = jnp.where(live, blk_e, last_e)
    first = jnp.logical_and(live, jnp.concatenate([jnp.ones((1,), bool), blk_e[1:] != blk_e[:-1]]))
    slot = ((jnp.cumsum(first.astype(jnp.int32)) - 1) % 2).astype(jnp.int32)
    ids = jnp.arange(ne, dtype=jnp.int32)
    later = jnp.logical_and(ids[None, :] > ids[:, None], counts[None, :] > 0)
    next_e = jnp.min(jnp.where(later, ids[None, :], ne), axis=1)
    next_e = jnp.where(next_e < ne, next_e, -1)
    nxt = jnp.sum(jnp.where(blk_e[:, None] == ids[None, :], next_e[None, :], 0), axis=1).astype(jnp.int32)

    xb = moe_dispatch(dest, pad_start + counts, padded - counts, hf, n_slots)
    yb = moe_experts(blk_e, n_active, first.astype(jnp.int32), nxt, slot, xb, w_gate, w_up, w_down, layer)
    return moe_combine(dest, yb, x, hf, wk.T, gate, sh_gate, sh_up, sh_down, final_g, final_norm)


def kernel(x, c, ada_w, ada_b, norm_mix_g, norm_ffn_g, hy_w_in, hy_conv_w, hy_conv_b, hy_f_w0, hy_f_b0, hy_f_w1, hy_f_b1, hy_f_w2, hy_f_b2, hy_f_freq, hy_f_wout, hy_skip, hy_w_out, ml_w_in, ml_conv_w, ml_conv_b, ml_gate_b, ml_head_g, ml_w_out, moe_router_w, moe_router_bias, moe_w_gate, moe_w_up, moe_w_down, sh_w_gate, sh_w_up, sh_w_down, final_g):
    bsz, seq, d = x.shape
    assert bsz == 1, "kernels are written for a single sequence"
    depth = ada_w.shape[0]
    xs = x.reshape(seq, d)
    for layer in range(depth):
        ada = ada_proj(c, ada_w, ada_b, layer)
        sh_m, sc_m, g_m, sh_f, sc_f, g_f = (ada[:, k * d:(k + 1) * d] for k in range(6))
        gm = norm_mix_g[layer].reshape(1, d)
        j = layer // 2
        if layer % 2 == 0:
            hm = normmod_slab(xs, gm, sc_m, sh_m)
            xs = hyena_mixer(hm, xs, g_m, hy_w_in, j, hy_conv_w[j], hy_conv_b[j], hy_f_w0[j], hy_f_b0[j],
                             hy_f_w1[j], hy_f_b1[j], hy_f_w2[j], hy_f_b2[j], hy_f_freq[j], hy_f_wout[j],
                             hy_skip[j], hy_w_out[j])
        else:
            hm = normmod(xs, gm, sc_m, sh_m)
            xs = mlstm_mixer(hm, xs, g_m, ml_w_in, j, ml_conv_w[j], ml_conv_b[j], ml_gate_b[j],
                             ml_head_g[j], ml_w_out[j])
        xs = moe_layer(xs, norm_ffn_g[layer].reshape(1, d), sc_f, sh_f, g_f, moe_router_w[layer],
                       moe_router_bias[layer], moe_w_gate, moe_w_up, moe_w_down,
                       sh_w_gate[layer], sh_w_up[layer], sh_w_down[layer], layer,
                       final_g.reshape(1, d), layer == depth - 1)
    return xs.reshape(bsz, seq, d)
```

```python
import functools
import math

import numpy as np
import jax
import jax.numpy as jnp
from jax import lax
from jax.experimental import pallas as pl
from jax.experimental.pallas import tpu as pltpu

F32 = jnp.float32
BF16 = jnp.bfloat16
EPS = 1e-6

V7X_VMEM_BYTES = 64 * 1024 * 1024
VMEM_LIMIT = V7X_VMEM_BYTES - 8 * 1024 * 1024

SHORT_CONV = 3
HY_BANDS = 16
HY_DECAY_TARGET = 1e-2
HY_FAST_DECAY = 0.3
HY_SLOW_DECAY = 1.5
ML_HEADS = 8
ML_CHUNK = 256
N_EXPERTS = 64
TOP_K = 6
N_GROUPS = 8
TOPK_GROUPS = 4
ROUTED_SCALE = 2.5
MOE_BLOCK = 256
DFT_NB = 128


def _cparams(sem):
    return pltpu.CompilerParams(dimension_semantics=sem, vmem_limit_bytes=VMEM_LIMIT)


def _ada_kernel(c_ref, w_ref, b_ref, o_ref, *, tc):
    rows, n = w_ref.shape[1], w_ref.shape[2]

    @pl.when(pl.program_id(0) == 0)
    def _():
        o_ref[...] = b_ref[0]

    c = c_ref[...]
    cs = c * jax.nn.sigmoid(c)
    for j in range(n // tc):
        part = (w_ref[0, :, j * tc:(j + 1) * tc] * cs).reshape(rows // 8, 8, tc).sum(axis=0)
        o_ref[:, j * tc:(j + 1) * tc] += jnp.sum(part, axis=0, keepdims=True)


def ada_proj(c, ada_w, ada_b, layer, rows=128):
    depth, d, n = ada_w.shape
    return pl.pallas_call(
        functools.partial(_ada_kernel, tc=2048),
        out_shape=jax.ShapeDtypeStruct((1, n), F32),
        grid=(d // rows,),
        in_specs=[pl.BlockSpec((rows, 1), lambda r: (r, 0)),
                  pl.BlockSpec((1, rows, n), lambda r: (layer, r, 0)),
                  pl.BlockSpec((1, 1, n), lambda r: (layer, 0, 0))],
        out_specs=pl.BlockSpec((1, n), lambda r: (0, 0)),
        compiler_params=_cparams(("arbitrary",)),
        name="ada_proj",
    )(c.reshape(d, 1), ada_w, ada_b.reshape(depth, 1, n))


def _normmod(x, g, sc, sh):
    r = lax.rsqrt(jnp.mean(x * x, axis=-1, keepdims=True) + EPS)
    return (x * r) * g * (1.0 + sc) + sh


def _normmod_kernel(x_ref, g_ref, sc_ref, sh_ref, o_ref):
    o_ref[...] = _normmod(x_ref[...], g_ref[...], sc_ref[...], sh_ref[...]).astype(o_ref.dtype)


def normmod(x, g, sc, sh, out_dtype=BF16, tm=512):
    t, d = x.shape
    vec = pl.BlockSpec((1, d), lambda i: (0, 0))
    return pl.pallas_call(
        _normmod_kernel,
        out_shape=jax.ShapeDtypeStruct((t, d), out_dtype),
        grid=(t // tm,),
        in_specs=[pl.BlockSpec((tm, d), lambda i: (i, 0)), vec, vec, vec],
        out_specs=pl.BlockSpec((tm, d), lambda i: (i, 0)),
        compiler_params=_cparams(("parallel",)),
        name="normmod",
    )(x, g, sc, sh)


def _mm_kernel(a_ref, w_ref, o_ref, wb_ref):
    @pl.when(pl.program_id(1) == 0)
    def _():
        wb_ref[...] = w_ref[0].astype(BF16)

    o_ref[...] = jnp.dot(a_ref[...], wb_ref[...], preferred_element_type=F32).astype(o_ref.dtype)


def matmul_cols(a, w, layer, n_out, out_dtype=BF16, tm=1024, tn=512):
    m, k = a.shape
    tm = min(tm, m)
    return pl.pallas_call(
        _mm_kernel,
        out_shape=jax.ShapeDtypeStruct((m, n_out), out_dtype),
        grid=(n_out // tn, m // tm),
        in_specs=[pl.BlockSpec((tm, k), lambda j, i: (i, 0)),
                  pl.BlockSpec((1, k, tn), lambda j, i: (layer, 0, j))],
        out_specs=pl.BlockSpec((tm, tn), lambda j, i: (i, j)),
        scratch_shapes=[pltpu.VMEM((k, tn), BF16)],
        compiler_params=_cparams(("arbitrary", "arbitrary")),
        name="matmul_cols",
    )(a, w)


def _outproj_kernel(a_ref, x_ref, gate_ref, w_ref, o_ref, wb_ref):
    @pl.when(pl.program_id(1) == 0)
    def _():
        wb_ref[...] = w_ref[...].astype(BF16)

    y = jnp.dot(a_ref[...], wb_ref[...], preferred_element_type=F32)
    o_ref[...] = x_ref[...] + gate_ref[...] * y


def outproj_residual(a, x, gate, w, tm=1024, tn=512):
    t, k = a.shape
    d = w.shape[1]
    tm = min(tm, t)
    return pl.pallas_call(
        _outproj_kernel,
        out_shape=jax.ShapeDtypeStruct((t, d), F32),
        grid=(d // tn, t // tm),
        in_specs=[pl.BlockSpec((tm, k), lambda j, i: (i, 0)),
                  pl.BlockSpec((tm, tn), lambda j, i: (i, j)),
                  pl.BlockSpec((1, tn), lambda j, i: (0, j)),
                  pl.BlockSpec((k, tn), lambda j, i: (0, j))],
        out_specs=pl.BlockSpec((tm, tn), lambda j, i: (i, j)),
        scratch_shapes=[pltpu.VMEM((k, tn), BF16)],
        compiler_params=_cparams(("arbitrary", "arbitrary")),
        name="outproj_residual",
    )(a, x, gate, w)


HALO = 16


def _conv3(main, prev, nxt, w, b, i, nt):
    tm = main.shape[0]
    rows = lax.broadcasted_iota(jnp.int32, main.shape, 0)
    pr = jnp.where(i > 0, prev[HALO - 1:HALO, :], 0.0)
    nx = jnp.where(i < nt - 1, nxt[0:1, :], 0.0)
    up = jnp.where(rows == 0, pr, pltpu.roll(main, 1, 0))
    dn = jnp.where(rows == tm - 1, nx, pltpu.roll(main, tm - 1, 0))
    return up * w[0:1, :] + main * w[1:2, :] + dn * w[2:3, :] + b


def _halo_specs(tm, tc, t, col_fn):
    nh = t // HALO
    r = tm // HALO
    return [
        pl.BlockSpec((tm, tc), lambda i, j: (i, col_fn(j))),
        pl.BlockSpec((HALO, tc), lambda i, j: (jnp.maximum(i * r - 1, 0), col_fn(j))),
        pl.BlockSpec((HALO, tc), lambda i, j: (jnp.minimum((i + 1) * r, nh - 1), col_fn(j))),
    ]


HY_GROUP = 8


def _normmod_slab_kernel(x_ref, g_ref, sc_ref, sh_ref, perm_ref, o_ref):
    s1, grp, d = x_ref.shape
    x = x_ref[...].reshape(s1 * grp, d)
    y = _normmod(x, g_ref[...], sc_ref[...], sh_ref[...]).astype(BF16)
    o_ref[...] = jnp.dot(perm_ref[...], y, preferred_element_type=F32).astype(o_ref.dtype)


def normmod_slab(x, g, sc, sh):
    t, d = x.shape
    s1 = t // DFT_NB
    grp = HY_GROUP
    n = s1 * grp
    perm = np.zeros((n, n), np.float32)
    src = np.arange(n)
    perm[(src % grp) * s1 + src // grp, src] = 1.0
    vec = pl.BlockSpec((1, d), lambda i: (0, 0))
    return pl.pallas_call(
        _normmod_slab_kernel,
        out_shape=jax.ShapeDtypeStruct((t, d), BF16),
        grid=(DFT_NB // grp,),
        in_specs=[pl.BlockSpec((s1, grp, d), lambda i: (0, i, 0)), vec, vec, vec,
                  pl.BlockSpec((n, n), lambda i: (0, 0))],
        out_specs=pl.BlockSpec((n, d), lambda i: (i, 0)),
        compiler_params=_cparams(("parallel",)),
        name="normmod_slab",
    )(x.reshape(s1, DFT_NB, d), g, sc, sh, jnp.asarray(perm).astype(BF16))


def _shift_rows(x, down):
    n = x.shape[0]
    rows = lax.broadcasted_iota(jnp.int32, x.shape, 0)
    if down:
        return jnp.where(rows == 0, 0.0, pltpu.roll(x, 1, 0))
    return jnp.where(rows == n - 1, 0.0, pltpu.roll(x, n - 1, 0))


def _hy_gate_kernel(*refs):
    mains, prevs, nexts = refs[0:3], refs[3:6], refs[6:9]
    ws, bs = refs[9:12], refs[12:15]
    vg_ref, x0_ref = refs[15:17]
    i = pl.program_id(0)
    nt = pl.num_programs(0)
    s1 = prevs[0].shape[0]
    d = mains[0].shape[1]
    grp = mains[0].shape[0] // s1

    def halo(ref, edge, down):
        h = ref[...].astype(F32)
        return jnp.where(edge, _shift_rows(h, down), h)

    ups = [halo(prevs[g], i == 0, True) for g in range(3)]
    dns = [halo(nexts[g], i == nt - 1, False) for g in range(3)]
    for j in range(grp):
        out = []
        for g in range(3):
            slab = lambda k: mains[g][k * s1:(k + 1) * s1, :].astype(F32)
            up = ups[g] if j == 0 else slab(j - 1)
            dn = dns[g] if j == grp - 1 else slab(j + 1)
            w = ws[g][...]
            out.append(up * w[0:1, :] + slab(j) * w[1:2, :] + dn * w[2:3, :] + bs[g][...])
        x0, x1, v = out
        vg_ref[:, j * d:(j + 1) * d] = (v * x1).astype(vg_ref.dtype)
        x0_ref[:, j * d:(j + 1) * d] = x0.astype(x0_ref.dtype)


def hy_gate(p, conv_w, conv_b):
    t, c3 = p.shape
    d = c3 // 3
    s1 = t // DFT_NB
    grp = HY_GROUP
    main = lambda g: pl.BlockSpec((grp * s1, d), lambda i: (i, g))
    prev = lambda g: pl.BlockSpec((s1, d), lambda i: ((i * grp + DFT_NB - 1) % DFT_NB, g))
    nxt = lambda g: pl.BlockSpec((s1, d), lambda i: (((i + 1) * grp) % DFT_NB, g))
    specs = [f(g) for f in (main, prev, nxt) for g in range(3)]
    specs += [pl.BlockSpec((SHORT_CONV, d), lambda i, g=g: (0, g)) for g in range(3)]
    specs += [pl.BlockSpec((1, d), lambda i, g=g: (0, g)) for g in range(3)]
    out_spec = pl.BlockSpec((s1, grp * d), lambda i: (0, i))
    cb = conv_b.reshape(1, c3)
    out = jax.ShapeDtypeStruct((s1, DFT_NB * d), BF16)
    return pl.pallas_call(
        _hy_gate_kernel,
        out_shape=(out, out),
        grid=(DFT_NB // grp,),
        in_specs=specs,
        out_specs=(out_spec, out_spec),
        compiler_params=_cparams(("parallel",)),
        name="hy_gate",
    )(*([p] * 9), conv_w, conv_w, conv_w, cb, cb, cb)


def _hy_features(seq):
    t = np.linspace(0.0, 1.0, seq, dtype=np.float64)[:, None]
    ang = 2.0 * math.pi * np.arange(seq, dtype=np.float64)[:, None] / seq
    bands = np.linspace(1e-4, HY_BANDS - 1, HY_BANDS, dtype=np.float64)
    z = np.concatenate([t, np.cos(ang * bands), -np.sin(ang * bands)], axis=-1)
    zp = np.zeros((seq, 128), np.float32)
    zp[:, :z.shape[1]] = z
    s1 = seq // DFT_NB
    return zp.reshape(s1, DFT_NB, 128).transpose(1, 0, 2).reshape(seq, 128)


def _hy_deltas(d):
    return np.abs(np.linspace(math.log(HY_DECAY_TARGET) / HY_FAST_DECAY,
                              math.log(HY_DECAY_TARGET) / HY_SLOW_DECAY, d,
                              dtype=np.float64)).astype(np.float32)[None, :]


def _hy_filter_kernel(z_ref, w0_ref, b0_ref, w1_ref, b1_ref, w2_ref, b2_ref, fr_ref, wo_ref,
                      dl_ref, o_ref, *, seq, grp):
    hi = lax.Precision.HIGHEST
    fr = fr_ref[...]
    a = jnp.sin(fr * (jnp.dot(z_ref[...], w0_ref[...], precision=hi, preferred_element_type=F32)
                      + b0_ref[...]))
    a = jnp.sin(fr * (jnp.dot(a, w1_ref[...], precision=hi, preferred_element_type=F32) + b1_ref[...]))
    a = jnp.sin(fr * (jnp.dot(a, w2_ref[...], precision=hi, preferred_element_type=F32) + b2_ref[...]))
    k = jnp.dot(a.astype(BF16), wo_ref[...].astype(BF16), preferred_element_type=F32)
    d2 = k.shape[1]
    d = d2 // 2
    s1 = k.shape[0] // grp
    for j in range(grp):
        s2 = pl.program_id(0) * grp + j
        tt = lax.broadcasted_iota(jnp.int32, (s1, 1), 0) * DFT_NB + s2
        window = jnp.exp(-(tt.astype(F32) * (1.0 / (seq - 1))) * dl_ref[...])
        kj = k[j * s1:(j + 1) * s1]
        o_ref[:, j * d2:j * d2 + d] = (kj[:, :d] * window).astype(o_ref.dtype)
        o_ref[:, j * d2 + d:(j + 1) * d2] = jnp.where(tt == 0, 0.0, kj[:, d:] * window).astype(o_ref.dtype)


def hy_filter(seq, w0, b0, w1, b1, w2, b2, freq, wout, grp=4):
    fw = w1.shape[0]
    d2 = wout.shape[1]
    s1 = seq // DFT_NB
    z = jnp.asarray(_hy_features(seq))
    w0p = jnp.zeros((128, fw), F32).at[:w0.shape[0]].set(w0)
    full = lambda shape: pl.BlockSpec(shape, lambda i: (0,) * len(shape))
    r = lambda v: v.reshape(1, -1)
    return pl.pallas_call(
        functools.partial(_hy_filter_kernel, seq=seq, grp=grp),
        out_shape=jax.ShapeDtypeStruct((s1, DFT_NB * d2), BF16),
        grid=(DFT_NB // grp,),
        in_specs=[pl.BlockSpec((grp * s1, 128), lambda i: (i, 0)),
                  full((128, fw)), full((1, fw)), full((fw, fw)), full((1, fw)),
                  full((fw, fw)), full((1, fw)), full((1, fw)), full((fw, d2)), full((1, d2 // 2))],
        out_specs=pl.BlockSpec((s1, grp * d2), lambda i: (0, i)),
        compiler_params=_cparams(("parallel",)),
        name="hy_filter",
    )(z, w0p, r(b0), w1, r(b1), w2, r(b2), r(freq), wout, jnp.asarray(_hy_deltas(d2 // 2)))


DFT_ROW_ALIGN = 16


def _dft_slabs(seq):
    nf = (2 * seq // DFT_NB) // 2 + 1
    return nf, -(-nf // DFT_ROW_ALIGN) * DFT_ROW_ALIGN


@functools.lru_cache(maxsize=None)
def _dft_tables(seq):
    n = 2 * seq
    nb = DFT_NB
    na = n // nb
    nf, nfp = _dft_slabs(seq)
    s1 = np.arange(na // 2)
    f1 = np.arange(nf)
    ph = 2.0 * math.pi * np.outer(f1, s1) / na
    fs = np.zeros((2 * nfp, na // 2))
    fs[:nf] = np.cos(ph)
    fs[nfp:nfp + nf] = -np.sin(ph)
    f2 = np.arange(nb)
    s2 = np.arange(nb)
    freq = f1[:, None, None] + na * f2[None, :, None]
    th = 2.0 * math.pi * ((freq * s2[None, None, :]) % n) / n
    gr, gi = np.cos(th), -np.sin(th)
    g = np.concatenate([np.concatenate([gr, -gi], axis=2),
                        np.concatenate([gi, gr], axis=2)], axis=1)
    ginv = np.transpose(g, (0, 2, 1))
    t1 = np.arange(na // 2)
    ph2 = 2.0 * math.pi * np.outer(t1, f1) / na
    wgt = np.full((nf,), 2.0)
    wgt[0] = wgt[nf - 1] = 1.0
    finv = np.zeros((na // 2, 2 * nfp))
    finv[:, :nf] = np.cos(ph2) * wgt
    finv[:, nfp:nfp + nf] = -np.sin(ph2) * wgt
    return tuple(a.astype(np.float32) for a in (fs, g, ginv, finv))


def _dft1_kernel(f_ref, x_ref, o_ref):
    y = jnp.dot(f_ref[...].astype(BF16), x_ref[...], preferred_element_type=F32)
    o_ref[...] = y.reshape(o_ref.shape).astype(o_ref.dtype)


def dft_stage1(x2d, fs, tn=8192):
    k, cols = x2d.shape
    na = fs.shape[0] // 2
    return pl.pallas_call(
        _dft1_kernel,
        out_shape=jax.ShapeDtypeStruct((2, na, cols), BF16),
        grid=(cols // tn,),
        in_specs=[pl.BlockSpec((2 * na, k), lambda c: (0, 0)),
                  pl.BlockSpec((k, tn), lambda c: (0, c))],
        out_specs=pl.BlockSpec((2, na, tn), lambda c: (0, 0, c)),
        compiler_params=_cparams(("parallel",)),
        name="dft_stage1",
    )(fs, x2d)


def _hspec_kernel(g_ref, af_ref, ab_ref, o_ref, *, scale):
    nb2, dt = o_ref.shape[1], o_ref.shape[2]
    g = g_ref[0].astype(BF16)
    hf = jnp.dot(g, af_ref[...].reshape(nb2, dt), preferred_element_type=F32)
    hb = jnp.dot(g, ab_ref[...].reshape(nb2, dt), preferred_element_type=F32)
    nb = nb2 // 2
    o_ref[0, :nb, :] = ((hf[:nb] + hb[:nb]) * scale).astype(o_ref.dtype)
    o_ref[0, nb:, :] = ((hf[nb:] - hb[nb:]) * scale).astype(o_ref.dtype)


def filter_spectrum(ak, g, d, n, dt=2048):
    nb = ak.shape[2]
    nf = g.shape[0]
    nd = d // dt
    scale = 1.0 / n
    return pl.pallas_call(
        functools.partial(_hspec_kernel, scale=scale),
        out_shape=jax.ShapeDtypeStruct((nf, 2 * nb, d), BF16),
        grid=(nf, nd),
        in_specs=[pl.BlockSpec((1, 2 * nb, 2 * nb), lambda f, j: (f, 0, 0)),
                  pl.BlockSpec((2, 1, nb, dt), lambda f, j: (0, f, 0, j)),
                  pl.BlockSpec((2, 1, nb, dt), lambda f, j: (0, f, 0, nd + j))],
        out_specs=pl.BlockSpec((1, 2 * nb, dt), lambda f, j: (f, 0, j)),
        compiler_params=_cparams(("parallel", "parallel")),
        name="filter_spectrum",
    )(g, ak, ak)


def _xspec_kernel(g_ref, gi_ref, a_ref, h_ref, o_ref, *, nf):
    nb2, dt = h_ref.shape[1], h_ref.shape[2]
    nb = nb2 // 2
    live = pl.program_id(0) < nf

    @pl.when(live)
    def _():
        x = jnp.dot(g_ref[0].astype(BF16), a_ref[...].reshape(nb2, dt), preferred_element_type=F32)
        h = h_ref[0].astype(F32)
        xr, xi, hr, hi = x[:nb], x[nb:], h[:nb], h[nb:]
        y = jnp.concatenate([xr * hr - xi * hi, xr * hi + xi * hr], axis=0).astype(BF16)
        b = jnp.dot(gi_ref[0].astype(BF16), y, preferred_element_type=F32)
        o_ref[...] = b.reshape(o_ref.shape).astype(o_ref.dtype)

    @pl.when(jnp.logical_not(live))
    def _():
        o_ref[...] = jnp.zeros_like(o_ref)


def spectrum_product(a, h, g, ginv, dt=2048):
    _, nfp, nb, d = a.shape
    nf = g.shape[0]
    slab = lambda f: jnp.minimum(f, nf - 1)
    gspec = pl.BlockSpec((1, 2 * nb, 2 * nb), lambda f, j: (slab(f), 0, 0))
    aspec = pl.BlockSpec((2, 1, nb, dt), lambda f, j: (0, f, 0, j))
    return pl.pallas_call(
        functools.partial(_xspec_kernel, nf=nf),
        out_shape=jax.ShapeDtypeStruct((2, nfp, nb, d), BF16),
        grid=(nfp, d // dt),
        in_specs=[gspec, gspec, aspec, pl.BlockSpec((1, 2 * nb, dt), lambda f, j: (slab(f), 0, j))],
        out_specs=aspec,
        compiler_params=_cparams(("parallel", "parallel")),
        name="spectrum_product",
    )(g, ginv, a, h)


def _idft2_kernel(f_ref, b_ref, vg_ref, x0_ref, skip_ref, o_ref):
    conv = jnp.dot(f_ref[...].astype(BF16), b_ref[...], preferred_element_type=F32)
    vg = vg_ref[...].astype(F32)
    o_ref[...] = ((conv + vg * skip_ref[...]) * x0_ref[...].astype(F32)).astype(o_ref.dtype)


def idft_stage2_mix(b2d, finv, vg2d, x02d, skip_t, tn=4096):
    k, cols = b2d.shape
    rows = finv.shape[0]
    return pl.pallas_call(
        _idft2_kernel,
        out_shape=jax.ShapeDtypeStruct((rows, cols), BF16),
        grid=(cols // tn,),
        in_specs=[pl.BlockSpec((rows, k), lambda c: (0, 0)),
                  pl.BlockSpec((k, tn), lambda c: (0, c)),
                  pl.BlockSpec((rows, tn), lambda c: (0, c)),
                  pl.BlockSpec((rows, tn), lambda c: (0, c)),
                  pl.BlockSpec((1, tn), lambda c: (0, 0))],
        out_specs=pl.BlockSpec((rows, tn), lambda c: (0, c)),
        compiler_params=_cparams(("parallel",)),
        name="idft_stage2_mix",
    )(finv, b2d, vg2d, x02d, skip_t)


def _outproj_slab_kernel(a_ref, x_ref, gate_ref, w_ref, o_ref, wb_ref):
    @pl.when(pl.program_id(1) == 0)
    def _():
        wb_ref[...] = w_ref[...].astype(BF16)

    s1, grp = x_ref.shape[0], x_ref.shape[1]
    k = w_ref.shape[0]
    a = jnp.concatenate([a_ref[:, j * k:(j + 1) * k] for j in range(grp)], axis=0)
    y = jnp.dot(a, wb_ref[...], preferred_element_type=F32)
    for j in range(grp):
        o_ref[:, j, :] = x_ref[:, j, :] + gate_ref[...] * y[j * s1:(j + 1) * s1]


def outproj_residual_slab(a2d, x, gate, w, tn=512):
    t, d = x.shape
    k = w.shape[0]
    s1 = t // DFT_NB
    grp = HY_GROUP
    xspec = pl.BlockSpec((s1, grp, tn), lambda j, i: (0, i, j))
    out = pl.pallas_call(
        _outproj_slab_kernel,
        out_shape=jax.ShapeDtypeStruct((s1, DFT_NB, d), F32),
        grid=(d // tn, DFT_NB // grp),
        in_specs=[pl.BlockSpec((s1, grp * k), lambda j, i: (0, i)), xspec,
                  pl.BlockSpec((1, tn), lambda j, i: (0, j)),
                  pl.BlockSpec((k, tn), lambda j, i: (0, j))],
        out_specs=xspec,
        scratch_shapes=[pltpu.VMEM((k, tn), BF16)],
        compiler_params=_cparams(("arbitrary", "arbitrary")),
        name="outproj_residual_slab",
    )(a2d, x.reshape(s1, DFT_NB, d), gate, w)
    return out.reshape(t, d)


def hyena_mixer(hm, x, gate, w_in, layer, conv_w, conv_b, f_w0, f_b0, f_w1, f_b1, f_w2, f_b2, f_freq,
                f_wout, skip, w_out):
    seq, d = hm.shape
    nb = DFT_NB
    fs, g, ginv, finv = _dft_tables(seq)
    _, nfp = _dft_slabs(seq)
    p = matmul_cols(hm, w_in, layer, 3 * d)
    vg, x0 = hy_gate(p, conv_w, conv_b)
    kf = hy_filter(seq, f_w0, f_b0, f_w1, f_b1, f_w2, f_b2, f_freq, f_wout)
    ak = dft_stage1(kf, fs)
    h = filter_spectrum(ak.reshape(2, nfp, nb, 2 * d), g, d, 2 * seq)
    a = dft_stage1(vg, fs)
    b = spectrum_product(a.reshape(2, nfp, nb, d), h, g, ginv)
    tn = 2 * d
    skip_t = jnp.tile(skip.reshape(1, d), (1, tn // d))
    y = idft_stage2_mix(b.reshape(2 * nfp, nb * d), finv, vg, x0, skip_t, tn=tn)
    return outproj_residual_slab(y, x, gate, w_out)


def _qk_prep_kernel(m_ref, p_ref, n_ref, w_ref, b_ref, s_ref, o_ref):
    i = pl.program_id(0)
    nt = pl.num_programs(0)
    f = lambda r: r[...].astype(F32)
    u = _conv3(f(m_ref), f(p_ref), f(n_ref), w_ref[...], b_ref[...], i, nt)
    o_ref[...] = (u * jax.nn.sigmoid(u) * s_ref[...]).astype(o_ref.dtype)


def qk_prep(p, conv_w, conv_b, qk_dim, dqk, tm=512, tc=512):
    t = p.shape[0]
    c = 2 * qk_dim
    scale = np.ones((1, c), np.float32)
    scale[:, qk_dim:] = 1.0 / math.sqrt(dqk)
    return pl.pallas_call(
        _qk_prep_kernel,
        out_shape=jax.ShapeDtypeStruct((t, c), BF16),
        grid=(t // tm, c // tc),
        in_specs=_halo_specs(tm, tc, t, lambda j: j) + [
            pl.BlockSpec((SHORT_CONV, tc), lambda i, j: (0, j)),
            pl.BlockSpec((1, tc), lambda i, j: (0, j)),
            pl.BlockSpec((1, tc), lambda i, j: (0, j))],
        out_specs=pl.BlockSpec((tm, tc), lambda i, j: (i, j)),
        compiler_params=_cparams(("parallel", "parallel")),
        name="qk_prep",
    )(p, p, p, conv_w, conv_b.reshape(1, c), jnp.asarray(scale))


def _log_sigmoid(x):
    return jnp.minimum(x, 0.0) - jnp.log1p(jnp.exp(-jnp.abs(x)))


def _gates_kernel(hm_ref, w_ref, b_ref, col_ref, row_ref, scal_ref):
    hi = lax.Precision.HIGHEST
    nh = ML_HEADS
    hm = hm_ref[...]
    cs = hm.shape[0]
    gt = jnp.dot(hm, w_ref[...].astype(BF16), preferred_element_type=F32) + b_ref[...]
    gtt = gt.T
    r = lax.broadcasted_iota(jnp.int32, (cs, cs), 0)
    c = lax.broadcasted_iota(jnp.int32, (cs, cs), 1)
    lower = (r >= c).astype(F32)
    upper = (r <= c).astype(F32)
    i_f, f_f, i_b, f_b = (gt[:, k * nh:(k + 1) * nh] for k in range(4))
    lf_f, lf_b = _log_sigmoid(f_f), _log_sigmoid(f_b)
    b_f = jnp.dot(lower, lf_f, precision=hi, preferred_element_type=F32)
    b_b = jnp.dot(upper, lf_b, precision=hi, preferred_element_type=F32)
    g_f = jnp.sum(lf_f, axis=0, keepdims=True)
    g_b = jnp.sum(lf_b, axis=0, keepdims=True)
    a_f = g_f - b_f + i_f
    a_b = g_b - b_b + i_b
    col_ref[...] = jnp.concatenate([b_f, b_b, a_f, a_b], axis=1)
    scal_ref[0] = jnp.concatenate([g_f, g_b, jnp.max(a_f, axis=0, keepdims=True),
                                   jnp.max(a_b, axis=0, keepdims=True)], axis=1)
    i_ft, f_ft, i_bt, f_bt = (gtt[k * nh:(k + 1) * nh, :] for k in range(4))
    b_ft = jnp.dot(_log_sigmoid(f_ft), upper, precision=hi, preferred_element_type=F32)
    b_bt = jnp.dot(_log_sigmoid(f_bt), lower, precision=hi, preferred_element_type=F32)
    row_ref[...] = jnp.concatenate([b_ft, b_bt, i_ft, i_bt], axis=0)


def ml_gates(hm, w_g, gate_b, cs):
    t, d = hm.shape
    g4 = w_g.shape[1]
    nc = t // cs
    lanes = 128
    w_pad = jnp.pad(w_g, ((0, 0), (0, lanes - g4)))
    b_pad = jnp.pad(gate_b.reshape(1, g4), ((0, 0), (0, lanes - g4)))
    return pl.pallas_call(
        _gates_kernel,
        out_shape=(jax.ShapeDtypeStruct((t, g4), F32), jax.ShapeDtypeStruct((g4, t), F32),
                   jax.ShapeDtypeStruct((nc, 1, g4), F32)),
        grid=(nc,),
        in_specs=[pl.BlockSpec((cs, d), lambda c: (c, 0)),
                  pl.BlockSpec((d, lanes), lambda c: (0, 0)),
                  pl.BlockSpec((1, lanes), lambda c: (0, 0))],
        out_specs=(pl.BlockSpec((cs, g4), lambda c: (c, 0)),
                   pl.BlockSpec((g4, cs), lambda c: (0, c)),
                   pl.BlockSpec((1, 1, g4), lambda c: (c, 0, 0))),
        compiler_params=_cparams(("parallel",)),
        name="ml_gates",
    )(hm, w_pad, b_pad)


def _mlstm_direction(qk_ref, v_ref, col_ref, row_ref, sc_ref, o_ref, c_ref, n_ref, m_ref, base, causal,
                     dqk, dv):
    nh = ML_HEADS
    cs = qk_ref.shape[0]
    qkd = nh * dqk
    per_head = lambda f: jnp.stack([f(h) for h in range(nh)])
    q = per_head(lambda h: qk_ref[:, h * dqk:(h + 1) * dqk])
    k = per_head(lambda h: qk_ref[:, qkd + h * dqk:qkd + (h + 1) * dqk])
    v = per_head(lambda h: v_ref[:, h * dv:(h + 1) * dv])
    b_col = per_head(lambda h: col_ref[:, base + h:base + h + 1])
    a_col = per_head(lambda h: col_ref[:, 2 * nh + base + h:2 * nh + base + h + 1])
    b_row = per_head(lambda h: row_ref[base + h:base + h + 1, :])
    li_row = per_head(lambda h: row_ref[2 * nh + base + h:2 * nh + base + h + 1, :])
    g = per_head(lambda h: sc_ref[0, :, base + h:base + h + 1])
    m_loc = per_head(lambda h: sc_ref[0, :, 2 * nh + base + h:2 * nh + base + h + 1])
    c_st = c_ref[base:base + nh]
    n_st = n_ref[base:base + nh]
    m_st = m_ref[base:base + nh][:, :, 0:1]
    r = lax.broadcasted_iota(jnp.int32, (1, cs, cs), 1)
    s = lax.broadcasted_iota(jnp.int32, (1, cs, cs), 2)
    mask = (s <= r) if causal else (s >= r)
    bdot = lambda a, b, ca, cb: lax.dot_general(a, b, (((ca,), (cb,)), ((0,), (0,))),
                                                preferred_element_type=F32)
    dlog = jnp.where(mask, b_col - b_row + li_row, -jnp.inf)
    m_inter = b_col + m_st
    m_t = jnp.maximum(m_inter, jnp.max(dlog, axis=-1, keepdims=True))
    p = jnp.exp(dlog - m_t) * bdot(q, k, 2, 2)
    s_inter = jnp.exp(m_inter - m_t)
    num = s_inter * bdot(q, c_st.astype(BF16), 2, 1) + bdot(p.astype(BF16), v, 2, 1)
    den = (s_inter * jnp.sum(q.astype(F32) * n_st, axis=-1, keepdims=True)
           + jnp.sum(p, axis=-1, keepdims=True))
    hout = num / jnp.maximum(jnp.abs(den), jnp.exp(-m_t))
    for h in range(nh):
        o_ref[:, h * dv:(h + 1) * dv] = hout[h]
    kw = k.astype(F32) * jnp.exp(a_col - m_loc)
    kwb = kw.astype(BF16)
    c_loc = per_head(lambda h: lax.dot_general(kwb[h], v[h], (((0,), (0,)), ((), ())),
                                               preferred_element_type=F32))
    n_loc = jnp.sum(kw, axis=1, keepdims=True)
    m_new = jnp.maximum(g + m_st, m_loc)
    s_prev = jnp.exp(g + m_st - m_new)
    s_loc = jnp.exp(m_loc - m_new)
    c_ref[base:base + nh] = s_prev * c_st + s_loc * c_loc
    n_ref[base:base + nh] = s_prev * n_st + s_loc * n_loc
    m_ref[base:base + nh] = jnp.broadcast_to(m_new, (nh,) + m_ref.shape[1:])


def _mlstm_kernel(qkf_ref, qkb_ref, vf_ref, vb_ref, colf_ref, colb_ref, rowf_ref, rowb_ref,
                  scf_ref, scb_ref, of_ref, ob_ref, c_ref, n_ref, m_ref, *, dqk, dv):
    @pl.when(pl.program_id(0) == 0)
    def _():
        c_ref[...] = jnp.zeros_like(c_ref)
        n_ref[...] = jnp.zeros_like(n_ref)
        m_ref[...] = jnp.zeros_like(m_ref)

    _mlstm_direction(qkf_ref, vf_ref, colf_ref, rowf_ref, scf_ref, of_ref, c_ref, n_ref, m_ref,
                     0, True, dqk, dv)
    _mlstm_direction(qkb_ref, vb_ref, colb_ref, rowb_ref, scb_ref, ob_ref, c_ref, n_ref, m_ref,
                     ML_HEADS, False, dqk, dv)


def mlstm_bidir(qk, p, col, row, scal, cs, dqk, dv):
    t = qk.shape[0]
    nh = ML_HEADS
    nc = t // cs
    qkd2 = 2 * nh * dqk
    vd = nh * dv
    vblk = qkd2 // vd
    g4 = col.shape[1]
    fwd = lambda c: c
    bwd = lambda c: nc - 1 - c
    mk = lambda fn: dict(
        qk=pl.BlockSpec((cs, qkd2), lambda c: (fn(c), 0)),
        v=pl.BlockSpec((cs, vd), lambda c: (fn(c), vblk)),
        col=pl.BlockSpec((cs, g4), lambda c: (fn(c), 0)),
        row=pl.BlockSpec((g4, cs), lambda c: (0, fn(c))),
        sc=pl.BlockSpec((1, 1, g4), lambda c: (fn(c), 0, 0)),
        o=pl.BlockSpec((cs, vd), lambda c: (fn(c), 0)))
    sf, sb = mk(fwd), mk(bwd)
    return pl.pallas_call(
        functools.partial(_mlstm_kernel, dqk=dqk, dv=dv),
        out_shape=(jax.ShapeDtypeStruct((t, vd), F32), jax.ShapeDtypeStruct((t, vd), F32)),
        grid=(nc,),
        in_specs=[sf["qk"], sb["qk"], sf["v"], sb["v"], sf["col"], sb["col"], sf["row"], sb["row"],
                  sf["sc"], sb["sc"]],
        out_specs=(sf["o"], sb["o"]),
        scratch_shapes=[pltpu.VMEM((2 * nh, dqk, dv), F32), pltpu.VMEM((2 * nh, 1, dqk), F32),
                        pltpu.VMEM((2 * nh, 1, 128), F32)],
        compiler_params=_cparams(("arbitrary",)),
        name="mlstm_bidir",
    )(qk, qk, p, p, col, col, row, row, scal, scal)


def _ml_post_kernel(hf_ref, hb_ref, o_ref, hg_ref, out_ref, *, dv):
    nh = ML_HEADS
    for h in range(nh):
        sl = slice(h * dv, (h + 1) * dv)
        hs = hf_ref[:, sl] + hb_ref[:, sl]
        hs = hs * lax.rsqrt(jnp.mean(hs * hs, axis=-1, keepdims=True) + EPS)
        og = jax.nn.sigmoid(o_ref[:, sl].astype(F32))
        out_ref[:, sl] = (hs * hg_ref[:, sl] * og).astype(out_ref.dtype)


def ml_post(hf, hb, p, head_g, dv, tm=512):
    t, vd = hf.shape
    oblk = p.shape[1] // vd - 1
    spec = pl.BlockSpec((tm, vd), lambda i: (i, 0))
    return pl.pallas_call(
        functools.partial(_ml_post_kernel, dv=dv),
        out_shape=jax.ShapeDtypeStruct((t, vd), BF16),
        grid=(t // tm,),
        in_specs=[spec, spec, pl.BlockSpec((tm, vd), lambda i: (i, oblk)),
                  pl.BlockSpec((1, vd), lambda i: (0, 0))],
        out_specs=spec,
        compiler_params=_cparams(("parallel",)),
        name="ml_post",
    )(hf, hb, p, head_g.reshape(1, vd))


def mlstm_mixer(hm, x, gate, w_in, layer, conv_w, conv_b, gate_b, head_g, w_out):
    seq, d = hm.shape
    nh = ML_HEADS
    qk_dim = d // 2
    dqk = qk_dim // nh
    dv = d // nh
    n_main = 2 * qk_dim + 2 * d
    cs = min(ML_CHUNK, seq)
    p = matmul_cols(hm, w_in, layer, n_main)
    col, row, scal = ml_gates(hm, w_in[layer, :, n_main:], gate_b, cs)
    qk = qk_prep(p, conv_w, conv_b, qk_dim, dqk)
    hf, hb = mlstm_bidir(qk, p, col, row, scal, cs, dqk, dv)
    a = ml_post(hf, hb, p, head_g, dv)
    return outproj_residual(a, x, gate, w_out)


def _first_argmax(vals, axis, n):
    m = jnp.max(vals, axis=axis, keepdims=True)
    iota = lax.broadcasted_iota(jnp.int32, vals.shape, axis)
    idx = jnp.min(jnp.where(vals == m, iota, n), axis=axis, keepdims=True)
    return m, idx, iota


def _router_kernel(x_ref, g_ref, sc_ref, sh_ref, rwt_ref, rb_ref,
                   hf_ref, eidx_ref, rank_ref, wk_ref, cnt_ref, carry_ref):
    ne, ng = N_EXPERTS, N_GROUPS
    per = ne // ng
    tm = x_ref.shape[0]

    @pl.when(pl.program_id(0) == 0)
    def _():
        carry_ref[...] = jnp.zeros_like(carry_ref)

    hf = _normmod(x_ref[...], g_ref[...], sc_ref[...], sh_ref[...])
    hf_ref[...] = hf
    logits = lax.dot_general(rwt_ref[...], hf, (((1,), (1,)), ((), ())),
                             precision=lax.Precision.HIGHEST, preferred_element_type=F32)
    scores = jax.nn.sigmoid(logits)
    sel = scores + rb_ref[...]
    sel3 = sel.reshape(ng, per, tm)
    m1, i1, io3 = _first_argmax(sel3, 1, per)
    m2 = jnp.max(jnp.where(io3 == i1, -jnp.inf, sel3), axis=1, keepdims=True)
    gs = (m1 + m2).reshape(ng, tm)
    gsel = jnp.zeros((ng, tm), F32)
    for _ in range(TOPK_GROUPS):
        _, gi, iog = _first_argmax(gs, 0, ng)
        hit = iog == gi
        gsel = jnp.where(hit, 1.0, gsel)
        gs = jnp.where(hit, -jnp.inf, gs)
    gmask = jnp.broadcast_to(gsel.reshape(ng, 1, tm), (ng, per, tm)).reshape(ne, tm)
    cand = jnp.where(gmask > 0.5, sel, -jnp.inf)
    picked = []
    chosen = jnp.zeros((ne, tm), F32)
    for _ in range(TOP_K):
        _, ei, ioe = _first_argmax(cand, 0, ne)
        hit = ioe == ei
        picked.append((ei, hit))
        chosen = jnp.where(hit, 1.0, chosen)
        cand = jnp.where(hit, -jnp.inf, cand)
    r = lax.broadcasted_iota(jnp.int32, (tm, tm), 0)
    c = lax.broadcasted_iota(jnp.int32, (tm, tm), 1)
    before = (r < c).astype(BF16)
    ranks = jnp.dot(chosen.astype(BF16), before, preferred_element_type=F32) + carry_ref[:, 0:1]
    carry_ref[...] = carry_ref[...] + jnp.sum(chosen, axis=1, keepdims=True)
    cnt_ref[...] = carry_ref[...]
    wks = [jnp.sum(jnp.where(hit, scores, 0.0), axis=0, keepdims=True) for _, hit in picked]
    wsum = functools.reduce(lambda a, b: a + b, wks)
    eidx_ref[...] = jnp.zeros_like(eidx_ref)
    rank_ref[...] = jnp.zeros_like(rank_ref)
    wk_ref[...] = jnp.zeros_like(wk_ref)
    for j, ((ei, hit), wk) in enumerate(zip(picked, wks)):
        eidx_ref[j:j + 1, :] = ei
        rank_ref[j:j + 1, :] = jnp.sum(jnp.where(hit, ranks, 0.0), axis=0, keepdims=True).astype(jnp.int32)
        wk_ref[j:j + 1, :] = wk / wsum * ROUTED_SCALE


def moe_router(x, g, sc, sh, router_w, router_bias, tm=512):
    t, d = x.shape
    ne = N_EXPERTS
    vec = pl.BlockSpec((1, d), lambda i: (0, 0))
    lane = pl.BlockSpec((8, tm), lambda i: (0, i))
    return pl.pallas_call(
        _router_kernel,
        out_shape=(jax.ShapeDtypeStruct((t, d), F32), jax.ShapeDtypeStruct((8, t), jnp.int32),
                   jax.ShapeDtypeStruct((8, t), jnp.int32), jax.ShapeDtypeStruct((8, t), F32),
                   jax.ShapeDtypeStruct((ne, 128), F32)),
        grid=(t // tm,),
        in_specs=[pl.BlockSpec((tm, d), lambda i: (i, 0)), vec, vec, vec,
                  pl.BlockSpec((ne, d), lambda i: (0, 0)), pl.BlockSpec((ne, 1), lambda i: (0, 0))],
        out_specs=(pl.BlockSpec((tm, d), lambda i: (i, 0)), lane, lane, lane,
                   pl.BlockSpec((ne, 128), lambda i: (0, 0))),
        scratch_shapes=[pltpu.VMEM((ne, 128), F32)],
        compiler_params=_cparams(("arbitrary",)),
        name="moe_router",
    )(x, g, sc, sh, router_w.T, router_bias.reshape(ne, 1))


def _plan_kernel(eidx_ref, rank_ref, ps_ref, dest_ref):
    ne = N_EXPERTS
    tm = eidx_ref.shape[1]
    io = lax.broadcasted_iota(jnp.int32, (ne, tm), 0)
    ps = ps_ref[...]
    dest_ref[...] = jnp.zeros_like(dest_ref)
    for j in range(TOP_K):
        hit = io == eidx_ref[j:j + 1, :]
        base = jnp.sum(jnp.where(hit, ps, 0.0), axis=0, keepdims=True)
        dest_ref[j:j + 1, :] = base.astype(jnp.int32) + rank_ref[j:j + 1, :]


def moe_plan(eidx, rank, pad_start, tm=2048):
    t = eidx.shape[1]
    tm = min(tm, t)
    lane = pl.BlockSpec((8, tm), lambda i: (0, i))
    return pl.pallas_call(
        _plan_kernel,
        out_shape=jax.ShapeDtypeStruct((8, t), jnp.int32),
        grid=(t // tm,),
        in_specs=[lane, lane, pl.BlockSpec((N_EXPERTS, 1), lambda i: (0, 0))],
        out_specs=lane,
        compiler_params=_cparams(("parallel",)),
        name="moe_plan",
    )(eidx, rank, pad_start.astype(F32).reshape(N_EXPERTS, 1))


SUBLANES = 8
_PAD_PIECES = tuple(1 << k for k in reversed(range(3, MOE_BLOCK.bit_length() - 1)))


def _dispatch_kernel(dest_ref, zs_ref, zn_ref, hf_ref, xb_ref, src_ref, zbuf, sem, zsem, *, t_total):
    tm = hf_ref.shape[0]
    base = pl.program_id(0) * tm

    def start(t, carry):
        for j in range(TOP_K):
            a = j * t_total + base + t
            d = dest_ref[a]
            src_ref[d] = a
            pltpu.make_async_copy(hf_ref.at[pl.ds(t, 1)], xb_ref.at[pl.ds(d, 1)], sem).start()
        return carry

    def wait_all():
        for _ in range(TOP_K):
            pltpu.make_async_copy(hf_ref, xb_ref.at[pl.ds(0, tm)], sem).wait()

    def fill(do_start):
        def body(e, carry):
            zs, zn = zs_ref[e], zn_ref[e]
            end = zs + zn

            def piece(src, dst):
                cp = pltpu.make_async_copy(src, dst, zsem)
                cp.start() if do_start else cp.wait()

            for r in range(SUBLANES - 1):
                @pl.when(r < (zn & (SUBLANES - 1)))
                def _():
                    piece(zbuf.at[pl.ds(0, 1)], xb_ref.at[pl.ds(zs + r, 1)])
            for p in _PAD_PIECES:
                @pl.when((zn & p) != 0)
                def _():
                    q = pl.multiple_of(end - (zn & ~(p - 1)), SUBLANES)
                    piece(zbuf.at[pl.ds(0, p)], xb_ref.at[pl.ds(q, p)])
            return carry
        lax.fori_loop(0, N_EXPERTS, body, 0)

    first = pl.program_id(0) == 0

    @pl.when(first)
    def _():
        zbuf[...] = jnp.zeros_like(zbuf)
        fill(True)

    lax.fori_loop(0, tm, start, 0)

    @pl.when(first)
    def _():
        fill(False)

    wait_all()


def moe_dispatch(dest, zero_start, zero_len, hf, n_slots, tm=256):
    t, d = hf.shape
    return pl.pallas_call(
        functools.partial(_dispatch_kernel, t_total=t),
        out_shape=(jax.ShapeDtypeStruct((n_slots, d), hf.dtype),
                   jax.ShapeDtypeStruct((n_slots,), jnp.int32)),
        grid_spec=pltpu.PrefetchScalarGridSpec(
            num_scalar_prefetch=3,
            grid=(t // tm,),
            in_specs=[pl.BlockSpec((tm, d), lambda i, *_: (i, 0))],
            out_specs=(pl.BlockSpec(memory_space=pl.ANY), pl.BlockSpec(memory_space=pltpu.SMEM)),
            scratch_shapes=[pltpu.VMEM((MOE_BLOCK // 2, d), hf.dtype),
                            pltpu.SemaphoreType.DMA(()), pltpu.SemaphoreType.DMA(())]),
        compiler_params=_cparams(("arbitrary",)),
        name="moe_dispatch",
    )(dest, zero_start, zero_len, hf)


def _ffn(x, wg, wu, wd):
    hg = jnp.dot(x, wg, preferred_element_type=F32)
    hu = jnp.dot(x, wu, preferred_element_type=F32)
    h = (hg * jax.nn.sigmoid(hg) * hu).astype(BF16)
    return jnp.dot(h, wd, preferred_element_type=F32)


def _expert_kernel(be_ref, na_ref, first_ref, next_ref, slot_ref, rv_ref, src_ref, x_ref, wg_hbm, wu_hbm,
                   wd_hbm, y6_hbm, dump_hbm, wgf, wuf, wdf, wgb, wub, wdb, ybuf, sems, ysem, *, layer):
    b = pl.program_id(0)
    n_act = na_ref[0]
    active = b < n_act

    def drain(s):
        pltpu.make_async_copy(ybuf.at[s], dump_hbm.at[pl.ds(0, MOE_BLOCK)], ysem.at[s]).wait()

    def fetch(e, s):
        return (pltpu.make_async_copy(wg_hbm.at[layer, e], wgf.at[s], sems.at[s, 0]),
                pltpu.make_async_copy(wu_hbm.at[layer, e], wuf.at[s], sems.at[s, 1]),
                pltpu.make_async_copy(wd_hbm.at[layer, e], wdf.at[s], sems.at[s, 2]))

    @pl.when(jnp.logical_and(active, first_ref[b] == 1))
    def _():
        s = slot_ref[b]

        @pl.when(b == 0)
        def _():
            for cp in fetch(be_ref[b], s):
                cp.start()

        for cp in fetch(be_ref[b], s):
            cp.wait()
        nxt = next_ref[b]

        @pl.when(nxt >= 0)
        def _():
            for cp in fetch(nxt, 1 - s):
                cp.start()

        wgb[...] = wgf[s].astype(BF16)
        wub[...] = wuf[s].astype(BF16)
        wdb[...] = wdf[s].astype(BF16)

    @pl.when(active)
    def _():
        s = b % 2

        @pl.when(b >= 2)
        def _():
            drain(s)

        ybuf[s] = _ffn(x_ref[...].astype(BF16), wgb[...], wub[...], wdb[...])
        rv = rv_ref[b]
        slot0 = b * MOE_BLOCK

        def to_token(r, carry):
            pltpu.make_async_copy(ybuf.at[s, pl.ds(r, 1)], y6_hbm.at[pl.ds(src_ref[slot0 + r], 1)],
                                  ysem.at[s]).start()
            return carry

        def to_dump(r, carry):
            pltpu.make_async_copy(ybuf.at[s, pl.ds(r, 1)], dump_hbm.at[pl.ds(slot0 + r, 1)],
                                  ysem.at[s]).start()
            return carry

        lax.fori_loop(0, rv, to_token, 0)
        lax.fori_loop(rv, MOE_BLOCK, to_dump, 0)

        @pl.when(b == n_act - 1)
        def _():
            @pl.when(b >= 1)
            def _():
                drain(1 - s)
            drain(s)


def moe_experts(blk_e, n_active, first, nxt, slot, rows_valid, src, xb, w_gate, w_up, w_down, layer,
                n_assign):
    p, d = xb.shape
    ff = w_gate.shape[3]
    nblk = p // MOE_BLOCK
    row = lambda b, be, na, *_: (jnp.minimum(b, na[0] - 1), 0)
    hbm = pl.BlockSpec(memory_space=pl.ANY)
    y6, _ = pl.pallas_call(
        functools.partial(_expert_kernel, layer=layer),
        out_shape=(jax.ShapeDtypeStruct((n_assign, d), F32), jax.ShapeDtypeStruct((p, d), F32)),
        grid_spec=pltpu.PrefetchScalarGridSpec(
            num_scalar_prefetch=7,
            grid=(nblk,),
            in_specs=[pl.BlockSpec((MOE_BLOCK, d), row), hbm, hbm, hbm],
            out_specs=(hbm, hbm),
            scratch_shapes=[pltpu.VMEM((2, d, ff), F32), pltpu.VMEM((2, d, ff), F32),
                            pltpu.VMEM((2, ff, d), F32),
                            pltpu.VMEM((d, ff), BF16), pltpu.VMEM((d, ff), BF16),
                            pltpu.VMEM((ff, d), BF16), pltpu.VMEM((2, MOE_BLOCK, d), F32),
                            pltpu.SemaphoreType.DMA((2, 3)), pltpu.SemaphoreType.DMA((2,))]),
        compiler_params=_cparams(("arbitrary",)),
        name="moe_experts",
    )(blk_e, n_active, first, nxt, slot, rows_valid, src, xb, w_gate, w_up, w_down)
    return y6


def _combine_kernel(y6_ref, x_ref, hf_ref, wk_ref, gate_ref, wg_ref, wu_ref, wd_ref, fg_ref, o_ref,
                    wgb, wub, wdb, *, final_norm):
    @pl.when(pl.program_id(0) == 0)
    def _():
        wgb[...] = wg_ref[...].astype(BF16)
        wub[...] = wu_ref[...].astype(BF16)
        wdb[...] = wd_ref[...].astype(BF16)

    acc = _ffn(hf_ref[...].astype(BF16), wgb[...], wub[...], wdb[...])
    for j in range(TOP_K):
        acc = acc + wk_ref[:, j:j + 1] * y6_ref[j]
    y = x_ref[...] + gate_ref[...] * acc
    if final_norm:
        y = (y * lax.rsqrt(jnp.mean(y * y, axis=-1, keepdims=True) + EPS)) * fg_ref[...]
    o_ref[...] = y


def moe_combine(y6, x, hf, wk_t, gate, sh_gate, sh_up, sh_down, final_g, final_norm, tm=128):
    t, d = x.shape
    ff = sh_gate.shape[1]
    tm = min(tm, t)
    tile = lambda i: (i, 0)
    full = lambda shape: pl.BlockSpec(shape, lambda i: (0, 0))
    return pl.pallas_call(
        functools.partial(_combine_kernel, final_norm=final_norm),
        out_shape=jax.ShapeDtypeStruct((t, d), F32),
        grid=(t // tm,),
        in_specs=[pl.BlockSpec((TOP_K, tm, d), lambda i: (0, i, 0)),
                  pl.BlockSpec((tm, d), tile), pl.BlockSpec((tm, d), tile),
                  pl.BlockSpec((tm, 8), tile), full((1, d)),
                  full((d, ff)), full((d, ff)), full((ff, d)), full((1, d))],
        out_specs=pl.BlockSpec((tm, d), tile),
        scratch_shapes=[pltpu.VMEM((d, ff), BF16), pltpu.VMEM((d, ff), BF16), pltpu.VMEM((ff, d), BF16)],
        compiler_params=_cparams(("arbitrary",)),
        name="moe_combine",
    )(y6, x, hf, wk_t, gate, sh_gate, sh_up, sh_down, final_g)


def moe_layer(x, g, sc, sh, gate, router_w, router_bias, w_gate, w_up, w_down, sh_gate, sh_up, sh_down,
              layer, final_g, final_norm):
    t, d = x.shape
    ne = N_EXPERTS
    hf, eidx, rank, wk, cnt = moe_router(x, g, sc, sh, router_w, router_bias)
    counts = cnt[:, 0].astype(jnp.int32)
    padded = (counts + MOE_BLOCK - 1) // MOE_BLOCK * MOE_BLOCK
    pad_end = jnp.cumsum(padded)
    pad_start = pad_end - padded
    n_slots = (t * TOP_K + ne * (MOE_BLOCK - 1) + MOE_BLOCK - 1) // MOE_BLOCK * MOE_BLOCK
    nblk = n_slots // MOE_BLOCK
    dest = moe_plan(eidx, rank, pad_start)[:TOP_K].reshape(TOP_K * t)
    blk_start = jnp.arange(nblk, dtype=jnp.int32) * MOE_BLOCK
    n_active = (pad_end[-1] // MOE_BLOCK).astype(jnp.int32).reshape(1)
    blk_e = jnp.minimum(jnp.sum(blk_start[:, None] >= pad_end[None, :], axis=1), ne - 1).astype(jnp.int32)
    last_e = jnp.max(jnp.where(counts > 0, jnp.arange(ne, dtype=jnp.int32), 0))
    live = blk_start < pad_end[-1]
    blk_e = jnp.where(live, blk_e, last_e)
    first = jnp.logical_and(live, jnp.concatenate([jnp.ones((1,), bool), blk_e[1:] != blk_e[:-1]]))
    slot = ((jnp.cumsum(first.astype(jnp.int32)) - 1) % 2).astype(jnp.int32)
    ids = jnp.arange(ne, dtype=jnp.int32)
    later = jnp.logical_and(ids[None, :] > ids[:, None], counts[None, :] > 0)
    next_e = jnp.min(jnp.where(later, ids[None, :], ne), axis=1)
    next_e = jnp.where(next_e < ne, next_e, -1)
    nxt = jnp.sum(jnp.where(blk_e[:, None] == ids[None, :], next_e[None, :], 0), axis=1).astype(jnp.int32)

    real_end = jnp.sum(jnp.where(blk_e[:, None] == ids[None, :], (pad_start + counts)[None, :], 0), axis=1)
    rows_valid = jnp.clip(real_end - blk_start, 0, MOE_BLOCK).astype(jnp.int32)

    xb, src = moe_dispatch(dest, pad_start + counts, padded - counts, hf, n_slots)
    y6 = moe_experts(blk_e, n_active, first.astype(jnp.int32), nxt, slot, rows_valid, src, xb,
                     w_gate, w_up, w_down, layer, TOP_K * t)
    return moe_combine(y6.reshape(TOP_K, t, d), x, hf, wk.T, gate, sh_gate, sh_up, sh_down,
                       final_g, final_norm)


def kernel(x, c, ada_w, ada_b, norm_mix_g, norm_ffn_g, hy_w_in, hy_conv_w, hy_conv_b, hy_f_w0, hy_f_b0, hy_f_w1, hy_f_b1, hy_f_w2, hy_f_b2, hy_f_freq, hy_f_wout, hy_skip, hy_w_out, ml_w_in, ml_conv_w, ml_conv_b, ml_gate_b, ml_head_g, ml_w_out, moe_router_w, moe_router_bias, moe_w_gate, moe_w_up, moe_w_down, sh_w_gate, sh_w_up, sh_w_down, final_g):
    bsz, seq, d = x.shape
    assert bsz == 1, "kernels are written for a single sequence"
    depth = ada_w.shape[0]
    xs = x.reshape(seq, d)
    for layer in range(depth):
        ada = ada_proj(c, ada_w, ada_b, layer)
        sh_m, sc_m, g_m, sh_f, sc_f, g_f = (ada[:, k * d:(k + 1) * d] for k in range(6))
        gm = norm_mix_g[layer].reshape(1, d)
        j = layer // 2
        if layer % 2 == 0:
            hm = normmod_slab(xs, gm, sc_m, sh_m)
            xs = hyena_mixer(hm, xs, g_m, hy_w_in, j, hy_conv_w[j], hy_conv_b[j], hy_f_w0[j], hy_f_b0[j],
                             hy_f_w1[j], hy_f_b1[j], hy_f_w2[j], hy_f_b2[j], hy_f_freq[j], hy_f_wout[j],
                             hy_skip[j], hy_w_out[j])
        else:
            hm = normmod(xs, gm, sc_m, sh_m)
            xs = mlstm_mixer(hm, xs, g_m, ml_w_in, j, ml_conv_w[j], ml_conv_b[j], ml_gate_b[j],
                             ml_head_g[j], ml_w_out[j])
        xs = moe_layer(xs, norm_ffn_g[layer].reshape(1, d), sc_f, sh_f, g_f, moe_router_w[layer],
                       moe_router_bias[layer], moe_w_gate, moe_w_up, moe_w_down,
                       sh_w_gate[layer], sh_w_up[layer], sh_w_down[layer], layer,
                       final_g.reshape(1, d), layer == depth - 1)
    return xs.reshape(bsz, seq, d)
```

```python
import functools
import math

import numpy as np
import jax
import jax.numpy as jnp
from jax import lax
from jax.experimental import pallas as pl
from jax.experimental.pallas import tpu as pltpu

F32 = jnp.float32
BF16 = jnp.bfloat16
EPS = 1e-6

V7X_VMEM_BYTES = 64 * 1024 * 1024
VMEM_LIMIT = V7X_VMEM_BYTES - 8 * 1024 * 1024

SHORT_CONV = 3
HY_BANDS = 16
HY_DECAY_TARGET = 1e-2
HY_FAST_DECAY = 0.3
HY_SLOW_DECAY = 1.5
ML_HEADS = 8
ML_CHUNK = 256
N_EXPERTS = 64
TOP_K = 6
N_GROUPS = 8
TOPK_GROUPS = 4
ROUTED_SCALE = 2.5
MOE_BLOCK = 256
DFT_NB = 128


def _cparams(sem):
    return pltpu.CompilerParams(dimension_semantics=sem, vmem_limit_bytes=VMEM_LIMIT)


def _ada_kernel(c_ref, w_ref, b_ref, o_ref, *, tc):
    rows, n = w_ref.shape[1], w_ref.shape[2]

    @pl.when(pl.program_id(0) == 0)
    def _():
        o_ref[...] = b_ref[0]

    c = c_ref[...]
    cs = c * jax.nn.sigmoid(c)
    for j in range(n // tc):
        part = (w_ref[0, :, j * tc:(j + 1) * tc] * cs).reshape(rows // 8, 8, tc).sum(axis=0)
        o_ref[:, j * tc:(j + 1) * tc] += jnp.sum(part, axis=0, keepdims=True)


def ada_proj(c, ada_w, ada_b, layer, rows=128):
    depth, d, n = ada_w.shape
    return pl.pallas_call(
        functools.partial(_ada_kernel, tc=2048),
        out_shape=jax.ShapeDtypeStruct((1, n), F32),
        grid=(d // rows,),
        in_specs=[pl.BlockSpec((rows, 1), lambda r: (r, 0)),
                  pl.BlockSpec((1, rows, n), lambda r: (layer, r, 0)),
                  pl.BlockSpec((1, 1, n), lambda r: (layer, 0, 0))],
        out_specs=pl.BlockSpec((1, n), lambda r: (0, 0)),
        compiler_params=_cparams(("arbitrary",)),
        name="ada_proj",
    )(c.reshape(d, 1), ada_w, ada_b.reshape(depth, 1, n))


def _normmod(x, g, sc, sh):
    r = lax.rsqrt(jnp.mean(x * x, axis=-1, keepdims=True) + EPS)
    return (x * r) * g * (1.0 + sc) + sh


def _normmod_kernel(x_ref, g_ref, sc_ref, sh_ref, o_ref):
    o_ref[...] = _normmod(x_ref[...], g_ref[...], sc_ref[...], sh_ref[...]).astype(o_ref.dtype)


def normmod(x, g, sc, sh, out_dtype=BF16, tm=512):
    t, d = x.shape
    vec = pl.BlockSpec((1, d), lambda i: (0, 0))
    return pl.pallas_call(
        _normmod_kernel,
        out_shape=jax.ShapeDtypeStruct((t, d), out_dtype),
        grid=(t // tm,),
        in_specs=[pl.BlockSpec((tm, d), lambda i: (i, 0)), vec, vec, vec],
        out_specs=pl.BlockSpec((tm, d), lambda i: (i, 0)),
        compiler_params=_cparams(("parallel",)),
        name="normmod",
    )(x, g, sc, sh)


def _mm_kernel(a_ref, w_ref, o_ref, wb_ref):
    @pl.when(pl.program_id(1) == 0)
    def _():
        wb_ref[...] = w_ref[0].astype(BF16)

    o_ref[...] = jnp.dot(a_ref[...], wb_ref[...], preferred_element_type=F32).astype(o_ref.dtype)


def matmul_cols(a, w, layer, n_out, out_dtype=BF16, tm=1024, tn=1024):
    m, k = a.shape
    tm = min(tm, m)
    return pl.pallas_call(
        _mm_kernel,
        out_shape=jax.ShapeDtypeStruct((m, n_out), out_dtype),
        grid=(n_out // tn, m // tm),
        in_specs=[pl.BlockSpec((tm, k), lambda j, i: (i, 0)),
                  pl.BlockSpec((1, k, tn), lambda j, i: (layer, 0, j))],
        out_specs=pl.BlockSpec((tm, tn), lambda j, i: (i, j)),
        scratch_shapes=[pltpu.VMEM((k, tn), BF16)],
        compiler_params=_cparams(("arbitrary", "arbitrary")),
        name="matmul_cols",
    )(a, w)


def _outproj_kernel(a_ref, x_ref, gate_ref, w_ref, o_ref, wb_ref):
    @pl.when(pl.program_id(1) == 0)
    def _():
        wb_ref[...] = w_ref[...].astype(BF16)

    y = jnp.dot(a_ref[...], wb_ref[...], preferred_element_type=F32)
    o_ref[...] = x_ref[...] + gate_ref[...] * y


def outproj_residual(a, x, gate, w, tm=1024, tn=1024):
    t, k = a.shape
    d = w.shape[1]
    tm = min(tm, t)
    return pl.pallas_call(
        _outproj_kernel,
        out_shape=jax.ShapeDtypeStruct((t, d), F32),
        grid=(d // tn, t // tm),
        in_specs=[pl.BlockSpec((tm, k), lambda j, i: (i, 0)),
                  pl.BlockSpec((tm, tn), lambda j, i: (i, j)),
                  pl.BlockSpec((1, tn), lambda j, i: (0, j)),
                  pl.BlockSpec((k, tn), lambda j, i: (0, j))],
        out_specs=pl.BlockSpec((tm, tn), lambda j, i: (i, j)),
        scratch_shapes=[pltpu.VMEM((k, tn), BF16)],
        compiler_params=_cparams(("arbitrary", "arbitrary")),
        name="outproj_residual",
    )(a, x, gate, w)


HALO = 16


def _conv3(main, prev, nxt, w, b, i, nt):
    tm = main.shape[0]
    rows = lax.broadcasted_iota(jnp.int32, main.shape, 0)
    pr = jnp.where(i > 0, prev[HALO - 1:HALO, :], 0.0)
    nx = jnp.where(i < nt - 1, nxt[0:1, :], 0.0)
    up = jnp.where(rows == 0, pr, pltpu.roll(main, 1, 0))
    dn = jnp.where(rows == tm - 1, nx, pltpu.roll(main, tm - 1, 0))
    return up * w[0:1, :] + main * w[1:2, :] + dn * w[2:3, :] + b


def _halo_specs(tm, tc, t, col_fn):
    nh = t // HALO
    r = tm // HALO
    return [
        pl.BlockSpec((tm, tc), lambda i, j: (i, col_fn(j))),
        pl.BlockSpec((HALO, tc), lambda i, j: (jnp.maximum(i * r - 1, 0), col_fn(j))),
        pl.BlockSpec((HALO, tc), lambda i, j: (jnp.minimum((i + 1) * r, nh - 1), col_fn(j))),
    ]


HY_GROUP = 8


def _normmod_slab_kernel(x_ref, g_ref, sc_ref, sh_ref, perm_ref, o_ref):
    s1, grp, d = x_ref.shape
    x = x_ref[...].reshape(s1 * grp, d)
    y = _normmod(x, g_ref[...], sc_ref[...], sh_ref[...]).astype(BF16)
    o_ref[...] = jnp.dot(perm_ref[...], y, preferred_element_type=F32).astype(o_ref.dtype)


def normmod_slab(x, g, sc, sh):
    t, d = x.shape
    s1 = t // DFT_NB
    grp = HY_GROUP
    n = s1 * grp
    perm = np.zeros((n, n), np.float32)
    src = np.arange(n)
    perm[(src % grp) * s1 + src // grp, src] = 1.0
    vec = pl.BlockSpec((1, d), lambda i: (0, 0))
    return pl.pallas_call(
        _normmod_slab_kernel,
        out_shape=jax.ShapeDtypeStruct((t, d), BF16),
        grid=(DFT_NB // grp,),
        in_specs=[pl.BlockSpec((s1, grp, d), lambda i: (0, i, 0)), vec, vec, vec,
                  pl.BlockSpec((n, n), lambda i: (0, 0))],
        out_specs=pl.BlockSpec((n, d), lambda i: (i, 0)),
        compiler_params=_cparams(("parallel",)),
        name="normmod_slab",
    )(x.reshape(s1, DFT_NB, d), g, sc, sh, jnp.asarray(perm).astype(BF16))


def _shift_rows(x, down):
    n = x.shape[0]
    rows = lax.broadcasted_iota(jnp.int32, x.shape, 0)
    if down:
        return jnp.where(rows == 0, 0.0, pltpu.roll(x, 1, 0))
    return jnp.where(rows == n - 1, 0.0, pltpu.roll(x, n - 1, 0))


def _hy_gate_kernel(*refs):
    mains, prevs, nexts = refs[0:3], refs[3:6], refs[6:9]
    ws, bs = refs[9:12], refs[12:15]
    vg_ref, x0_ref = refs[15:17]
    i = pl.program_id(0)
    nt = pl.num_programs(0)
    s1 = prevs[0].shape[0]
    d = mains[0].shape[1]
    grp = mains[0].shape[0] // s1

    def halo(ref, edge, down):
        h = ref[...].astype(F32)
        return jnp.where(edge, _shift_rows(h, down), h)

    ups = [halo(prevs[g], i == 0, True) for g in range(3)]
    dns = [halo(nexts[g], i == nt - 1, False) for g in range(3)]
    for j in range(grp):
        out = []
        for g in range(3):
            slab = lambda k: mains[g][k * s1:(k + 1) * s1, :].astype(F32)
            up = ups[g] if j == 0 else slab(j - 1)
            dn = dns[g] if j == grp - 1 else slab(j + 1)
            w = ws[g][...]
            out.append(up * w[0:1, :] + slab(j) * w[1:2, :] + dn * w[2:3, :] + bs[g][...])
        x0, x1, v = out
        vg_ref[:, j * d:(j + 1) * d] = (v * x1).astype(vg_ref.dtype)
        x0_ref[:, j * d:(j + 1) * d] = x0.astype(x0_ref.dtype)


def hy_gate(p, conv_w, conv_b):
    t, c3 = p.shape
    d = c3 // 3
    s1 = t // DFT_NB
    grp = HY_GROUP
    main = lambda g: pl.BlockSpec((grp * s1, d), lambda i: (i, g))
    prev = lambda g: pl.BlockSpec((s1, d), lambda i: ((i * grp + DFT_NB - 1) % DFT_NB, g))
    nxt = lambda g: pl.BlockSpec((s1, d), lambda i: (((i + 1) * grp) % DFT_NB, g))
    specs = [f(g) for f in (main, prev, nxt) for g in range(3)]
    specs += [pl.BlockSpec((SHORT_CONV, d), lambda i, g=g: (0, g)) for g in range(3)]
    specs += [pl.BlockSpec((1, d), lambda i, g=g: (0, g)) for g in range(3)]
    out_spec = pl.BlockSpec((s1, grp * d), lambda i: (0, i))
    cb = conv_b.reshape(1, c3)
    out = jax.ShapeDtypeStruct((s1, DFT_NB * d), BF16)
    return pl.pallas_call(
        _hy_gate_kernel,
        out_shape=(out, out),
        grid=(DFT_NB // grp,),
        in_specs=specs,
        out_specs=(out_spec, out_spec),
        compiler_params=_cparams(("parallel",)),
        name="hy_gate",
    )(*([p] * 9), conv_w, conv_w, conv_w, cb, cb, cb)


def _hy_features(seq):
    t = np.linspace(0.0, 1.0, seq, dtype=np.float64)[:, None]
    ang = 2.0 * math.pi * np.arange(seq, dtype=np.float64)[:, None] / seq
    bands = np.linspace(1e-4, HY_BANDS - 1, HY_BANDS, dtype=np.float64)
    z = np.concatenate([t, np.cos(ang * bands), -np.sin(ang * bands)], axis=-1)
    zp = np.zeros((seq, 128), np.float32)
    zp[:, :z.shape[1]] = z
    s1 = seq // DFT_NB
    return zp.reshape(s1, DFT_NB, 128).transpose(1, 0, 2).reshape(seq, 128)


def _hy_deltas(d):
    return np.abs(np.linspace(math.log(HY_DECAY_TARGET) / HY_FAST_DECAY,
                              math.log(HY_DECAY_TARGET) / HY_SLOW_DECAY, d,
                              dtype=np.float64)).astype(np.float32)[None, :]


def _hy_filter_kernel(z_ref, w0_ref, b0_ref, w1_ref, b1_ref, w2_ref, b2_ref, fr_ref, wo_ref,
                      dl_ref, o_ref, *, seq, grp):
    hi = lax.Precision.HIGHEST
    fr = fr_ref[...]
    a = jnp.sin(fr * (jnp.dot(z_ref[...], w0_ref[...], precision=hi, preferred_element_type=F32)
                      + b0_ref[...]))
    a = jnp.sin(fr * (jnp.dot(a, w1_ref[...], precision=hi, preferred_element_type=F32) + b1_ref[...]))
    a = jnp.sin(fr * (jnp.dot(a, w2_ref[...], precision=hi, preferred_element_type=F32) + b2_ref[...]))
    k = jnp.dot(a.astype(BF16), wo_ref[...].astype(BF16), preferred_element_type=F32)
    d2 = k.shape[1]
    d = d2 // 2
    s1 = k.shape[0] // grp
    for j in range(grp):
        s2 = pl.program_id(0) * grp + j
        tt = lax.broadcasted_iota(jnp.int32, (s1, 1), 0) * DFT_NB + s2
        window = jnp.exp(-(tt.astype(F32) * (1.0 / (seq - 1))) * dl_ref[...])
        kj = k[j * s1:(j + 1) * s1]
        o_ref[:, j * d2:j * d2 + d] = (kj[:, :d] * window).astype(o_ref.dtype)
        o_ref[:, j * d2 + d:(j + 1) * d2] = jnp.where(tt == 0, 0.0, kj[:, d:] * window).astype(o_ref.dtype)


def hy_filter(seq, w0, b0, w1, b1, w2, b2, freq, wout, grp=4):
    fw = w1.shape[0]
    d2 = wout.shape[1]
    s1 = seq // DFT_NB
    z = jnp.asarray(_hy_features(seq))
    w0p = jnp.zeros((128, fw), F32).at[:w0.shape[0]].set(w0)
    full = lambda shape: pl.BlockSpec(shape, lambda i: (0,) * len(shape))
    r = lambda v: v.reshape(1, -1)
    return pl.pallas_call(
        functools.partial(_hy_filter_kernel, seq=seq, grp=grp),
        out_shape=jax.ShapeDtypeStruct((s1, DFT_NB * d2), BF16),
        grid=(DFT_NB // grp,),
        in_specs=[pl.BlockSpec((grp * s1, 128), lambda i: (i, 0)),
                  full((128, fw)), full((1, fw)), full((fw, fw)), full((1, fw)),
                  full((fw, fw)), full((1, fw)), full((1, fw)), full((fw, d2)), full((1, d2 // 2))],
        out_specs=pl.BlockSpec((s1, grp * d2), lambda i: (0, i)),
        compiler_params=_cparams(("parallel",)),
        name="hy_filter",
    )(z, w0p, r(b0), w1, r(b1), w2, r(b2), r(freq), wout, jnp.asarray(_hy_deltas(d2 // 2)))


DFT_ROW_ALIGN = 16


def _dft_slabs(seq):
    nf = (2 * seq // DFT_NB) // 2 + 1
    return nf, -(-nf // DFT_ROW_ALIGN) * DFT_ROW_ALIGN


@functools.lru_cache(maxsize=None)
def _dft_tables(seq):
    n = 2 * seq
    nb = DFT_NB
    na = n // nb
    nf, nfp = _dft_slabs(seq)
    s1 = np.arange(na // 2)
    f1 = np.arange(nf)
    ph = 2.0 * math.pi * np.outer(f1, s1) / na
    fs = np.zeros((2 * nfp, na // 2))
    fs[:nf] = np.cos(ph)
    fs[nfp:nfp + nf] = -np.sin(ph)
    f2 = np.arange(nb)
    s2 = np.arange(nb)
    freq = f1[:, None, None] + na * f2[None, :, None]
    th = 2.0 * math.pi * ((freq * s2[None, None, :]) % n) / n
    gr, gi = np.cos(th), -np.sin(th)
    g = np.concatenate([np.concatenate([gr, -gi], axis=2),
                        np.concatenate([gi, gr], axis=2)], axis=1)
    ginv = np.transpose(g, (0, 2, 1))
    t1 = np.arange(na // 2)
    ph2 = 2.0 * math.pi * np.outer(t1, f1) / na
    wgt = np.full((nf,), 2.0)
    wgt[0] = wgt[nf - 1] = 1.0
    finv = np.zeros((na // 2, 2 * nfp))
    finv[:, :nf] = np.cos(ph2) * wgt
    finv[:, nfp:nfp + nf] = -np.sin(ph2) * wgt
    return tuple(a.astype(np.float32) for a in (fs, g, ginv, finv))


def _dft1_kernel(f_ref, x_ref, o_ref):
    y = jnp.dot(f_ref[...].astype(BF16), x_ref[...], preferred_element_type=F32)
    o_ref[...] = y.reshape(o_ref.shape).astype(o_ref.dtype)


def dft_stage1(x2d, fs, tn=8192):
    k, cols = x2d.shape
    na = fs.shape[0] // 2
    return pl.pallas_call(
        _dft1_kernel,
        out_shape=jax.ShapeDtypeStruct((2, na, cols), BF16),
        grid=(cols // tn,),
        in_specs=[pl.BlockSpec((2 * na, k), lambda c: (0, 0)),
                  pl.BlockSpec((k, tn), lambda c: (0, c))],
        out_specs=pl.BlockSpec((2, na, tn), lambda c: (0, 0, c)),
        compiler_params=_cparams(("parallel",)),
        name="dft_stage1",
    )(fs, x2d)


def _hspec_kernel(g_ref, af_ref, ab_ref, o_ref, *, scale):
    nb2, dt = o_ref.shape[1], o_ref.shape[2]
    g = g_ref[0].astype(BF16)
    hf = jnp.dot(g, af_ref[...].reshape(nb2, dt), preferred_element_type=F32)
    hb = jnp.dot(g, ab_ref[...].reshape(nb2, dt), preferred_element_type=F32)
    nb = nb2 // 2
    o_ref[0, :nb, :] = ((hf[:nb] + hb[:nb]) * scale).astype(o_ref.dtype)
    o_ref[0, nb:, :] = ((hf[nb:] - hb[nb:]) * scale).astype(o_ref.dtype)


def filter_spectrum(ak, g, d, n, dt=2048):
    nb = ak.shape[2]
    nf = g.shape[0]
    nd = d // dt
    scale = 1.0 / n
    return pl.pallas_call(
        functools.partial(_hspec_kernel, scale=scale),
        out_shape=jax.ShapeDtypeStruct((nf, 2 * nb, d), BF16),
        grid=(nf, nd),
        in_specs=[pl.BlockSpec((1, 2 * nb, 2 * nb), lambda f, j: (f, 0, 0)),
                  pl.BlockSpec((2, 1, nb, dt), lambda f, j: (0, f, 0, j)),
                  pl.BlockSpec((2, 1, nb, dt), lambda f, j: (0, f, 0, nd + j))],
        out_specs=pl.BlockSpec((1, 2 * nb, dt), lambda f, j: (f, 0, j)),
        compiler_params=_cparams(("parallel", "parallel")),
        name="filter_spectrum",
    )(g, ak, ak)


def _xspec_kernel(g_ref, gi_ref, a_ref, h_ref, o_ref, *, nf):
    nb2, dt = h_ref.shape[1], h_ref.shape[2]
    nb = nb2 // 2
    live = pl.program_id(0) < nf

    @pl.when(live)
    def _():
        x = jnp.dot(g_ref[0].astype(BF16), a_ref[...].reshape(nb2, dt), preferred_element_type=F32)
        h = h_ref[0].astype(F32)
        xr, xi, hr, hi = x[:nb], x[nb:], h[:nb], h[nb:]
        y = jnp.concatenate([xr * hr - xi * hi, xr * hi + xi * hr], axis=0).astype(BF16)
        b = jnp.dot(gi_ref[0].astype(BF16), y, preferred_element_type=F32)
        o_ref[...] = b.reshape(o_ref.shape).astype(o_ref.dtype)

    @pl.when(jnp.logical_not(live))
    def _():
        o_ref[...] = jnp.zeros_like(o_ref)


def spectrum_product(a, h, g, ginv, dt=2048):
    _, nfp, nb, d = a.shape
    nf = g.shape[0]
    slab = lambda f: jnp.minimum(f, nf - 1)
    gspec = pl.BlockSpec((1, 2 * nb, 2 * nb), lambda f, j: (slab(f), 0, 0))
    aspec = pl.BlockSpec((2, 1, nb, dt), lambda f, j: (0, f, 0, j))
    return pl.pallas_call(
        functools.partial(_xspec_kernel, nf=nf),
        out_shape=jax.ShapeDtypeStruct((2, nfp, nb, d), BF16),
        grid=(nfp, d // dt),
        in_specs=[gspec, gspec, aspec, pl.BlockSpec((1, 2 * nb, dt), lambda f, j: (slab(f), 0, j))],
        out_specs=aspec,
        compiler_params=_cparams(("parallel", "parallel")),
        name="spectrum_product",
    )(g, ginv, a, h)


def _idft2_kernel(f_ref, b_ref, vg_ref, x0_ref, skip_ref, o_ref):
    conv = jnp.dot(f_ref[...].astype(BF16), b_ref[...], preferred_element_type=F32)
    vg = vg_ref[...].astype(F32)
    o_ref[...] = ((conv + vg * skip_ref[...]) * x0_ref[...].astype(F32)).astype(o_ref.dtype)


def idft_stage2_mix(b2d, finv, vg2d, x02d, skip_t, tn=4096):
    k, cols = b2d.shape
    rows = finv.shape[0]
    return pl.pallas_call(
        _idft2_kernel,
        out_shape=jax.ShapeDtypeStruct((rows, cols), BF16),
        grid=(cols // tn,),
        in_specs=[pl.BlockSpec((rows, k), lambda c: (0, 0)),
                  pl.BlockSpec((k, tn), lambda c: (0, c)),
                  pl.BlockSpec((rows, tn), lambda c: (0, c)),
                  pl.BlockSpec((rows, tn), lambda c: (0, c)),
                  pl.BlockSpec((1, tn), lambda c: (0, 0))],
        out_specs=pl.BlockSpec((rows, tn), lambda c: (0, c)),
        compiler_params=_cparams(("parallel",)),
        name="idft_stage2_mix",
    )(finv, b2d, vg2d, x02d, skip_t)


def _outproj_slab_kernel(a_ref, x_ref, gate_ref, w_ref, o_ref, wb_ref):
    @pl.when(pl.program_id(1) == 0)
    def _():
        wb_ref[...] = w_ref[...].astype(BF16)

    s1, grp = x_ref.shape[0], x_ref.shape[1]
    k = w_ref.shape[0]
    a = jnp.concatenate([a_ref[:, j * k:(j + 1) * k] for j in range(grp)], axis=0)
    y = jnp.dot(a, wb_ref[...], preferred_element_type=F32)
    for j in range(grp):
        o_ref[:, j, :] = x_ref[:, j, :] + gate_ref[...] * y[j * s1:(j + 1) * s1]


def outproj_residual_slab(a2d, x, gate, w, tn=1024):
    t, d = x.shape
    k = w.shape[0]
    s1 = t // DFT_NB
    grp = HY_GROUP
    xspec = pl.BlockSpec((s1, grp, tn), lambda j, i: (0, i, j))
    out = pl.pallas_call(
        _outproj_slab_kernel,
        out_shape=jax.ShapeDtypeStruct((s1, DFT_NB, d), F32),
        grid=(d // tn, DFT_NB // grp),
        in_specs=[pl.BlockSpec((s1, grp * k), lambda j, i: (0, i)), xspec,
                  pl.BlockSpec((1, tn), lambda j, i: (0, j)),
                  pl.BlockSpec((k, tn), lambda j, i: (0, j))],
        out_specs=xspec,
        scratch_shapes=[pltpu.VMEM((k, tn), BF16)],
        compiler_params=_cparams(("arbitrary", "arbitrary")),
        name="outproj_residual_slab",
    )(a2d, x.reshape(s1, DFT_NB, d), gate, w)
    return out.reshape(t, d)


def hyena_mixer(hm, x, gate, w_in, layer, conv_w, conv_b, f_w0, f_b0, f_w1, f_b1, f_w2, f_b2, f_freq,
                f_wout, skip, w_out):
    seq, d = hm.shape
    nb = DFT_NB
    fs, g, ginv, finv = _dft_tables(seq)
    _, nfp = _dft_slabs(seq)
    p = matmul_cols(hm, w_in, layer, 3 * d)
    vg, x0 = hy_gate(p, conv_w, conv_b)
    kf = hy_filter(seq, f_w0, f_b0, f_w1, f_b1, f_w2, f_b2, f_freq, f_wout)
    ak = dft_stage1(kf, fs)
    h = filter_spectrum(ak.reshape(2, nfp, nb, 2 * d), g, d, 2 * seq)
    a = dft_stage1(vg, fs)
    b = spectrum_product(a.reshape(2, nfp, nb, d), h, g, ginv)
    tn = 2 * d
    skip_t = jnp.tile(skip.reshape(1, d), (1, tn // d))
    y = idft_stage2_mix(b.reshape(2 * nfp, nb * d), finv, vg, x0, skip_t, tn=tn)
    return outproj_residual_slab(y, x, gate, w_out)


def _qk_prep_kernel(m_ref, p_ref, n_ref, w_ref, b_ref, s_ref, o_ref):
    i = pl.program_id(0)
    nt = pl.num_programs(0)
    f = lambda r: r[...].astype(F32)
    u = _conv3(f(m_ref), f(p_ref), f(n_ref), w_ref[...], b_ref[...], i, nt)
    o_ref[...] = (u * jax.nn.sigmoid(u) * s_ref[...]).astype(o_ref.dtype)


def qk_prep(p, conv_w, conv_b, qk_dim, dqk, tm=512, tc=512):
    t = p.shape[0]
    c = 2 * qk_dim
    scale = np.ones((1, c), np.float32)
    scale[:, qk_dim:] = 1.0 / math.sqrt(dqk)
    return pl.pallas_call(
        _qk_prep_kernel,
        out_shape=jax.ShapeDtypeStruct((t, c), BF16),
        grid=(t // tm, c // tc),
        in_specs=_halo_specs(tm, tc, t, lambda j: j) + [
            pl.BlockSpec((SHORT_CONV, tc), lambda i, j: (0, j)),
            pl.BlockSpec((1, tc), lambda i, j: (0, j)),
            pl.BlockSpec((1, tc), lambda i, j: (0, j))],
        out_specs=pl.BlockSpec((tm, tc), lambda i, j: (i, j)),
        compiler_params=_cparams(("parallel", "parallel")),
        name="qk_prep",
    )(p, p, p, conv_w, conv_b.reshape(1, c), jnp.asarray(scale))


def _log_sigmoid(x):
    return jnp.minimum(x, 0.0) - jnp.log1p(jnp.exp(-jnp.abs(x)))


def _gates_kernel(hm_ref, w_ref, b_ref, col_ref, row_ref, scal_ref):
    hi = lax.Precision.HIGHEST
    nh = ML_HEADS
    hm = hm_ref[...]
    cs = hm.shape[0]
    gt = jnp.dot(hm, w_ref[...].astype(BF16), preferred_element_type=F32) + b_ref[...]
    gtt = gt.T
    r = lax.broadcasted_iota(jnp.int32, (cs, cs), 0)
    c = lax.broadcasted_iota(jnp.int32, (cs, cs), 1)
    lower = (r >= c).astype(F32)
    upper = (r <= c).astype(F32)
    i_f, f_f, i_b, f_b = (gt[:, k * nh:(k + 1) * nh] for k in range(4))
    lf_f, lf_b = _log_sigmoid(f_f), _log_sigmoid(f_b)
    b_f = jnp.dot(lower, lf_f, precision=hi, preferred_element_type=F32)
    b_b = jnp.dot(upper, lf_b, precision=hi, preferred_element_type=F32)
    g_f = jnp.sum(lf_f, axis=0, keepdims=True)
    g_b = jnp.sum(lf_b, axis=0, keepdims=True)
    a_f = g_f - b_f + i_f
    a_b = g_b - b_b + i_b
    col_ref[...] = jnp.concatenate([b_f, b_b, a_f, a_b], axis=1)
    scal_ref[0] = jnp.concatenate([g_f, g_b, jnp.max(a_f, axis=0, keepdims=True),
                                   jnp.max(a_b, axis=0, keepdims=True)], axis=1)
    i_ft, f_ft, i_bt, f_bt = (gtt[k * nh:(k + 1) * nh, :] for k in range(4))
    b_ft = jnp.dot(_log_sigmoid(f_ft), upper, precision=hi, preferred_element_type=F32)
    b_bt = jnp.dot(_log_sigmoid(f_bt), lower, precision=hi, preferred_element_type=F32)
    row_ref[...] = jnp.concatenate([b_ft, b_bt, i_ft, i_bt], axis=0)


def ml_gates(hm, w_g, gate_b, cs):
    t, d = hm.shape
    g4 = w_g.shape[1]
    nc = t // cs
    lanes = 128
    w_pad = jnp.pad(w_g, ((0, 0), (0, lanes - g4)))
    b_pad = jnp.pad(gate_b.reshape(1, g4), ((0, 0), (0, lanes - g4)))
    return pl.pallas_call(
        _gates_kernel,
        out_shape=(jax.ShapeDtypeStruct((t, g4), F32), jax.ShapeDtypeStruct((g4, t), F32),
                   jax.ShapeDtypeStruct((nc, 1, g4), F32)),
        grid=(nc,),
        in_specs=[pl.BlockSpec((cs, d), lambda c: (c, 0)),
                  pl.BlockSpec((d, lanes), lambda c: (0, 0)),
                  pl.BlockSpec((1, lanes), lambda c: (0, 0))],
        out_specs=(pl.BlockSpec((cs, g4), lambda c: (c, 0)),
                   pl.BlockSpec((g4, cs), lambda c: (0, c)),
                   pl.BlockSpec((1, 1, g4), lambda c: (c, 0, 0))),
        compiler_params=_cparams(("parallel",)),
        name="ml_gates",
    )(hm, w_pad, b_pad)


def _mlstm_direction(qk_ref, v_ref, col_ref, row_ref, sc_ref, o_ref, c_ref, n_ref, m_ref, base, causal,
                     dqk, dv):
    nh = ML_HEADS
    cs = qk_ref.shape[0]
    qkd = nh * dqk
    per_head = lambda f: jnp.stack([f(h) for h in range(nh)])
    q = per_head(lambda h: qk_ref[:, h * dqk:(h + 1) * dqk])
    k = per_head(lambda h: qk_ref[:, qkd + h * dqk:qkd + (h + 1) * dqk])
    v = per_head(lambda h: v_ref[:, h * dv:(h + 1) * dv])
    b_col = per_head(lambda h: col_ref[:, base + h:base + h + 1])
    a_col = per_head(lambda h: col_ref[:, 2 * nh + base + h:2 * nh + base + h + 1])
    b_row = per_head(lambda h: row_ref[base + h:base + h + 1, :])
    li_row = per_head(lambda h: row_ref[2 * nh + base + h:2 * nh + base + h + 1, :])
    g = per_head(lambda h: sc_ref[0, :, base + h:base + h + 1])
    m_loc = per_head(lambda h: sc_ref[0, :, 2 * nh + base + h:2 * nh + base + h + 1])
    c_st = c_ref[base:base + nh]
    n_st = n_ref[base:base + nh]
    m_st = m_ref[base:base + nh][:, :, 0:1]
    r = lax.broadcasted_iota(jnp.int32, (1, cs, cs), 1)
    s = lax.broadcasted_iota(jnp.int32, (1, cs, cs), 2)
    mask = (s <= r) if causal else (s >= r)
    bdot = lambda a, b, ca, cb: lax.dot_general(a, b, (((ca,), (cb,)), ((0,), (0,))),
                                                preferred_element_type=F32)
    dlog = jnp.where(mask, b_col - b_row + li_row, -jnp.inf)
    m_inter = b_col + m_st
    m_t = jnp.maximum(m_inter, jnp.max(dlog, axis=-1, keepdims=True))
    p = jnp.exp(dlog - m_t) * bdot(q, k, 2, 2)
    s_inter = jnp.exp(m_inter - m_t)
    num = s_inter * bdot(q, c_st.astype(BF16), 2, 1) + bdot(p.astype(BF16), v, 2, 1)
    den = (s_inter * jnp.sum(q.astype(F32) * n_st, axis=-1, keepdims=True)
           + jnp.sum(p, axis=-1, keepdims=True))
    hout = num / jnp.maximum(jnp.abs(den), jnp.exp(-m_t))
    for h in range(nh):
        o_ref[:, h * dv:(h + 1) * dv] = hout[h]
    kw = k.astype(F32) * jnp.exp(a_col - m_loc)
    kwb = kw.astype(BF16)
    c_loc = per_head(lambda h: lax.dot_general(kwb[h], v[h], (((0,), (0,)), ((), ())),
                                               preferred_element_type=F32))
    n_loc = jnp.sum(kw, axis=1, keepdims=True)
    m_new = jnp.maximum(g + m_st, m_loc)
    s_prev = jnp.exp(g + m_st - m_new)
    s_loc = jnp.exp(m_loc - m_new)
    c_ref[base:base + nh] = s_prev * c_st + s_loc * c_loc
    n_ref[base:base + nh] = s_prev * n_st + s_loc * n_loc
    m_ref[base:base + nh] = jnp.broadcast_to(m_new, (nh,) + m_ref.shape[1:])


def _mlstm_kernel(qkf_ref, qkb_ref, vf_ref, vb_ref, colf_ref, colb_ref, rowf_ref, rowb_ref,
                  scf_ref, scb_ref, of_ref, ob_ref, c_ref, n_ref, m_ref, *, dqk, dv):
    @pl.when(pl.program_id(0) == 0)
    def _():
        c_ref[...] = jnp.zeros_like(c_ref)
        n_ref[...] = jnp.zeros_like(n_ref)
        m_ref[...] = jnp.zeros_like(m_ref)

    _mlstm_direction(qkf_ref, vf_ref, colf_ref, rowf_ref, scf_ref, of_ref, c_ref, n_ref, m_ref,
                     0, True, dqk, dv)
    _mlstm_direction(qkb_ref, vb_ref, colb_ref, rowb_ref, scb_ref, ob_ref, c_ref, n_ref, m_ref,
                     ML_HEADS, False, dqk, dv)


def mlstm_bidir(qk, p, col, row, scal, cs, dqk, dv):
    t = qk.shape[0]
    nh = ML_HEADS
    nc = t // cs
    qkd2 = 2 * nh * dqk
    vd = nh * dv
    vblk = qkd2 // vd
    g4 = col.shape[1]
    fwd = lambda c: c
    bwd = lambda c: nc - 1 - c
    mk = lambda fn: dict(
        qk=pl.BlockSpec((cs, qkd2), lambda c: (fn(c), 0)),
        v=pl.BlockSpec((cs, vd), lambda c: (fn(c), vblk)),
        col=pl.BlockSpec((cs, g4), lambda c: (fn(c), 0)),
        row=pl.BlockSpec((g4, cs), lambda c: (0, fn(c))),
        sc=pl.BlockSpec((1, 1, g4), lambda c: (fn(c), 0, 0)),
        o=pl.BlockSpec((cs, vd), lambda c: (fn(c), 0)))
    sf, sb = mk(fwd), mk(bwd)
    return pl.pallas_call(
        functools.partial(_mlstm_kernel, dqk=dqk, dv=dv),
        out_shape=(jax.ShapeDtypeStruct((t, vd), F32), jax.ShapeDtypeStruct((t, vd), F32)),
        grid=(nc,),
        in_specs=[sf["qk"], sb["qk"], sf["v"], sb["v"], sf["col"], sb["col"], sf["row"], sb["row"],
                  sf["sc"], sb["sc"]],
        out_specs=(sf["o"], sb["o"]),
        scratch_shapes=[pltpu.VMEM((2 * nh, dqk, dv), F32), pltpu.VMEM((2 * nh, 1, dqk), F32),
                        pltpu.VMEM((2 * nh, 1, 128), F32)],
        compiler_params=_cparams(("arbitrary",)),
        name="mlstm_bidir",
    )(qk, qk, p, p, col, col, row, row, scal, scal)


def _ml_post_kernel(hf_ref, hb_ref, o_ref, hg_ref, out_ref, *, dv):
    nh = ML_HEADS
    for h in range(nh):
        sl = slice(h * dv, (h + 1) * dv)
        hs = hf_ref[:, sl] + hb_ref[:, sl]
        hs = hs * lax.rsqrt(jnp.mean(hs * hs, axis=-1, keepdims=True) + EPS)
        og = jax.nn.sigmoid(o_ref[:, sl].astype(F32))
        out_ref[:, sl] = (hs * hg_ref[:, sl] * og).astype(out_ref.dtype)


def ml_post(hf, hb, p, head_g, dv, tm=512):
    t, vd = hf.shape
    oblk = p.shape[1] // vd - 1
    spec = pl.BlockSpec((tm, vd), lambda i: (i, 0))
    return pl.pallas_call(
        functools.partial(_ml_post_kernel, dv=dv),
        out_shape=jax.ShapeDtypeStruct((t, vd), BF16),
        grid=(t // tm,),
        in_specs=[spec, spec, pl.BlockSpec((tm, vd), lambda i: (i, oblk)),
                  pl.BlockSpec((1, vd), lambda i: (0, 0))],
        out_specs=spec,
        compiler_params=_cparams(("parallel",)),
        name="ml_post",
    )(hf, hb, p, head_g.reshape(1, vd))


def mlstm_mixer(hm, x, gate, w_in, layer, conv_w, conv_b, gate_b, head_g, w_out):
    seq, d = hm.shape
    nh = ML_HEADS
    qk_dim = d // 2
    dqk = qk_dim // nh
    dv = d // nh
    n_main = 2 * qk_dim + 2 * d
    cs = min(ML_CHUNK, seq)
    p = matmul_cols(hm, w_in, layer, n_main)
    col, row, scal = ml_gates(hm, w_in[layer, :, n_main:], gate_b, cs)
    qk = qk_prep(p, conv_w, conv_b, qk_dim, dqk)
    hf, hb = mlstm_bidir(qk, p, col, row, scal, cs, dqk, dv)
    a = ml_post(hf, hb, p, head_g, dv)
    return outproj_residual(a, x, gate, w_out)


def _first_argmax(vals, axis, n):
    m = jnp.max(vals, axis=axis, keepdims=True)
    iota = lax.broadcasted_iota(jnp.int32, vals.shape, axis)
    idx = jnp.min(jnp.where(vals == m, iota, n), axis=axis, keepdims=True)
    return m, idx, iota


def _router_kernel(x_ref, g_ref, sc_ref, sh_ref, rwt_ref, rb_ref,
                   hf_ref, eidx_ref, rank_ref, wk_ref, cnt_ref, carry_ref):
    ne, ng = N_EXPERTS, N_GROUPS
    per = ne // ng
    tm = x_ref.shape[0]

    @pl.when(pl.program_id(0) == 0)
    def _():
        carry_ref[...] = jnp.zeros_like(carry_ref)

    hf = _normmod(x_ref[...], g_ref[...], sc_ref[...], sh_ref[...])
    hf_ref[...] = hf
    logits = lax.dot_general(rwt_ref[...], hf, (((1,), (1,)), ((), ())),
                             precision=lax.Precision.HIGHEST, preferred_element_type=F32)
    scores = jax.nn.sigmoid(logits)
    sel = scores + rb_ref[...]
    sel3 = sel.reshape(ng, per, tm)
    m1, i1, io3 = _first_argmax(sel3, 1, per)
    m2 = jnp.max(jnp.where(io3 == i1, -jnp.inf, sel3), axis=1, keepdims=True)
    gs = (m1 + m2).reshape(ng, tm)
    gsel = jnp.zeros((ng, tm), F32)
    for _ in range(TOPK_GROUPS):
        _, gi, iog = _first_argmax(gs, 0, ng)
        hit = iog == gi
        gsel = jnp.where(hit, 1.0, gsel)
        gs = jnp.where(hit, -jnp.inf, gs)
    gmask = jnp.broadcast_to(gsel.reshape(ng, 1, tm), (ng, per, tm)).reshape(ne, tm)
    cand = jnp.where(gmask > 0.5, sel, -jnp.inf)
    picked = []
    chosen = jnp.zeros((ne, tm), F32)
    for _ in range(TOP_K):
        _, ei, ioe = _first_argmax(cand, 0, ne)
        hit = ioe == ei
        picked.append((ei, hit))
        chosen = jnp.where(hit, 1.0, chosen)
        cand = jnp.where(hit, -jnp.inf, cand)
    r = lax.broadcasted_iota(jnp.int32, (tm, tm), 0)
    c = lax.broadcasted_iota(jnp.int32, (tm, tm), 1)
    before = (r < c).astype(BF16)
    ranks = jnp.dot(chosen.astype(BF16), before, preferred_element_type=F32) + carry_ref[:, 0:1]
    carry_ref[...] = carry_ref[...] + jnp.sum(chosen, axis=1, keepdims=True)
    cnt_ref[...] = carry_ref[...]
    wks = [jnp.sum(jnp.where(hit, scores, 0.0), axis=0, keepdims=True) for _, hit in picked]
    wsum = functools.reduce(lambda a, b: a + b, wks)
    eidx_ref[...] = jnp.zeros_like(eidx_ref)
    rank_ref[...] = jnp.zeros_like(rank_ref)
    wk_ref[...] = jnp.zeros_like(wk_ref)
    for j, ((ei, hit), wk) in enumerate(zip(picked, wks)):
        eidx_ref[j:j + 1, :] = ei
        rank_ref[j:j + 1, :] = jnp.sum(jnp.where(hit, ranks, 0.0), axis=0, keepdims=True).astype(jnp.int32)
        wk_ref[j:j + 1, :] = wk / wsum * ROUTED_SCALE


def moe_router(x, g, sc, sh, router_w, router_bias, tm=512):
    t, d = x.shape
    ne = N_EXPERTS
    vec = pl.BlockSpec((1, d), lambda i: (0, 0))
    lane = pl.BlockSpec((8, tm), lambda i: (0, i))
    return pl.pallas_call(
        _router_kernel,
        out_shape=(jax.ShapeDtypeStruct((t, d), F32), jax.ShapeDtypeStruct((8, t), jnp.int32),
                   jax.ShapeDtypeStruct((8, t), jnp.int32), jax.ShapeDtypeStruct((8, t), F32),
                   jax.ShapeDtypeStruct((ne, 128), F32)),
        grid=(t // tm,),
        in_specs=[pl.BlockSpec((tm, d), lambda i: (i, 0)), vec, vec, vec,
                  pl.BlockSpec((ne, d), lambda i: (0, 0)), pl.BlockSpec((ne, 1), lambda i: (0, 0))],
        out_specs=(pl.BlockSpec((tm, d), lambda i: (i, 0)), lane, lane, lane,
                   pl.BlockSpec((ne, 128), lambda i: (0, 0))),
        scratch_shapes=[pltpu.VMEM((ne, 128), F32)],
        compiler_params=_cparams(("arbitrary",)),
        name="moe_router",
    )(x, g, sc, sh, router_w.T, router_bias.reshape(ne, 1))


def _plan_kernel(eidx_ref, rank_ref, ps_ref, dest_ref):
    ne = N_EXPERTS
    tm = eidx_ref.shape[1]
    io = lax.broadcasted_iota(jnp.int32, (ne, tm), 0)
    ps = ps_ref[...]
    dest_ref[...] = jnp.zeros_like(dest_ref)
    for j in range(TOP_K):
        hit = io == eidx_ref[j:j + 1, :]
        base = jnp.sum(jnp.where(hit, ps, 0.0), axis=0, keepdims=True)
        dest_ref[j:j + 1, :] = base.astype(jnp.int32) + rank_ref[j:j + 1, :]


def moe_plan(eidx, rank, pad_start, tm=2048):
    t = eidx.shape[1]
    tm = min(tm, t)
    lane = pl.BlockSpec((8, tm), lambda i: (0, i))
    return pl.pallas_call(
        _plan_kernel,
        out_shape=jax.ShapeDtypeStruct((8, t), jnp.int32),
        grid=(t // tm,),
        in_specs=[lane, lane, pl.BlockSpec((N_EXPERTS, 1), lambda i: (0, 0))],
        out_specs=lane,
        compiler_params=_cparams(("parallel",)),
        name="moe_plan",
    )(eidx, rank, pad_start.astype(F32).reshape(N_EXPERTS, 1))


SUBLANES = 8
_PAD_PIECES = tuple(1 << k for k in reversed(range(3, MOE_BLOCK.bit_length() - 1)))


def _dispatch_kernel(dest_ref, zs_ref, zn_ref, hf_ref, xb_ref, zbuf, sem, zsem, *, t_total):
    tm = hf_ref.shape[0]
    base = pl.program_id(0) * tm

    def row_copy(t, j):
        d = dest_ref[j * t_total + base + t]
        return pltpu.make_async_copy(hf_ref.at[pl.ds(t, 1)], xb_ref.at[pl.ds(d, 1)], sem)

    def start(t, carry):
        for j in range(TOP_K):
            row_copy(t, j).start()
        return carry

    def wait_all():
        for _ in range(TOP_K):
            pltpu.make_async_copy(hf_ref, xb_ref.at[pl.ds(0, tm)], sem).wait()

    def fill(do_start):
        def body(e, carry):
            zs, zn = zs_ref[e], zn_ref[e]
            end = zs + zn

            def piece(src, dst):
                cp = pltpu.make_async_copy(src, dst, zsem)
                cp.start() if do_start else cp.wait()

            for r in range(SUBLANES - 1):
                @pl.when(r < (zn & (SUBLANES - 1)))
                def _():
                    piece(zbuf.at[pl.ds(0, 1)], xb_ref.at[pl.ds(zs + r, 1)])
            for p in _PAD_PIECES:
                @pl.when((zn & p) != 0)
                def _():
                    q = pl.multiple_of(end - (zn & ~(p - 1)), SUBLANES)
                    piece(zbuf.at[pl.ds(0, p)], xb_ref.at[pl.ds(q, p)])
            return carry
        lax.fori_loop(0, N_EXPERTS, body, 0)

    first = pl.program_id(0) == 0

    @pl.when(first)
    def _():
        zbuf[...] = jnp.zeros_like(zbuf)
        fill(True)

    lax.fori_loop(0, tm, start, 0)

    @pl.when(first)
    def _():
        fill(False)

    wait_all()


def moe_dispatch(dest, zero_start, zero_len, hf, n_slots, tm=256):
    t, d = hf.shape
    return pl.pallas_call(
        functools.partial(_dispatch_kernel, t_total=t),
        out_shape=jax.ShapeDtypeStruct((n_slots, d), hf.dtype),
        grid_spec=pltpu.PrefetchScalarGridSpec(
            num_scalar_prefetch=3,
            grid=(t // tm,),
            in_specs=[pl.BlockSpec((tm, d), lambda i, *_: (i, 0))],
            out_specs=pl.BlockSpec(memory_space=pl.ANY),
            scratch_shapes=[pltpu.VMEM((MOE_BLOCK // 2, d), hf.dtype),
                            pltpu.SemaphoreType.DMA(()), pltpu.SemaphoreType.DMA(())]),
        compiler_params=_cparams(("arbitrary",)),
        name="moe_dispatch",
    )(dest, zero_start, zero_len, hf)


def _ffn(x, wg, wu, wd):
    hg = jnp.dot(x, wg, preferred_element_type=F32)
    hu = jnp.dot(x, wu, preferred_element_type=F32)
    h = (hg * jax.nn.sigmoid(hg) * hu).astype(BF16)
    return jnp.dot(h, wd, preferred_element_type=F32)


def _expert_kernel(be_ref, na_ref, first_ref, next_ref, slot_ref, x_ref, wg_hbm, wu_hbm, wd_hbm, o_ref,
                   wgf, wuf, wdf, wgb, wub, wdb, sems, *, layer):
    b = pl.program_id(0)
    active = b < na_ref[0]

    def fetch(e, s):
        return (pltpu.make_async_copy(wg_hbm.at[layer, e], wgf.at[s], sems.at[s, 0]),
                pltpu.make_async_copy(wu_hbm.at[layer, e], wuf.at[s], sems.at[s, 1]),
                pltpu.make_async_copy(wd_hbm.at[layer, e], wdf.at[s], sems.at[s, 2]))

    @pl.when(jnp.logical_and(active, first_ref[b] == 1))
    def _():
        s = slot_ref[b]

        @pl.when(b == 0)
        def _():
            for cp in fetch(be_ref[b], s):
                cp.start()

        for cp in fetch(be_ref[b], s):
            cp.wait()
        nxt = next_ref[b]

        @pl.when(nxt >= 0)
        def _():
            for cp in fetch(nxt, 1 - s):
                cp.start()

        wgb[...] = wgf[s].astype(BF16)
        wub[...] = wuf[s].astype(BF16)
        wdb[...] = wdf[s].astype(BF16)

    @pl.when(active)
    def _():
        o_ref[...] = _ffn(x_ref[...].astype(BF16), wgb[...], wub[...], wdb[...])


def moe_experts(blk_e, n_active, first, nxt, slot, xb, w_gate, w_up, w_down, layer):
    p, d = xb.shape
    ff = w_gate.shape[3]
    nblk = p // MOE_BLOCK
    row = lambda b, be, na, *_: (jnp.minimum(b, na[0] - 1), 0)
    hbm = pl.BlockSpec(memory_space=pl.ANY)
    return pl.pallas_call(
        functools.partial(_expert_kernel, layer=layer),
        out_shape=jax.ShapeDtypeStruct((p, d), F32),
        grid_spec=pltpu.PrefetchScalarGridSpec(
            num_scalar_prefetch=5,
            grid=(nblk,),
            in_specs=[pl.BlockSpec((MOE_BLOCK, d), row), hbm, hbm, hbm],
            out_specs=pl.BlockSpec((MOE_BLOCK, d), row),
            scratch_shapes=[pltpu.VMEM((2, d, ff), F32), pltpu.VMEM((2, d, ff), F32),
                            pltpu.VMEM((2, ff, d), F32),
                            pltpu.VMEM((d, ff), BF16), pltpu.VMEM((d, ff), BF16),
                            pltpu.VMEM((ff, d), BF16), pltpu.SemaphoreType.DMA((2, 3))]),
        compiler_params=_cparams(("arbitrary",)),
        name="moe_experts",
    )(blk_e, n_active, first, nxt, slot, xb, w_gate, w_up, w_down)


def _combine_kernel(dest_ref, yb_ref, x_ref, hf_ref, wk_ref, gate_ref, wg_ref, wu_ref, wd_ref, fg_ref, o_ref,
                    buf, wgb, wub, wdb, sem, *, t_total, final_norm):
    tm, d = x_ref.shape
    base = pl.program_id(0) * tm

    @pl.when(pl.program_id(0) == 0)
    def _():
        wgb[...] = wg_ref[...].astype(BF16)
        wub[...] = wu_ref[...].astype(BF16)
        wdb[...] = wd_ref[...].astype(BF16)

    def row_copy(t, j):
        dst = dest_ref[j * t_total + base + t]
        return pltpu.make_async_copy(yb_ref.at[pl.ds(dst, 1)], buf.at[j, pl.ds(t, 1)], sem)

    def start(t, carry):
        for j in range(TOP_K):
            row_copy(t, j).start()
        return carry

    lax.fori_loop(0, tm, start, 0)
    acc = _ffn(hf_ref[...].astype(BF16), wgb[...], wub[...], wdb[...])
    for j in range(TOP_K):
        pltpu.make_async_copy(yb_ref.at[pl.ds(0, tm)], buf.at[j], sem).wait()
    for j in range(TOP_K):
        acc = acc + wk_ref[:, j:j + 1] * buf[j]
    y = x_ref[...] + gate_ref[...] * acc
    if final_norm:
        y = (y * lax.rsqrt(jnp.mean(y * y, axis=-1, keepdims=True) + EPS)) * fg_ref[...]
    o_ref[...] = y


def moe_combine(dest, yb, x, hf, wk_t, gate, sh_gate, sh_up, sh_down, final_g, final_norm, tm=128):
    t, d = x.shape
    ff = sh_gate.shape[1]
    tm = min(tm, t)
    tile = lambda i, dest: (i, 0)
    full = lambda shape: pl.BlockSpec(shape, lambda i, dest: (0, 0))
    return pl.pallas_call(
        functools.partial(_combine_kernel, t_total=t, final_norm=final_norm),
        out_shape=jax.ShapeDtypeStruct((t, d), F32),
        grid_spec=pltpu.PrefetchScalarGridSpec(
            num_scalar_prefetch=1,
            grid=(t // tm,),
            in_specs=[pl.BlockSpec(memory_space=pl.ANY),
                      pl.BlockSpec((tm, d), tile), pl.BlockSpec((tm, d), tile),
                      pl.BlockSpec((tm, 8), tile), full((1, d)),
                      full((d, ff)), full((d, ff)), full((ff, d)), full((1, d))],
            out_specs=pl.BlockSpec((tm, d), tile),
            scratch_shapes=[pltpu.VMEM((TOP_K, tm, d), F32),
                            pltpu.VMEM((d, ff), BF16), pltpu.VMEM((d, ff), BF16), pltpu.VMEM((ff, d), BF16),
                            pltpu.SemaphoreType.DMA(())]),
        compiler_params=_cparams(("arbitrary",)),
        name="moe_combine",
    )(dest, yb, x, hf, wk_t, gate, sh_gate, sh_up, sh_down, final_g)


def moe_layer(x, g, sc, sh, gate, router_w, router_bias, w_gate, w_up, w_down, sh_gate, sh_up, sh_down,
              layer, final_g, final_norm):
    t, d = x.shape
    ne = N_EXPERTS
    hf, eidx, rank, wk, cnt = moe_router(x, g, sc, sh, router_w, router_bias)
    counts = cnt[:, 0].astype(jnp.int32)
    padded = (counts + MOE_BLOCK - 1) // MOE_BLOCK * MOE_BLOCK
    pad_end = jnp.cumsum(padded)
    pad_start = pad_end - padded
    n_slots = (t * TOP_K + ne * (MOE_BLOCK - 1) + MOE_BLOCK - 1) // MOE_BLOCK * MOE_BLOCK
    nblk = n_slots // MOE_BLOCK
    dest = moe_plan(eidx, rank, pad_start)[:TOP_K].reshape(TOP_K * t)
    blk_start = jnp.arange(nblk, dtype=jnp.int32) * MOE_BLOCK
    n_active = (pad_end[-1] // MOE_BLOCK).astype(jnp.int32).reshape(1)
    blk_e = jnp.minimum(jnp.sum(blk_start[:, None] >= pad_end[None, :], axis=1), ne - 1).astype(jnp.int32)
    last_e = jnp.max(jnp.where(counts > 0, jnp.arange(ne, dtype=jnp.int32), 0))
    live = blk_start < pad_end[-1]
    blk_e = jnp.where(live, blk_e, last_e)
    first = jnp.logical_and(live, jnp.concatenate([jnp.ones((1,), bool), blk_e[1:] != blk_e[:-1]]))
    slot = ((jnp.cumsum(first.astype(jnp.int32)) - 1) % 2).astype(jnp.int32)
    ids = jnp.arange(ne, dtype=jnp.int32)
    later = jnp.logical_and(ids[None, :] > ids[:, None], counts[None, :] > 0)
    next_e = jnp.min(jnp.where(later, ids[None, :], ne), axis=1)
    next_e = jnp.where(next_e < ne, next_e, -1)
    nxt = jnp.sum(jnp.where(blk_e[:, None] == ids[None, :], next_e[None, :], 0), axis=1).astype(jnp.int32)

    xb = moe_dispatch(dest, pad_start + counts, padded - counts, hf, n_slots)
    yb = moe_experts(blk_e, n_active, first.astype(jnp.int32), nxt, slot, xb, w_gate, w_up, w_down, layer)
    return moe_combine(dest, yb, x, hf, wk.T, gate, sh_gate, sh_up, sh_down, final_g, final_norm)


def kernel(x, c, ada_w, ada_b, norm_mix_g, norm_ffn_g, hy_w_in, hy_conv_w, hy_conv_b, hy_f_w0, hy_f_b0, hy_f_w1, hy_f_b1, hy_f_w2, hy_f_b2, hy_f_freq, hy_f_wout, hy_skip, hy_w_out, ml_w_in, ml_conv_w, ml_conv_b, ml_gate_b, ml_head_g, ml_w_out, moe_router_w, moe_router_bias, moe_w_gate, moe_w_up, moe_w_down, sh_w_gate, sh_w_up, sh_w_down, final_g):
    bsz, seq, d = x.shape
    assert bsz == 1, "kernels are written for a single sequence"
    depth = ada_w.shape[0]
    xs = x.reshape(seq, d)
    for layer in range(depth):
        ada = ada_proj(c, ada_w, ada_b, layer)
        sh_m, sc_m, g_m, sh_f, sc_f, g_f = (ada[:, k * d:(k + 1) * d] for k in range(6))
        gm = norm_mix_g[layer].reshape(1, d)
        j = layer // 2
        if layer % 2 == 0:
            hm = normmod_slab(xs, gm, sc_m, sh_m)
            xs = hyena_mixer(hm, xs, g_m, hy_w_in, j, hy_conv_w[j], hy_conv_b[j], hy_f_w0[j], hy_f_b0[j],
                             hy_f_w1[j], hy_f_b1[j], hy_f_w2[j], hy_f_b2[j], hy_f_freq[j], hy_f_wout[j],
                             hy_skip[j], hy_w_out[j])
        else:
            hm = normmod(xs, gm, sc_m, sh_m)
            xs = mlstm_mixer(hm, xs, g_m, ml_w_in, j, ml_conv_w[j], ml_conv_b[j], ml_gate_b[j],
                             ml_head_g[j], ml_w_out[j])
        xs = moe_layer(xs, norm_ffn_g[layer].reshape(1, d), sc_f, sh_f, g_f, moe_router_w[layer],
                       moe_router_bias[layer], moe_w_gate, moe_w_up, moe_w_down,
                       sh_w_gate[layer], sh_w_up[layer], sh_w_down[layer], layer,
                       final_g.reshape(1, d), layer == depth - 1)
    return xs.reshape(bsz, seq, d)
```

```python
import functools
import math

import numpy as np
import jax
import jax.numpy as jnp
from jax import lax
from jax.experimental import pallas as pl
from jax.experimental.pallas import tpu as pltpu

F32 = jnp.float32
BF16 = jnp.bfloat16
EPS = 1e-6

V7X_VMEM_BYTES = 64 * 1024 * 1024
VMEM_LIMIT = V7X_VMEM_BYTES - 8 * 1024 * 1024

SHORT_CONV = 3
HY_BANDS = 16
HY_DECAY_TARGET = 1e-2
HY_FAST_DECAY = 0.3
HY_SLOW_DECAY = 1.5
ML_HEADS = 8
ML_CHUNK = 256
N_EXPERTS = 64
TOP_K = 6
N_GROUPS = 8
TOPK_GROUPS = 4
ROUTED_SCALE = 2.5
MOE_BLOCK = 256
DFT_NB = 128


def _cparams(sem):
    return pltpu.CompilerParams(dimension_semantics=sem, vmem_limit_bytes=VMEM_LIMIT)


def _ada_kernel(c_ref, w_ref, b_ref, o_ref, *, tc):
    rows, n = w_ref.shape[1], w_ref.shape[2]

    @pl.when(pl.program_id(0) == 0)
    def _():
        o_ref[...] = b_ref[0]

    c = c_ref[...]
    cs = c * jax.nn.sigmoid(c)
    for j in range(n // tc):
        part = (w_ref[0, :, j * tc:(j + 1) * tc] * cs).reshape(rows // 8, 8, tc).sum(axis=0)
        o_ref[:, j * tc:(j + 1) * tc] += jnp.sum(part, axis=0, keepdims=True)


def ada_proj(c, ada_w, ada_b, layer, rows=128):
    depth, d, n = ada_w.shape
    return pl.pallas_call(
        functools.partial(_ada_kernel, tc=2048),
        out_shape=jax.ShapeDtypeStruct((1, n), F32),
        grid=(d // rows,),
        in_specs=[pl.BlockSpec((rows, 1), lambda r: (r, 0)),
                  pl.BlockSpec((1, rows, n), lambda r: (layer, r, 0)),
                  pl.BlockSpec((1, 1, n), lambda r: (layer, 0, 0))],
        out_specs=pl.BlockSpec((1, n), lambda r: (0, 0)),
        compiler_params=_cparams(("arbitrary",)),
        name="ada_proj",
    )(c.reshape(d, 1), ada_w, ada_b.reshape(depth, 1, n))


def _normmod(x, g, sc, sh):
    r = lax.rsqrt(jnp.mean(x * x, axis=-1, keepdims=True) + EPS)
    return (x * r) * g * (1.0 + sc) + sh


def _normmod_kernel(x_ref, g_ref, sc_ref, sh_ref, o_ref):
    o_ref[...] = _normmod(x_ref[...], g_ref[...], sc_ref[...], sh_ref[...]).astype(o_ref.dtype)


def normmod(x, g, sc, sh, out_dtype=BF16, tm=512):
    t, d = x.shape
    vec = pl.BlockSpec((1, d), lambda i: (0, 0))
    return pl.pallas_call(
        _normmod_kernel,
        out_shape=jax.ShapeDtypeStruct((t, d), out_dtype),
        grid=(t // tm,),
        in_specs=[pl.BlockSpec((tm, d), lambda i: (i, 0)), vec, vec, vec],
        out_specs=pl.BlockSpec((tm, d), lambda i: (i, 0)),
        compiler_params=_cparams(("parallel",)),
        name="normmod",
    )(x, g, sc, sh)


def _mm_kernel(a_ref, w_ref, o_ref, wb_ref):
    @pl.when(pl.program_id(1) == 0)
    def _():
        wb_ref[...] = w_ref[0].astype(BF16)

    o_ref[...] = jnp.dot(a_ref[...], wb_ref[...], preferred_element_type=F32).astype(o_ref.dtype)


def matmul_cols(a, w, layer, n_out, out_dtype=BF16, tm=1024, tn=1024):
    m, k = a.shape
    tm = min(tm, m)
    return pl.pallas_call(
        _mm_kernel,
        out_shape=jax.ShapeDtypeStruct((m, n_out), out_dtype),
        grid=(n_out // tn, m // tm),
        in_specs=[pl.BlockSpec((tm, k), lambda j, i: (i, 0)),
                  pl.BlockSpec((1, k, tn), lambda j, i: (layer, 0, j))],
        out_specs=pl.BlockSpec((tm, tn), lambda j, i: (i, j)),
        scratch_shapes=[pltpu.VMEM((k, tn), BF16)],
        compiler_params=_cparams(("arbitrary", "arbitrary")),
        name="matmul_cols",
    )(a, w)


def _outproj_kernel(a_ref, x_ref, gate_ref, w_ref, o_ref, wb_ref):
    @pl.when(pl.program_id(1) == 0)
    def _():
        wb_ref[...] = w_ref[...].astype(BF16)

    y = jnp.dot(a_ref[...], wb_ref[...], preferred_element_type=F32)
    o_ref[...] = x_ref[...] + gate_ref[...] * y


def outproj_residual(a, x, gate, w, tm=1024, tn=1024):
    t, k = a.shape
    d = w.shape[1]
    tm = min(tm, t)
    return pl.pallas_call(
        _outproj_kernel,
        out_shape=jax.ShapeDtypeStruct((t, d), F32),
        grid=(d // tn, t // tm),
        in_specs=[pl.BlockSpec((tm, k), lambda j, i: (i, 0)),
                  pl.BlockSpec((tm, tn), lambda j, i: (i, j)),
                  pl.BlockSpec((1, tn), lambda j, i: (0, j)),
                  pl.BlockSpec((k, tn), lambda j, i: (0, j))],
        out_specs=pl.BlockSpec((tm, tn), lambda j, i: (i, j)),
        scratch_shapes=[pltpu.VMEM((k, tn), BF16)],
        compiler_params=_cparams(("arbitrary", "arbitrary")),
        name="outproj_residual",
    )(a, x, gate, w)


HALO = 16


def _conv3(main, prev, nxt, w, b, i, nt):
    tm = main.shape[0]
    rows = lax.broadcasted_iota(jnp.int32, main.shape, 0)
    pr = jnp.where(i > 0, prev[HALO - 1:HALO, :], 0.0)
    nx = jnp.where(i < nt - 1, nxt[0:1, :], 0.0)
    up = jnp.where(rows == 0, pr, pltpu.roll(main, 1, 0))
    dn = jnp.where(rows == tm - 1, nx, pltpu.roll(main, tm - 1, 0))
    return up * w[0:1, :] + main * w[1:2, :] + dn * w[2:3, :] + b


def _halo_specs(tm, tc, t, col_fn):
    nh = t // HALO
    r = tm // HALO
    return [
        pl.BlockSpec((tm, tc), lambda i, j: (i, col_fn(j))),
        pl.BlockSpec((HALO, tc), lambda i, j: (jnp.maximum(i * r - 1, 0), col_fn(j))),
        pl.BlockSpec((HALO, tc), lambda i, j: (jnp.minimum((i + 1) * r, nh - 1), col_fn(j))),
    ]


HY_GROUP = 8


def _normmod_slab_kernel(x_ref, g_ref, sc_ref, sh_ref, perm_ref, o_ref):
    s1, grp, d = x_ref.shape
    x = x_ref[...].reshape(s1 * grp, d)
    y = _normmod(x, g_ref[...], sc_ref[...], sh_ref[...]).astype(BF16)
    o_ref[...] = jnp.dot(perm_ref[...], y, preferred_element_type=F32).astype(o_ref.dtype)


def normmod_slab(x, g, sc, sh):
    t, d = x.shape
    s1 = t // DFT_NB
    grp = HY_GROUP
    n = s1 * grp
    perm = np.zeros((n, n), np.float32)
    src = np.arange(n)
    perm[(src % grp) * s1 + src // grp, src] = 1.0
    vec = pl.BlockSpec((1, d), lambda i: (0, 0))
    return pl.pallas_call(
        _normmod_slab_kernel,
        out_shape=jax.ShapeDtypeStruct((t, d), BF16),
        grid=(DFT_NB // grp,),
        in_specs=[pl.BlockSpec((s1, grp, d), lambda i: (0, i, 0)), vec, vec, vec,
                  pl.BlockSpec((n, n), lambda i: (0, 0))],
        out_specs=pl.BlockSpec((n, d), lambda i: (i, 0)),
        compiler_params=_cparams(("parallel",)),
        name="normmod_slab",
    )(x.reshape(s1, DFT_NB, d), g, sc, sh, jnp.asarray(perm).astype(BF16))


def _shift_rows(x, down):
    n = x.shape[0]
    rows = lax.broadcasted_iota(jnp.int32, x.shape, 0)
    if down:
        return jnp.where(rows == 0, 0.0, pltpu.roll(x, 1, 0))
    return jnp.where(rows == n - 1, 0.0, pltpu.roll(x, n - 1, 0))


def _hy_gate_kernel(*refs):
    mains, prevs, nexts = refs[0:3], refs[3:6], refs[6:9]
    ws, bs = refs[9:12], refs[12:15]
    vg_ref, x0_ref = refs[15:17]
    i = pl.program_id(0)
    nt = pl.num_programs(0)
    s1 = prevs[0].shape[0]
    d = mains[0].shape[1]
    grp = mains[0].shape[0] // s1

    def halo(ref, edge, down):
        h = ref[...].astype(F32)
        return jnp.where(edge, _shift_rows(h, down), h)

    ups = [halo(prevs[g], i == 0, True) for g in range(3)]
    dns = [halo(nexts[g], i == nt - 1, False) for g in range(3)]
    for j in range(grp):
        out = []
        for g in range(3):
            slab = lambda k: mains[g][k * s1:(k + 1) * s1, :].astype(F32)
            up = ups[g] if j == 0 else slab(j - 1)
            dn = dns[g] if j == grp - 1 else slab(j + 1)
            w = ws[g][...]
            out.append(up * w[0:1, :] + slab(j) * w[1:2, :] + dn * w[2:3, :] + bs[g][...])
        x0, x1, v = out
        vg_ref[:, j * d:(j + 1) * d] = (v * x1).astype(vg_ref.dtype)
        x0_ref[:, j * d:(j + 1) * d] = x0.astype(x0_ref.dtype)


def hy_gate(p, conv_w, conv_b):
    t, c3 = p.shape
    d = c3 // 3
    s1 = t // DFT_NB
    grp = HY_GROUP
    main = lambda g: pl.BlockSpec((grp * s1, d), lambda i: (i, g))
    prev = lambda g: pl.BlockSpec((s1, d), lambda i: ((i * grp + DFT_NB - 1) % DFT_NB, g))
    nxt = lambda g: pl.BlockSpec((s1, d), lambda i: (((i + 1) * grp) % DFT_NB, g))
    specs = [f(g) for f in (main, prev, nxt) for g in range(3)]
    specs += [pl.BlockSpec((SHORT_CONV, d), lambda i, g=g: (0, g)) for g in range(3)]
    specs += [pl.BlockSpec((1, d), lambda i, g=g: (0, g)) for g in range(3)]
    out_spec = pl.BlockSpec((s1, grp * d), lambda i: (0, i))
    cb = conv_b.reshape(1, c3)
    out = jax.ShapeDtypeStruct((s1, DFT_NB * d), BF16)
    return pl.pallas_call(
        _hy_gate_kernel,
        out_shape=(out, out),
        grid=(DFT_NB // grp,),
        in_specs=specs,
        out_specs=(out_spec, out_spec),
        compiler_params=_cparams(("parallel",)),
        name="hy_gate",
    )(*([p] * 9), conv_w, conv_w, conv_w, cb, cb, cb)


def _hy_features(seq):
    t = np.linspace(0.0, 1.0, seq, dtype=np.float64)[:, None]
    ang = 2.0 * math.pi * np.arange(seq, dtype=np.float64)[:, None] / seq
    bands = np.linspace(1e-4, HY_BANDS - 1, HY_BANDS, dtype=np.float64)
    z = np.concatenate([t, np.cos(ang * bands), -np.sin(ang * bands)], axis=-1)
    zp = np.zeros((seq, 128), np.float32)
    zp[:, :z.shape[1]] = z
    s1 = seq // DFT_NB
    return zp.reshape(s1, DFT_NB, 128).transpose(1, 0, 2).reshape(seq, 128)


def _hy_deltas(d):
    return np.abs(np.linspace(math.log(HY_DECAY_TARGET) / HY_FAST_DECAY,
                              math.log(HY_DECAY_TARGET) / HY_SLOW_DECAY, d,
                              dtype=np.float64)).astype(np.float32)[None, :]


def _hy_filter_kernel(z_ref, w0_ref, b0_ref, w1_ref, b1_ref, w2_ref, b2_ref, fr_ref, wo_ref,
                      dl_ref, o_ref, *, seq, grp):
    hi = lax.Precision.HIGHEST
    fr = fr_ref[...]
    a = jnp.sin(fr * (jnp.dot(z_ref[...], w0_ref[...], precision=hi, preferred_element_type=F32)
                      + b0_ref[...]))
    a = jnp.sin(fr * (jnp.dot(a, w1_ref[...], precision=hi, preferred_element_type=F32) + b1_ref[...]))
    a = jnp.sin(fr * (jnp.dot(a, w2_ref[...], precision=hi, preferred_element_type=F32) + b2_ref[...]))
    k = jnp.dot(a.astype(BF16), wo_ref[...].astype(BF16), preferred_element_type=F32)
    d2 = k.shape[1]
    d = d2 // 2
    s1 = k.shape[0] // grp
    for j in range(grp):
        s2 = pl.program_id(0) * grp + j
        tt = lax.broadcasted_iota(jnp.int32, (s1, 1), 0) * DFT_NB + s2
        window = jnp.exp(-(tt.astype(F32) * (1.0 / (seq - 1))) * dl_ref[...])
        kj = k[j * s1:(j + 1) * s1]
        o_ref[:, j * d2:j * d2 + d] = (kj[:, :d] * window).astype(o_ref.dtype)
        o_ref[:, j * d2 + d:(j + 1) * d2] = jnp.where(tt == 0, 0.0, kj[:, d:] * window).astype(o_ref.dtype)


def hy_filter(seq, w0, b0, w1, b1, w2, b2, freq, wout, grp=4):
    fw = w1.shape[0]
    d2 = wout.shape[1]
    s1 = seq // DFT_NB
    z = jnp.asarray(_hy_features(seq))
    w0p = jnp.zeros((128, fw), F32).at[:w0.shape[0]].set(w0)
    full = lambda shape: pl.BlockSpec(shape, lambda i: (0,) * len(shape))
    r = lambda v: v.reshape(1, -1)
    return pl.pallas_call(
        functools.partial(_hy_filter_kernel, seq=seq, grp=grp),
        out_shape=jax.ShapeDtypeStruct((s1, DFT_NB * d2), BF16),
        grid=(DFT_NB // grp,),
        in_specs=[pl.BlockSpec((grp * s1, 128), lambda i: (i, 0)),
                  full((128, fw)), full((1, fw)), full((fw, fw)), full((1, fw)),
                  full((fw, fw)), full((1, fw)), full((1, fw)), full((fw, d2)), full((1, d2 // 2))],
        out_specs=pl.BlockSpec((s1, grp * d2), lambda i: (0, i)),
        compiler_params=_cparams(("parallel",)),
        name="hy_filter",
    )(z, w0p, r(b0), w1, r(b1), w2, r(b2), r(freq), wout, jnp.asarray(_hy_deltas(d2 // 2)))


DFT_ROW_ALIGN = 16


def _dft_slabs(seq):
    nf = (2 * seq // DFT_NB) // 2 + 1
    return nf, -(-nf // DFT_ROW_ALIGN) * DFT_ROW_ALIGN


@functools.lru_cache(maxsize=None)
def _dft_tables(seq):
    n = 2 * seq
    nb = DFT_NB
    na = n // nb
    nf, nfp = _dft_slabs(seq)
    s1 = np.arange(na // 2)
    f1 = np.arange(nf)
    ph = 2.0 * math.pi * np.outer(f1, s1) / na
    fs = np.zeros((2 * nfp, na // 2))
    fs[:nf] = np.cos(ph)
    fs[nfp:nfp + nf] = -np.sin(ph)
    f2 = np.arange(nb)
    s2 = np.arange(nb)
    freq = f1[:, None, None] + na * f2[None, :, None]
    th = 2.0 * math.pi * ((freq * s2[None, None, :]) % n) / n
    gr, gi = np.cos(th), -np.sin(th)
    g = np.concatenate([np.concatenate([gr, -gi], axis=2),
                        np.concatenate([gi, gr], axis=2)], axis=1)
    ginv = np.transpose(g, (0, 2, 1))
    t1 = np.arange(na // 2)
    ph2 = 2.0 * math.pi * np.outer(t1, f1) / na
    wgt = np.full((nf,), 2.0)
    wgt[0] = wgt[nf - 1] = 1.0
    finv = np.zeros((na // 2, 2 * nfp))
    finv[:, :nf] = np.cos(ph2) * wgt
    finv[:, nfp:nfp + nf] = -np.sin(ph2) * wgt
    return tuple(a.astype(np.float32) for a in (fs, g, ginv, finv))


def _dft1_kernel(f_ref, x_ref, o_ref):
    y = jnp.dot(f_ref[...].astype(BF16), x_ref[...], preferred_element_type=F32)
    o_ref[...] = y.reshape(o_ref.shape).astype(o_ref.dtype)


def dft_stage1(x2d, fs, tn=8192):
    k, cols = x2d.shape
    na = fs.shape[0] // 2
    return pl.pallas_call(
        _dft1_kernel,
        out_shape=jax.ShapeDtypeStruct((2, na, cols), BF16),
        grid=(cols // tn,),
        in_specs=[pl.BlockSpec((2 * na, k), lambda c: (0, 0)),
                  pl.BlockSpec((k, tn), lambda c: (0, c))],
        out_specs=pl.BlockSpec((2, na, tn), lambda c: (0, 0, c)),
        compiler_params=_cparams(("parallel",)),
        name="dft_stage1",
    )(fs, x2d)


def _hspec_kernel(g_ref, af_ref, ab_ref, o_ref, *, scale):
    nb2, dt = o_ref.shape[1], o_ref.shape[2]
    g = g_ref[0].astype(BF16)
    hf = jnp.dot(g, af_ref[...].reshape(nb2, dt), preferred_element_type=F32)
    hb = jnp.dot(g, ab_ref[...].reshape(nb2, dt), preferred_element_type=F32)
    nb = nb2 // 2
    o_ref[0, :nb, :] = ((hf[:nb] + hb[:nb]) * scale).astype(o_ref.dtype)
    o_ref[0, nb:, :] = ((hf[nb:] - hb[nb:]) * scale).astype(o_ref.dtype)


def filter_spectrum(ak, g, d, n, dt=2048):
    nb = ak.shape[2]
    nf = g.shape[0]
    nd = d // dt
    scale = 1.0 / n
    return pl.pallas_call(
        functools.partial(_hspec_kernel, scale=scale),
        out_shape=jax.ShapeDtypeStruct((nf, 2 * nb, d), BF16),
        grid=(nf, nd),
        in_specs=[pl.BlockSpec((1, 2 * nb, 2 * nb), lambda f, j: (f, 0, 0)),
                  pl.BlockSpec((2, 1, nb, dt), lambda f, j: (0, f, 0, j)),
                  pl.BlockSpec((2, 1, nb, dt), lambda f, j: (0, f, 0, nd + j))],
        out_specs=pl.BlockSpec((1, 2 * nb, dt), lambda f, j: (f, 0, j)),
        compiler_params=_cparams(("parallel", "parallel")),
        name="filter_spectrum",
    )(g, ak, ak)


def _xspec_kernel(g_ref, gi_ref, a_ref, h_ref, o_ref, *, nf):
    nb2, dt = h_ref.shape[1], h_ref.shape[2]
    nb = nb2 // 2
    live = pl.program_id(0) < nf

    @pl.when(live)
    def _():
        x = jnp.dot(g_ref[0].astype(BF16), a_ref[...].reshape(nb2, dt), preferred_element_type=F32)
        h = h_ref[0].astype(F32)
        xr, xi, hr, hi = x[:nb], x[nb:], h[:nb], h[nb:]
        y = jnp.concatenate([xr * hr - xi * hi, xr * hi + xi * hr], axis=0).astype(BF16)
        b = jnp.dot(gi_ref[0].astype(BF16), y, preferred_element_type=F32)
        o_ref[...] = b.reshape(o_ref.shape).astype(o_ref.dtype)

    @pl.when(jnp.logical_not(live))
    def _():
        o_ref[...] = jnp.zeros_like(o_ref)


def spectrum_product(a, h, g, ginv, dt=2048):
    _, nfp, nb, d = a.shape
    nf = g.shape[0]
    slab = lambda f: jnp.minimum(f, nf - 1)
    gspec = pl.BlockSpec((1, 2 * nb, 2 * nb), lambda f, j: (slab(f), 0, 0))
    aspec = pl.BlockSpec((2, 1, nb, dt), lambda f, j: (0, f, 0, j))
    return pl.pallas_call(
        functools.partial(_xspec_kernel, nf=nf),
        out_shape=jax.ShapeDtypeStruct((2, nfp, nb, d), BF16),
        grid=(nfp, d // dt),
        in_specs=[gspec, gspec, aspec, pl.BlockSpec((1, 2 * nb, dt), lambda f, j: (slab(f), 0, j))],
        out_specs=aspec,
        compiler_params=_cparams(("parallel", "parallel")),
        name="spectrum_product",
    )(g, ginv, a, h)


def _idft2_kernel(f_ref, b_ref, vg_ref, x0_ref, skip_ref, o_ref):
    conv = jnp.dot(f_ref[...].astype(BF16), b_ref[...], preferred_element_type=F32)
    vg = vg_ref[...].astype(F32)
    o_ref[...] = ((conv + vg * skip_ref[...]) * x0_ref[...].astype(F32)).astype(o_ref.dtype)


def idft_stage2_mix(b2d, finv, vg2d, x02d, skip_t, tn=4096):
    k, cols = b2d.shape
    rows = finv.shape[0]
    return pl.pallas_call(
        _idft2_kernel,
        out_shape=jax.ShapeDtypeStruct((rows, cols), BF16),
        grid=(cols // tn,),
        in_specs=[pl.BlockSpec((rows, k), lambda c: (0, 0)),
                  pl.BlockSpec((k, tn), lambda c: (0, c)),
                  pl.BlockSpec((rows, tn), lambda c: (0, c)),
                  pl.BlockSpec((rows, tn), lambda c: (0, c)),
                  pl.BlockSpec((1, tn), lambda c: (0, 0))],
        out_specs=pl.BlockSpec((rows, tn), lambda c: (0, c)),
        compiler_params=_cparams(("parallel",)),
        name="idft_stage2_mix",
    )(finv, b2d, vg2d, x02d, skip_t)


def _outproj_slab_kernel(a_ref, x_ref, gate_ref, w_ref, o_ref, wb_ref):
    @pl.when(pl.program_id(1) == 0)
    def _():
        wb_ref[...] = w_ref[...].astype(BF16)

    s1, grp = x_ref.shape[0], x_ref.shape[1]
    k = w_ref.shape[0]
    a = jnp.concatenate([a_ref[:, j * k:(j + 1) * k] for j in range(grp)], axis=0)
    y = jnp.dot(a, wb_ref[...], preferred_element_type=F32)
    for j in range(grp):
        o_ref[:, j, :] = x_ref[:, j, :] + gate_ref[...] * y[j * s1:(j + 1) * s1]


def outproj_residual_slab(a2d, x, gate, w, tn=1024):
    t, d = x.shape
    k = w.shape[0]
    s1 = t // DFT_NB
    grp = HY_GROUP
    xspec = pl.BlockSpec((s1, grp, tn), lambda j, i: (0, i, j))
    out = pl.pallas_call(
        _outproj_slab_kernel,
        out_shape=jax.ShapeDtypeStruct((s1, DFT_NB, d), F32),
        grid=(d // tn, DFT_NB // grp),
        in_specs=[pl.BlockSpec((s1, grp * k), lambda j, i: (0, i)), xspec,
                  pl.BlockSpec((1, tn), lambda j, i: (0, j)),
                  pl.BlockSpec((k, tn), lambda j, i: (0, j))],
        out_specs=xspec,
        scratch_shapes=[pltpu.VMEM((k, tn), BF16)],
        compiler_params=_cparams(("arbitrary", "arbitrary")),
        name="outproj_residual_slab",
    )(a2d, x.reshape(s1, DFT_NB, d), gate, w)
    return out.reshape(t, d)


def hyena_mixer(hm, x, gate, w_in, layer, conv_w, conv_b, f_w0, f_b0, f_w1, f_b1, f_w2, f_b2, f_freq,
                f_wout, skip, w_out):
    seq, d = hm.shape
    nb = DFT_NB
    fs, g, ginv, finv = _dft_tables(seq)
    _, nfp = _dft_slabs(seq)
    p = matmul_cols(hm, w_in, layer, 3 * d)
    vg, x0 = hy_gate(p, conv_w, conv_b)
    kf = hy_filter(seq, f_w0, f_b0, f_w1, f_b1, f_w2, f_b2, f_freq, f_wout)
    ak = dft_stage1(kf, fs)
    h = filter_spectrum(ak.reshape(2, nfp, nb, 2 * d), g, d, 2 * seq)
    a = dft_stage1(vg, fs)
    b = spectrum_product(a.reshape(2, nfp, nb, d), h, g, ginv)
    tn = 2 * d
    skip_t = jnp.tile(skip.reshape(1, d), (1, tn // d))
    y = idft_stage2_mix(b.reshape(2 * nfp, nb * d), finv, vg, x0, skip_t, tn=tn)
    return outproj_residual_slab(y, x, gate, w_out)


def _qk_prep_kernel(m_ref, p_ref, n_ref, w_ref, b_ref, s_ref, o_ref):
    i = pl.program_id(0)
    nt = pl.num_programs(0)
    f = lambda r: r[...].astype(F32)
    u = _conv3(f(m_ref), f(p_ref), f(n_ref), w_ref[...], b_ref[...], i, nt)
    o_ref[...] = (u * jax.nn.sigmoid(u) * s_ref[...]).astype(o_ref.dtype)


def qk_prep(p, conv_w, conv_b, qk_dim, dqk, tm=1024, tc=1024):
    t = p.shape[0]
    c = 2 * qk_dim
    scale = np.ones((1, c), np.float32)
    scale[:, qk_dim:] = 1.0 / math.sqrt(dqk)
    return pl.pallas_call(
        _qk_prep_kernel,
        out_shape=jax.ShapeDtypeStruct((t, c), BF16),
        grid=(t // tm, c // tc),
        in_specs=_halo_specs(tm, tc, t, lambda j: j) + [
            pl.BlockSpec((SHORT_CONV, tc), lambda i, j: (0, j)),
            pl.BlockSpec((1, tc), lambda i, j: (0, j)),
            pl.BlockSpec((1, tc), lambda i, j: (0, j))],
        out_specs=pl.BlockSpec((tm, tc), lambda i, j: (i, j)),
        compiler_params=_cparams(("parallel", "parallel")),
        name="qk_prep",
    )(p, p, p, conv_w, conv_b.reshape(1, c), jnp.asarray(scale))


def _log_sigmoid(x):
    return jnp.minimum(x, 0.0) - jnp.log1p(jnp.exp(-jnp.abs(x)))


def _gates_kernel(hm_ref, w_ref, b_ref, col_ref, row_ref, scal_ref):
    hi = lax.Precision.HIGHEST
    nh = ML_HEADS
    hm = hm_ref[...]
    cs = hm.shape[0]
    gt = jnp.dot(hm, w_ref[...].astype(BF16), preferred_element_type=F32) + b_ref[...]
    gtt = gt.T
    r = lax.broadcasted_iota(jnp.int32, (cs, cs), 0)
    c = lax.broadcasted_iota(jnp.int32, (cs, cs), 1)
    lower = (r >= c).astype(F32)
    upper = (r <= c).astype(F32)
    i_f, f_f, i_b, f_b = (gt[:, k * nh:(k + 1) * nh] for k in range(4))
    lf_f, lf_b = _log_sigmoid(f_f), _log_sigmoid(f_b)
    b_f = jnp.dot(lower, lf_f, precision=hi, preferred_element_type=F32)
    b_b = jnp.dot(upper, lf_b, precision=hi, preferred_element_type=F32)
    g_f = jnp.sum(lf_f, axis=0, keepdims=True)
    g_b = jnp.sum(lf_b, axis=0, keepdims=True)
    a_f = g_f - b_f + i_f
    a_b = g_b - b_b + i_b
    col_ref[...] = jnp.concatenate([b_f, b_b, a_f, a_b], axis=1)
    scal_ref[0] = jnp.concatenate([g_f, g_b, jnp.max(a_f, axis=0, keepdims=True),
                                   jnp.max(a_b, axis=0, keepdims=True)], axis=1)
    i_ft, f_ft, i_bt, f_bt = (gtt[k * nh:(k + 1) * nh, :] for k in range(4))
    b_ft = jnp.dot(_log_sigmoid(f_ft), upper, precision=hi, preferred_element_type=F32)
    b_bt = jnp.dot(_log_sigmoid(f_bt), lower, precision=hi, preferred_element_type=F32)
    row_ref[...] = jnp.concatenate([b_ft, b_bt, i_ft, i_bt], axis=0)


def ml_gates(hm, w_g, gate_b, cs):
    t, d = hm.shape
    g4 = w_g.shape[1]
    nc = t // cs
    lanes = 128
    w_pad = jnp.pad(w_g, ((0, 0), (0, lanes - g4)))
    b_pad = jnp.pad(gate_b.reshape(1, g4), ((0, 0), (0, lanes - g4)))
    return pl.pallas_call(
        _gates_kernel,
        out_shape=(jax.ShapeDtypeStruct((t, g4), F32), jax.ShapeDtypeStruct((g4, t), F32),
                   jax.ShapeDtypeStruct((nc, 1, g4), F32)),
        grid=(nc,),
        in_specs=[pl.BlockSpec((cs, d), lambda c: (c, 0)),
                  pl.BlockSpec((d, lanes), lambda c: (0, 0)),
                  pl.BlockSpec((1, lanes), lambda c: (0, 0))],
        out_specs=(pl.BlockSpec((cs, g4), lambda c: (c, 0)),
                   pl.BlockSpec((g4, cs), lambda c: (0, c)),
                   pl.BlockSpec((1, 1, g4), lambda c: (c, 0, 0))),
        compiler_params=_cparams(("parallel",)),
        name="ml_gates",
    )(hm, w_pad, b_pad)


def _mlstm_direction(qk_ref, v_ref, col_ref, row_ref, sc_ref, o_ref, c_ref, n_ref, m_ref, base, causal,
                     dqk, dv):
    nh = ML_HEADS
    cs = qk_ref.shape[0]
    qkd = nh * dqk
    per_head = lambda f: jnp.stack([f(h) for h in range(nh)])
    q = per_head(lambda h: qk_ref[:, h * dqk:(h + 1) * dqk])
    k = per_head(lambda h: qk_ref[:, qkd + h * dqk:qkd + (h + 1) * dqk])
    v = per_head(lambda h: v_ref[:, h * dv:(h + 1) * dv])
    b_col = per_head(lambda h: col_ref[:, base + h:base + h + 1])
    a_col = per_head(lambda h: col_ref[:, 2 * nh + base + h:2 * nh + base + h + 1])
    b_row = per_head(lambda h: row_ref[base + h:base + h + 1, :])
    li_row = per_head(lambda h: row_ref[2 * nh + base + h:2 * nh + base + h + 1, :])
    g = per_head(lambda h: sc_ref[0, :, base + h:base + h + 1])
    m_loc = per_head(lambda h: sc_ref[0, :, 2 * nh + base + h:2 * nh + base + h + 1])
    c_st = c_ref[base:base + nh]
    n_st = n_ref[base:base + nh]
    m_st = m_ref[base:base + nh][:, :, 0:1]
    r = lax.broadcasted_iota(jnp.int32, (1, cs, cs), 1)
    s = lax.broadcasted_iota(jnp.int32, (1, cs, cs), 2)
    mask = (s <= r) if causal else (s >= r)
    bdot = lambda a, b, ca, cb: lax.dot_general(a, b, (((ca,), (cb,)), ((0,), (0,))),
                                                preferred_element_type=F32)
    dlog = jnp.where(mask, b_col - b_row + li_row, -jnp.inf)
    m_inter = b_col + m_st
    m_t = jnp.maximum(m_inter, jnp.max(dlog, axis=-1, keepdims=True))
    p = jnp.exp(dlog - m_t) * bdot(q, k, 2, 2)
    s_inter = jnp.exp(m_inter - m_t)
    num = s_inter * bdot(q, c_st.astype(BF16), 2, 1) + bdot(p.astype(BF16), v, 2, 1)
    den = (s_inter * jnp.sum(q.astype(F32) * n_st, axis=-1, keepdims=True)
           + jnp.sum(p, axis=-1, keepdims=True))
    hout = num / jnp.maximum(jnp.abs(den), jnp.exp(-m_t))
    for h in range(nh):
        o_ref[:, h * dv:(h + 1) * dv] = hout[h].astype(o_ref.dtype)
    kw = k.astype(F32) * jnp.exp(a_col - m_loc)
    kwb = kw.astype(BF16)
    c_loc = per_head(lambda h: lax.dot_general(kwb[h], v[h], (((0,), (0,)), ((), ())),
                                               preferred_element_type=F32))
    n_loc = jnp.sum(kw, axis=1, keepdims=True)
    m_new = jnp.maximum(g + m_st, m_loc)
    s_prev = jnp.exp(g + m_st - m_new)
    s_loc = jnp.exp(m_loc - m_new)
    c_ref[base:base + nh] = s_prev * c_st + s_loc * c_loc
    n_ref[base:base + nh] = s_prev * n_st + s_loc * n_loc
    m_ref[base:base + nh] = jnp.broadcast_to(m_new, (nh,) + m_ref.shape[1:])


def _mlstm_kernel(qkf_ref, qkb_ref, vf_ref, vb_ref, colf_ref, colb_ref, rowf_ref, rowb_ref,
                  scf_ref, scb_ref, of_ref, ob_ref, c_ref, n_ref, m_ref, *, dqk, dv):
    @pl.when(pl.program_id(0) == 0)
    def _():
        c_ref[...] = jnp.zeros_like(c_ref)
        n_ref[...] = jnp.zeros_like(n_ref)
        m_ref[...] = jnp.zeros_like(m_ref)

    _mlstm_direction(qkf_ref, vf_ref, colf_ref, rowf_ref, scf_ref, of_ref, c_ref, n_ref, m_ref,
                     0, True, dqk, dv)
    _mlstm_direction(qkb_ref, vb_ref, colb_ref, rowb_ref, scb_ref, ob_ref, c_ref, n_ref, m_ref,
                     ML_HEADS, False, dqk, dv)


def mlstm_bidir(qk, p, col, row, scal, cs, dqk, dv):
    t = qk.shape[0]
    nh = ML_HEADS
    nc = t // cs
    qkd2 = 2 * nh * dqk
    vd = nh * dv
    vblk = qkd2 // vd
    g4 = col.shape[1]
    fwd = lambda c: c
    bwd = lambda c: nc - 1 - c
    mk = lambda fn: dict(
        qk=pl.BlockSpec((cs, qkd2), lambda c: (fn(c), 0)),
        v=pl.BlockSpec((cs, vd), lambda c: (fn(c), vblk)),
        col=pl.BlockSpec((cs, g4), lambda c: (fn(c), 0)),
        row=pl.BlockSpec((g4, cs), lambda c: (0, fn(c))),
        sc=pl.BlockSpec((1, 1, g4), lambda c: (fn(c), 0, 0)),
        o=pl.BlockSpec((cs, vd), lambda c: (fn(c), 0)))
    sf, sb = mk(fwd), mk(bwd)
    return pl.pallas_call(
        functools.partial(_mlstm_kernel, dqk=dqk, dv=dv),
        out_shape=(jax.ShapeDtypeStruct((t, vd), BF16), jax.ShapeDtypeStruct((t, vd), BF16)),
        grid=(nc,),
        in_specs=[sf["qk"], sb["qk"], sf["v"], sb["v"], sf["col"], sb["col"], sf["row"], sb["row"],
                  sf["sc"], sb["sc"]],
        out_specs=(sf["o"], sb["o"]),
        scratch_shapes=[pltpu.VMEM((2 * nh, dqk, dv), F32), pltpu.VMEM((2 * nh, 1, dqk), F32),
                        pltpu.VMEM((2 * nh, 1, 128), F32)],
        compiler_params=_cparams(("arbitrary",)),
        name="mlstm_bidir",
    )(qk, qk, p, p, col, col, row, row, scal, scal)


def _ml_post_kernel(hf_ref, hb_ref, o_ref, hg_ref, out_ref, *, dv):
    nh = ML_HEADS
    for h in range(nh):
        sl = slice(h * dv, (h + 1) * dv)
        hs = hf_ref[:, sl].astype(F32) + hb_ref[:, sl].astype(F32)
        hs = hs * lax.rsqrt(jnp.mean(hs * hs, axis=-1, keepdims=True) + EPS)
        og = jax.nn.sigmoid(o_ref[:, sl].astype(F32))
        out_ref[:, sl] = (hs * hg_ref[:, sl] * og).astype(out_ref.dtype)


def ml_post(hf, hb, p, head_g, dv, tm=512):
    t, vd = hf.shape
    oblk = p.shape[1] // vd - 1
    spec = pl.BlockSpec((tm, vd), lambda i: (i, 0))
    return pl.pallas_call(
        functools.partial(_ml_post_kernel, dv=dv),
        out_shape=jax.ShapeDtypeStruct((t, vd), BF16),
        grid=(t // tm,),
        in_specs=[spec, spec, pl.BlockSpec((tm, vd), lambda i: (i, oblk)),
                  pl.BlockSpec((1, vd), lambda i: (0, 0))],
        out_specs=spec,
        compiler_params=_cparams(("parallel",)),
        name="ml_post",
    )(hf, hb, p, head_g.reshape(1, vd))


def mlstm_mixer(hm, x, gate, w_in, layer, conv_w, conv_b, gate_b, head_g, w_out):
    seq, d = hm.shape
    nh = ML_HEADS
    qk_dim = d // 2
    dqk = qk_dim // nh
    dv = d // nh
    n_main = 2 * qk_dim + 2 * d
    cs = min(ML_CHUNK, seq)
    p = matmul_cols(hm, w_in, layer, n_main)
    col, row, scal = ml_gates(hm, w_in[layer, :, n_main:], gate_b, cs)
    qk = qk_prep(p, conv_w, conv_b, qk_dim, dqk)
    hf, hb = mlstm_bidir(qk, p, col, row, scal, cs, dqk, dv)
    a = ml_post(hf, hb, p, head_g, dv)
    return outproj_residual(a, x, gate, w_out)


def _first_argmax(vals, axis, n):
    m = jnp.max(vals, axis=axis, keepdims=True)
    iota = lax.broadcasted_iota(jnp.int32, vals.shape, axis)
    idx = jnp.min(jnp.where(vals == m, iota, n), axis=axis, keepdims=True)
    return m, idx, iota


def _router_kernel(x_ref, g_ref, sc_ref, sh_ref, rwt_ref, rb_ref,
                   hf_ref, eidx_ref, rank_ref, wk_ref, cnt_ref, carry_ref):
    ne, ng = N_EXPERTS, N_GROUPS
    per = ne // ng
    tm = x_ref.shape[0]

    @pl.when(pl.program_id(0) == 0)
    def _():
        carry_ref[...] = jnp.zeros_like(carry_ref)

    hf = _normmod(x_ref[...], g_ref[...], sc_ref[...], sh_ref[...])
    hf_ref[...] = hf
    logits = lax.dot_general(rwt_ref[...], hf, (((1,), (1,)), ((), ())),
                             precision=lax.Precision.HIGHEST, preferred_element_type=F32)
    scores = jax.nn.sigmoid(logits)
    sel = scores + rb_ref[...]
    sel3 = sel.reshape(ng, per, tm)
    m1, i1, io3 = _first_argmax(sel3, 1, per)
    m2 = jnp.max(jnp.where(io3 == i1, -jnp.inf, sel3), axis=1, keepdims=True)
    gs = (m1 + m2).reshape(ng, tm)
    gsel = jnp.zeros((ng, tm), F32)
    for _ in range(TOPK_GROUPS):
        _, gi, iog = _first_argmax(gs, 0, ng)
        hit = iog == gi
        gsel = jnp.where(hit, 1.0, gsel)
        gs = jnp.where(hit, -jnp.inf, gs)
    gmask = jnp.broadcast_to(gsel.reshape(ng, 1, tm), (ng, per, tm)).reshape(ne, tm)
    cand = jnp.where(gmask > 0.5, sel, -jnp.inf)
    picked = []
    chosen = jnp.zeros((ne, tm), F32)
    for _ in range(TOP_K):
        _, ei, ioe = _first_argmax(cand, 0, ne)
        hit = ioe == ei
        picked.append((ei, hit))
        chosen = jnp.where(hit, 1.0, chosen)
        cand = jnp.where(hit, -jnp.inf, cand)
    r = lax.broadcasted_iota(jnp.int32, (tm, tm), 0)
    c = lax.broadcasted_iota(jnp.int32, (tm, tm), 1)
    before = (r < c).astype(BF16)
    ranks = jnp.dot(chosen.astype(BF16), before, preferred_element_type=F32) + carry_ref[:, 0:1]
    carry_ref[...] = carry_ref[...] + jnp.sum(chosen, axis=1, keepdims=True)
    cnt_ref[...] = carry_ref[...]
    wks = [jnp.sum(jnp.where(hit, scores, 0.0), axis=0, keepdims=True) for _, hit in picked]
    wsum = functools.reduce(lambda a, b: a + b, wks)
    eidx_ref[...] = jnp.zeros_like(eidx_ref)
    rank_ref[...] = jnp.zeros_like(rank_ref)
    wk_ref[...] = jnp.zeros_like(wk_ref)
    for j, ((ei, hit), wk) in enumerate(zip(picked, wks)):
        eidx_ref[j:j + 1, :] = ei
        rank_ref[j:j + 1, :] = jnp.sum(jnp.where(hit, ranks, 0.0), axis=0, keepdims=True).astype(jnp.int32)
        wk_ref[j:j + 1, :] = wk / wsum * ROUTED_SCALE


def moe_router(x, g, sc, sh, router_w, router_bias, tm=512):
    t, d = x.shape
    ne = N_EXPERTS
    vec = pl.BlockSpec((1, d), lambda i: (0, 0))
    lane = pl.BlockSpec((8, tm), lambda i: (0, i))
    return pl.pallas_call(
        _router_kernel,
        out_shape=(jax.ShapeDtypeStruct((t, d), F32), jax.ShapeDtypeStruct((8, t), jnp.int32),
                   jax.ShapeDtypeStruct((8, t), jnp.int32), jax.ShapeDtypeStruct((8, t), F32),
                   jax.ShapeDtypeStruct((ne, 128), F32)),
        grid=(t // tm,),
        in_specs=[pl.BlockSpec((tm, d), lambda i: (i, 0)), vec, vec, vec,
                  pl.BlockSpec((ne, d), lambda i: (0, 0)), pl.BlockSpec((ne, 1), lambda i: (0, 0))],
        out_specs=(pl.BlockSpec((tm, d), lambda i: (i, 0)), lane, lane, lane,
                   pl.BlockSpec((ne, 128), lambda i: (0, 0))),
        scratch_shapes=[pltpu.VMEM((ne, 128), F32)],
        compiler_params=_cparams(("arbitrary",)),
        name="moe_router",
    )(x, g, sc, sh, router_w.T, router_bias.reshape(ne, 1))


def _plan_kernel(eidx_ref, rank_ref, ps_ref, dest_ref):
    ne = N_EXPERTS
    tm = eidx_ref.shape[1]
    io = lax.broadcasted_iota(jnp.int32, (ne, tm), 0)
    ps = ps_ref[...]
    dest_ref[...] = jnp.zeros_like(dest_ref)
    for j in range(TOP_K):
        hit = io == eidx_ref[j:j + 1, :]
        base = jnp.sum(jnp.where(hit, ps, 0.0), axis=0, keepdims=True)
        dest_ref[j:j + 1, :] = base.astype(jnp.int32) + rank_ref[j:j + 1, :]


def moe_plan(eidx, rank, pad_start, tm=2048):
    t = eidx.shape[1]
    tm = min(tm, t)
    lane = pl.BlockSpec((8, tm), lambda i: (0, i))
    return pl.pallas_call(
        _plan_kernel,
        out_shape=jax.ShapeDtypeStruct((8, t), jnp.int32),
        grid=(t // tm,),
        in_specs=[lane, lane, pl.BlockSpec((N_EXPERTS, 1), lambda i: (0, 0))],
        out_specs=lane,
        compiler_params=_cparams(("parallel",)),
        name="moe_plan",
    )(eidx, rank, pad_start.astype(F32).reshape(N_EXPERTS, 1))


SUBLANES = 8
_PAD_PIECES = tuple(1 << k for k in reversed(range(3, MOE_BLOCK.bit_length() - 1)))


def _dispatch_kernel(dest_ref, zs_ref, zn_ref, hf_ref, xb_ref, zbuf, sem, zsem, *, t_total):
    tm = hf_ref.shape[0]
    base = pl.program_id(0) * tm

    def row_copy(t, j):
        d = dest_ref[j * t_total + base + t]
        return pltpu.make_async_copy(hf_ref.at[pl.ds(t, 1)], xb_ref.at[pl.ds(d, 1)], sem)

    def start(t, carry):
        for j in range(TOP_K):
            row_copy(t, j).start()
        return carry

    def wait_all():
        for _ in range(TOP_K):
            pltpu.make_async_copy(hf_ref, xb_ref.at[pl.ds(0, tm)], sem).wait()

    def fill(do_start):
        def body(e, carry):
            zs, zn = zs_ref[e], zn_ref[e]
            end = zs + zn

            def piece(src, dst):
                cp = pltpu.make_async_copy(src, dst, zsem)
                cp.start() if do_start else cp.wait()

            for r in range(SUBLANES - 1):
                @pl.when(r < (zn & (SUBLANES - 1)))
                def _():
                    piece(zbuf.at[pl.ds(0, 1)], xb_ref.at[pl.ds(zs + r, 1)])
            for p in _PAD_PIECES:
                @pl.when((zn & p) != 0)
                def _():
                    q = pl.multiple_of(end - (zn & ~(p - 1)), SUBLANES)
                    piece(zbuf.at[pl.ds(0, p)], xb_ref.at[pl.ds(q, p)])
            return carry
        lax.fori_loop(0, N_EXPERTS, body, 0)

    first = pl.program_id(0) == 0

    @pl.when(first)
    def _():
        zbuf[...] = jnp.zeros_like(zbuf)
        fill(True)

    lax.fori_loop(0, tm, start, 0)

    @pl.when(first)
    def _():
        fill(False)

    wait_all()


def moe_dispatch(dest, zero_start, zero_len, hf, n_slots, tm=256):
    t, d = hf.shape
    return pl.pallas_call(
        functools.partial(_dispatch_kernel, t_total=t),
        out_shape=jax.ShapeDtypeStruct((n_slots, d), hf.dtype),
        grid_spec=pltpu.PrefetchScalarGridSpec(
            num_scalar_prefetch=3,
            grid=(t // tm,),
            in_specs=[pl.BlockSpec((tm, d), lambda i, *_: (i, 0))],
            out_specs=pl.BlockSpec(memory_space=pl.ANY),
            scratch_shapes=[pltpu.VMEM((MOE_BLOCK // 2, d), hf.dtype),
                            pltpu.SemaphoreType.DMA(()), pltpu.SemaphoreType.DMA(())]),
        compiler_params=_cparams(("arbitrary",)),
        name="moe_dispatch",
    )(dest, zero_start, zero_len, hf)


def _ffn(x, wg, wu, wd):
    hg = jnp.dot(x, wg, preferred_element_type=F32)
    hu = jnp.dot(x, wu, preferred_element_type=F32)
    h = (hg * jax.nn.sigmoid(hg) * hu).astype(BF16)
    return jnp.dot(h, wd, preferred_element_type=F32)


def _expert_kernel(be_ref, na_ref, first_ref, next_ref, slot_ref, x_ref, wg_hbm, wu_hbm, wd_hbm, o_ref,
                   wgf, wuf, wdf, wgb, wub, wdb, sems, *, layer):
    b = pl.program_id(0)
    active = b < na_ref[0]

    def fetch(e, s):
        return (pltpu.make_async_copy(wg_hbm.at[layer, e], wgf.at[s], sems.at[s, 0]),
                pltpu.make_async_copy(wu_hbm.at[layer, e], wuf.at[s], sems.at[s, 1]),
                pltpu.make_async_copy(wd_hbm.at[layer, e], wdf.at[s], sems.at[s, 2]))

    @pl.when(jnp.logical_and(active, first_ref[b] == 1))
    def _():
        s = slot_ref[b]

        @pl.when(b == 0)
        def _():
            for cp in fetch(be_ref[b], s):
                cp.start()

        for cp in fetch(be_ref[b], s):
            cp.wait()
        nxt = next_ref[b]

        @pl.when(nxt >= 0)
        def _():
            for cp in fetch(nxt, 1 - s):
                cp.start()

        wgb[...] = wgf[s].astype(BF16)
        wub[...] = wuf[s].astype(BF16)
        wdb[...] = wdf[s].astype(BF16)

    @pl.when(active)
    def _():
        o_ref[...] = _ffn(x_ref[...].astype(BF16), wgb[...], wub[...], wdb[...])


def moe_experts(blk_e, n_active, first, nxt, slot, xb, w_gate, w_up, w_down, layer):
    p, d = xb.shape
    ff = w_gate.shape[3]
    nblk = p // MOE_BLOCK
    row = lambda b, be, na, *_: (jnp.minimum(b, na[0] - 1), 0)
    hbm = pl.BlockSpec(memory_space=pl.ANY)
    return pl.pallas_call(
        functools.partial(_expert_kernel, layer=layer),
        out_shape=jax.ShapeDtypeStruct((p, d), F32),
        grid_spec=pltpu.PrefetchScalarGridSpec(
            num_scalar_prefetch=5,
            grid=(nblk,),
            in_specs=[pl.BlockSpec((MOE_BLOCK, d), row), hbm, hbm, hbm],
            out_specs=pl.BlockSpec((MOE_BLOCK, d), row),
            scratch_shapes=[pltpu.VMEM((2, d, ff), F32), pltpu.VMEM((2, d, ff), F32),
                            pltpu.VMEM((2, ff, d), F32),
                            pltpu.VMEM((d, ff), BF16), pltpu.VMEM((d, ff), BF16),
                            pltpu.VMEM((ff, d), BF16), pltpu.SemaphoreType.DMA((2, 3))]),
        compiler_params=_cparams(("arbitrary",)),
        name="moe_experts",
    )(blk_e, n_active, first, nxt, slot, xb, w_gate, w_up, w_down)


def _combine_kernel(dest_ref, yb_ref, x_ref, hf_ref, wk_ref, gate_ref, wg_ref, wu_ref, wd_ref, fg_ref, o_ref,
                    buf, wgb, wub, wdb, sem, *, t_total, final_norm):
    tm, d = x_ref.shape
    base = pl.program_id(0) * tm

    @pl.when(pl.program_id(0) == 0)
    def _():
        wgb[...] = wg_ref[...].astype(BF16)
        wub[...] = wu_ref[...].astype(BF16)
        wdb[...] = wd_ref[...].astype(BF16)

    def row_copy(t, j):
        dst = dest_ref[j * t_total + base + t]
        return pltpu.make_async_copy(yb_ref.at[pl.ds(dst, 1)], buf.at[j, pl.ds(t, 1)], sem)

    def start(t, carry):
        for j in range(TOP_K):
            row_copy(t, j).start()
        return carry

    lax.fori_loop(0, tm, start, 0)
    acc = _ffn(hf_ref[...].astype(BF16), wgb[...], wub[...], wdb[...])
    for j in range(TOP_K):
        pltpu.make_async_copy(yb_ref.at[pl.ds(0, tm)], buf.at[j], sem).wait()
    for j in range(TOP_K):
        acc = acc + wk_ref[:, j:j + 1] * buf[j]
    y = x_ref[...] + gate_ref[...] * acc
    if final_norm:
        y = (y * lax.rsqrt(jnp.mean(y * y, axis=-1, keepdims=True) + EPS)) * fg_ref[...]
    o_ref[...] = y


def moe_combine(dest, yb, x, hf, wk_t, gate, sh_gate, sh_up, sh_down, final_g, final_norm, tm=128):
    t, d = x.shape
    ff = sh_gate.shape[1]
    tm = min(tm, t)
    tile = lambda i, dest: (i, 0)
    full = lambda shape: pl.BlockSpec(shape, lambda i, dest: (0, 0))
    return pl.pallas_call(
        functools.partial(_combine_kernel, t_total=t, final_norm=final_norm),
        out_shape=jax.ShapeDtypeStruct((t, d), F32),
        grid_spec=pltpu.PrefetchScalarGridSpec(
            num_scalar_prefetch=1,
            grid=(t // tm,),
            in_specs=[pl.BlockSpec(memory_space=pl.ANY),
                      pl.BlockSpec((tm, d), tile), pl.BlockSpec((tm, d), tile),
                      pl.BlockSpec((tm, 8), tile), full((1, d)),
                      full((d, ff)), full((d, ff)), full((ff, d)), full((1, d))],
            out_specs=pl.BlockSpec((tm, d), tile),
            scratch_shapes=[pltpu.VMEM((TOP_K, tm, d), F32),
                            pltpu.VMEM((d, ff), BF16), pltpu.VMEM((d, ff), BF16), pltpu.VMEM((ff, d), BF16),
                            pltpu.SemaphoreType.DMA(())]),
        compiler_params=_cparams(("arbitrary",)),
        name="moe_combine",
    )(dest, yb, x, hf, wk_t, gate, sh_gate, sh_up, sh_down, final_g)


def moe_layer(x, g, sc, sh, gate, router_w, router_bias, w_gate, w_up, w_down, sh_gate, sh_up, sh_down,
              layer, final_g, final_norm):
    t, d = x.shape
    ne = N_EXPERTS
    hf, eidx, rank, wk, cnt = moe_router(x, g, sc, sh, router_w, router_bias)
    counts = cnt[:, 0].astype(jnp.int32)
    padded = (counts + MOE_BLOCK - 1) // MOE_BLOCK * MOE_BLOCK
    pad_end = jnp.cumsum(padded)
    pad_start = pad_end - padded
    n_slots = (t * TOP_K + ne * (MOE_BLOCK - 1) + MOE_BLOCK - 1) // MOE_BLOCK * MOE_BLOCK
    nblk = n_slots // MOE_BLOCK
    dest = moe_plan(eidx, rank, pad_start)[:TOP_K].reshape(TOP_K * t)
    blk_start = jnp.arange(nblk, dtype=jnp.int32) * MOE_BLOCK
    n_active = (pad_end[-1] // MOE_BLOCK).astype(jnp.int32).reshape(1)
    blk_e = jnp.minimum(jnp.sum(blk_start[:, None] >= pad_end[None, :], axis=1), ne - 1).astype(jnp.int32)
    last_e = jnp.max(jnp.where(counts > 0, jnp.arange(ne, dtype=jnp.int32), 0))
    live = blk_start < pad_end[-1]
    blk_e = jnp.where(live, blk_e, last_e)
    first = jnp.logical_and(live, jnp.concatenate([jnp.ones((1,), bool), blk_e[1:] != blk_e[:-1]]))
    slot = ((jnp.cumsum(first.astype(jnp.int32)) - 1) % 2).astype(jnp.int32)
    ids = jnp.arange(ne, dtype=jnp.int32)
    later = jnp.logical_and(ids[None, :] > ids[:, None], counts[None, :] > 0)
    next_e = jnp.min(jnp.where(later, ids[None, :], ne), axis=1)
    next_e = jnp.where(next_e < ne, next_e, -1)
    nxt = jnp.sum(jnp.where(blk_e[:, None] == ids[None, :], next_e[None, :], 0), axis=1).astype(jnp.int32)

    xb = moe_dispatch(dest, pad_start + counts, padded - counts, hf, n_slots)
    yb = moe_experts(blk_e, n_active, first.astype(jnp.int32), nxt, slot, xb, w_gate, w_up, w_down, layer)
    return moe_combine(dest, yb, x, hf, wk.T, gate, sh_gate, sh_up, sh_down, final_g, final_norm)


def kernel(x, c, ada_w, ada_b, norm_mix_g, norm_ffn_g, hy_w_in, hy_conv_w, hy_conv_b, hy_f_w0, hy_f_b0, hy_f_w1, hy_f_b1, hy_f_w2, hy_f_b2, hy_f_freq, hy_f_wout, hy_skip, hy_w_out, ml_w_in, ml_conv_w, ml_conv_b, ml_gate_b, ml_head_g, ml_w_out, moe_router_w, moe_router_bias, moe_w_gate, moe_w_up, moe_w_down, sh_w_gate, sh_w_up, sh_w_down, final_g):
    bsz, seq, d = x.shape
    assert bsz == 1, "kernels are written for a single sequence"
    depth = ada_w.shape[0]
    xs = x.reshape(seq, d)
    for layer in range(depth):
        ada = ada_proj(c, ada_w, ada_b, layer)
        sh_m, sc_m, g_m, sh_f, sc_f, g_f = (ada[:, k * d:(k + 1) * d] for k in range(6))
        gm = norm_mix_g[layer].reshape(1, d)
        j = layer // 2
        if layer % 2 == 0:
            hm = normmod_slab(xs, gm, sc_m, sh_m)
            xs = hyena_mixer(hm, xs, g_m, hy_w_in, j, hy_conv_w[j], hy_conv_b[j], hy_f_w0[j], hy_f_b0[j],
                             hy_f_w1[j], hy_f_b1[j], hy_f_w2[j], hy_f_b2[j], hy_f_freq[j], hy_f_wout[j],
                             hy_skip[j], hy_w_out[j])
        else:
            hm = normmod(xs, gm, sc_m, sh_m)
            xs = mlstm_mixer(hm, xs, g_m, ml_w_in, j, ml_conv_w[j], ml_conv_b[j], ml_gate_b[j],
                             ml_head_g[j], ml_w_out[j])
        xs = moe_layer(xs, norm_ffn_g[layer].reshape(1, d), sc_f, sh_f, g_f, moe_router_w[layer],
                       moe_router_bias[layer], moe_w_gate, moe_w_up, moe_w_down,
                       sh_w_gate[layer], sh_w_up[layer], sh_w_down[layer], layer,
                       final_g.reshape(1, d), layer == depth - 1)
    return xs.reshape(bsz, seq, d)
```

```python
import functools
import math

import numpy as np
import jax
import jax.numpy as jnp
from jax import lax
from jax.experimental import pallas as pl
from jax.experimental.pallas import tpu as pltpu

F32 = jnp.float32
BF16 = jnp.bfloat16
EPS = 1e-6

V7X_VMEM_BYTES = 64 * 1024 * 1024
VMEM_LIMIT = V7X_VMEM_BYTES - 8 * 1024 * 1024

SHORT_CONV = 3
HY_BANDS = 16
HY_DECAY_TARGET = 1e-2
HY_FAST_DECAY = 0.3
HY_SLOW_DECAY = 1.5
ML_HEADS = 8
ML_CHUNK = 256
N_EXPERTS = 64
TOP_K = 6
N_GROUPS = 8
TOPK_GROUPS = 4
ROUTED_SCALE = 2.5
MOE_BLOCK = 256
DFT_NB = 128


def _cparams(sem):
    return pltpu.CompilerParams(dimension_semantics=sem, vmem_limit_bytes=VMEM_LIMIT)


def _ada_kernel(c_ref, w_ref, b_ref, o_ref, *, tc):
    rows, n = w_ref.shape[1], w_ref.shape[2]

    @pl.when(pl.program_id(0) == 0)
    def _():
        o_ref[...] = b_ref[0]

    c = c_ref[...]
    cs = c * jax.nn.sigmoid(c)
    for j in range(n // tc):
        part = (w_ref[0, :, j * tc:(j + 1) * tc] * cs).reshape(rows // 8, 8, tc).sum(axis=0)
        o_ref[:, j * tc:(j + 1) * tc] += jnp.sum(part, axis=0, keepdims=True)


def ada_proj(c, ada_w, ada_b, layer, rows=128):
    depth, d, n = ada_w.shape
    return pl.pallas_call(
        functools.partial(_ada_kernel, tc=2048),
        out_shape=jax.ShapeDtypeStruct((1, n), F32),
        grid=(d // rows,),
        in_specs=[pl.BlockSpec((rows, 1), lambda r: (r, 0)),
                  pl.BlockSpec((1, rows, n), lambda r: (layer, r, 0)),
                  pl.BlockSpec((1, 1, n), lambda r: (layer, 0, 0))],
        out_specs=pl.BlockSpec((1, n), lambda r: (0, 0)),
        compiler_params=_cparams(("arbitrary",)),
        name="ada_proj",
    )(c.reshape(d, 1), ada_w, ada_b.reshape(depth, 1, n))


def _normmod(x, g, sc, sh):
    r = lax.rsqrt(jnp.mean(x * x, axis=-1, keepdims=True) + EPS)
    return (x * r) * g * (1.0 + sc) + sh


def _normmod_kernel(x_ref, g_ref, sc_ref, sh_ref, o_ref):
    o_ref[...] = _normmod(x_ref[...], g_ref[...], sc_ref[...], sh_ref[...]).astype(o_ref.dtype)


def normmod(x, g, sc, sh, out_dtype=BF16, tm=512):
    t, d = x.shape
    vec = pl.BlockSpec((1, d), lambda i: (0, 0))
    return pl.pallas_call(
        _normmod_kernel,
        out_shape=jax.ShapeDtypeStruct((t, d), out_dtype),
        grid=(t // tm,),
        in_specs=[pl.BlockSpec((tm, d), lambda i: (i, 0)), vec, vec, vec],
        out_specs=pl.BlockSpec((tm, d), lambda i: (i, 0)),
        compiler_params=_cparams(("parallel",)),
        name="normmod",
    )(x, g, sc, sh)


def _mm_kernel(a_ref, w_ref, o_ref, wb_ref):
    @pl.when(pl.program_id(1) == 0)
    def _():
        wb_ref[...] = w_ref[0].astype(BF16)

    o_ref[...] = jnp.dot(a_ref[...], wb_ref[...], preferred_element_type=F32).astype(o_ref.dtype)


def matmul_cols(a, w, layer, n_out, out_dtype=BF16, tm=1024, tn=1024):
    m, k = a.shape
    tm = min(tm, m)
    return pl.pallas_call(
        _mm_kernel,
        out_shape=jax.ShapeDtypeStruct((m, n_out), out_dtype),
        grid=(n_out // tn, m // tm),
        in_specs=[pl.BlockSpec((tm, k), lambda j, i: (i, 0)),
                  pl.BlockSpec((1, k, tn), lambda j, i: (layer, 0, j))],
        out_specs=pl.BlockSpec((tm, tn), lambda j, i: (i, j)),
        scratch_shapes=[pltpu.VMEM((k, tn), BF16)],
        compiler_params=_cparams(("arbitrary", "arbitrary")),
        name="matmul_cols",
    )(a, w)


def _outproj_kernel(a_ref, x_ref, gate_ref, w_ref, o_ref, wb_ref):
    @pl.when(pl.program_id(1) == 0)
    def _():
        wb_ref[...] = w_ref[...].astype(BF16)

    y = jnp.dot(a_ref[...], wb_ref[...], preferred_element_type=F32)
    o_ref[...] = x_ref[...] + gate_ref[...] * y


def outproj_residual(a, x, gate, w, tm=1024, tn=1024):
    t, k = a.shape
    d = w.shape[1]
    tm = min(tm, t)
    return pl.pallas_call(
        _outproj_kernel,
        out_shape=jax.ShapeDtypeStruct((t, d), F32),
        grid=(d // tn, t // tm),
        in_specs=[pl.BlockSpec((tm, k), lambda j, i: (i, 0)),
                  pl.BlockSpec((tm, tn), lambda j, i: (i, j)),
                  pl.BlockSpec((1, tn), lambda j, i: (0, j)),
                  pl.BlockSpec((k, tn), lambda j, i: (0, j))],
        out_specs=pl.BlockSpec((tm, tn), lambda j, i: (i, j)),
        scratch_shapes=[pltpu.VMEM((k, tn), BF16)],
        compiler_params=_cparams(("arbitrary", "arbitrary")),
        name="outproj_residual",
    )(a, x, gate, w)


HALO = 16


def _conv3(main, prev, nxt, w, b, i, nt):
    tm = main.shape[0]
    rows = lax.broadcasted_iota(jnp.int32, main.shape, 0)
    pr = jnp.where(i > 0, prev[HALO - 1:HALO, :], 0.0)
    nx = jnp.where(i < nt - 1, nxt[0:1, :], 0.0)
    up = jnp.where(rows == 0, pr, pltpu.roll(main, 1, 0))
    dn = jnp.where(rows == tm - 1, nx, pltpu.roll(main, tm - 1, 0))
    return up * w[0:1, :] + main * w[1:2, :] + dn * w[2:3, :] + b


def _halo_specs(tm, tc, t, col_fn):
    nh = t // HALO
    r = tm // HALO
    return [
        pl.BlockSpec((tm, tc), lambda i, j: (i, col_fn(j))),
        pl.BlockSpec((HALO, tc), lambda i, j: (jnp.maximum(i * r - 1, 0), col_fn(j))),
        pl.BlockSpec((HALO, tc), lambda i, j: (jnp.minimum((i + 1) * r, nh - 1), col_fn(j))),
    ]


HY_GROUP = 8


def _normmod_slab_kernel(x_ref, g_ref, sc_ref, sh_ref, perm_ref, o_ref):
    s1, grp, d = x_ref.shape
    x = x_ref[...].reshape(s1 * grp, d)
    y = _normmod(x, g_ref[...], sc_ref[...], sh_ref[...]).astype(BF16)
    o_ref[...] = jnp.dot(perm_ref[...], y, preferred_element_type=F32).astype(o_ref.dtype)


def normmod_slab(x, g, sc, sh):
    t, d = x.shape
    s1 = t // DFT_NB
    grp = HY_GROUP
    n = s1 * grp
    perm = np.zeros((n, n), np.float32)
    src = np.arange(n)
    perm[(src % grp) * s1 + src // grp, src] = 1.0
    vec = pl.BlockSpec((1, d), lambda i: (0, 0))
    return pl.pallas_call(
        _normmod_slab_kernel,
        out_shape=jax.ShapeDtypeStruct((t, d), BF16),
        grid=(DFT_NB // grp,),
        in_specs=[pl.BlockSpec((s1, grp, d), lambda i: (0, i, 0)), vec, vec, vec,
                  pl.BlockSpec((n, n), lambda i: (0, 0))],
        out_specs=pl.BlockSpec((n, d), lambda i: (i, 0)),
        compiler_params=_cparams(("parallel",)),
        name="normmod_slab",
    )(x.reshape(s1, DFT_NB, d), g, sc, sh, jnp.asarray(perm).astype(BF16))


def _shift_rows(x, down):
    n = x.shape[0]
    rows = lax.broadcasted_iota(jnp.int32, x.shape, 0)
    if down:
        return jnp.where(rows == 0, 0.0, pltpu.roll(x, 1, 0))
    return jnp.where(rows == n - 1, 0.0, pltpu.roll(x, n - 1, 0))


def _hy_gate_kernel(*refs):
    mains, prevs, nexts = refs[0:3], refs[3:6], refs[6:9]
    ws, bs = refs[9:12], refs[12:15]
    vg_ref, x0_ref = refs[15:17]
    i = pl.program_id(0)
    nt = pl.num_programs(0)
    s1 = prevs[0].shape[0]
    d = mains[0].shape[1]
    grp = mains[0].shape[0] // s1

    def halo(ref, edge, down):
        h = ref[...].astype(F32)
        return jnp.where(edge, _shift_rows(h, down), h)

    ups = [halo(prevs[g], i == 0, True) for g in range(3)]
    dns = [halo(nexts[g], i == nt - 1, False) for g in range(3)]
    for j in range(grp):
        out = []
        for g in range(3):
            slab = lambda k: mains[g][k * s1:(k + 1) * s1, :].astype(F32)
            up = ups[g] if j == 0 else slab(j - 1)
            dn = dns[g] if j == grp - 1 else slab(j + 1)
            w = ws[g][...]
            out.append(up * w[0:1, :] + slab(j) * w[1:2, :] + dn * w[2:3, :] + bs[g][...])
        x0, x1, v = out
        vg_ref[:, j * d:(j + 1) * d] = (v * x1).astype(vg_ref.dtype)
        x0_ref[:, j * d:(j + 1) * d] = x0.astype(x0_ref.dtype)


def hy_gate(p, conv_w, conv_b):
    t, c3 = p.shape
    d = c3 // 3
    s1 = t // DFT_NB
    grp = HY_GROUP
    main = lambda g: pl.BlockSpec((grp * s1, d), lambda i: (i, g))
    prev = lambda g: pl.BlockSpec((s1, d), lambda i: ((i * grp + DFT_NB - 1) % DFT_NB, g))
    nxt = lambda g: pl.BlockSpec((s1, d), lambda i: (((i + 1) * grp) % DFT_NB, g))
    specs = [f(g) for f in (main, prev, nxt) for g in range(3)]
    specs += [pl.BlockSpec((SHORT_CONV, d), lambda i, g=g: (0, g)) for g in range(3)]
    specs += [pl.BlockSpec((1, d), lambda i, g=g: (0, g)) for g in range(3)]
    out_spec = pl.BlockSpec((s1, grp * d), lambda i: (0, i))
    cb = conv_b.reshape(1, c3)
    out = jax.ShapeDtypeStruct((s1, DFT_NB * d), BF16)
    return pl.pallas_call(
        _hy_gate_kernel,
        out_shape=(out, out),
        grid=(DFT_NB // grp,),
        in_specs=specs,
        out_specs=(out_spec, out_spec),
        compiler_params=_cparams(("parallel",)),
        name="hy_gate",
    )(*([p] * 9), conv_w, conv_w, conv_w, cb, cb, cb)


def _hy_features(seq):
    t = np.linspace(0.0, 1.0, seq, dtype=np.float64)[:, None]
    ang = 2.0 * math.pi * np.arange(seq, dtype=np.float64)[:, None] / seq
    bands = np.linspace(1e-4, HY_BANDS - 1, HY_BANDS, dtype=np.float64)
    z = np.concatenate([t, np.cos(ang * bands), -np.sin(ang * bands)], axis=-1)
    zp = np.zeros((seq, 128), np.float32)
    zp[:, :z.shape[1]] = z
    s1 = seq // DFT_NB
    return zp.reshape(s1, DFT_NB, 128).transpose(1, 0, 2).reshape(seq, 128)


def _hy_deltas(d):
    return np.abs(np.linspace(math.log(HY_DECAY_TARGET) / HY_FAST_DECAY,
                              math.log(HY_DECAY_TARGET) / HY_SLOW_DECAY, d,
                              dtype=np.float64)).astype(np.float32)[None, :]


def _hy_filter_kernel(z_ref, w0_ref, b0_ref, w1_ref, b1_ref, w2_ref, b2_ref, fr_ref, wo_ref,
                      dl_ref, o_ref, *, seq, grp):
    hi = lax.Precision.HIGHEST
    fr = fr_ref[...]
    a = jnp.sin(fr * (jnp.dot(z_ref[...], w0_ref[...], precision=hi, preferred_element_type=F32)
                      + b0_ref[...]))
    a = jnp.sin(fr * (jnp.dot(a, w1_ref[...], precision=hi, preferred_element_type=F32) + b1_ref[...]))
    a = jnp.sin(fr * (jnp.dot(a, w2_ref[...], precision=hi, preferred_element_type=F32) + b2_ref[...]))
    k = jnp.dot(a.astype(BF16), wo_ref[...].astype(BF16), preferred_element_type=F32)
    d2 = k.shape[1]
    d = d2 // 2
    s1 = k.shape[0] // grp
    for j in range(grp):
        s2 = pl.program_id(0) * grp + j
        tt = lax.broadcasted_iota(jnp.int32, (s1, 1), 0) * DFT_NB + s2
        window = jnp.exp(-(tt.astype(F32) * (1.0 / (seq - 1))) * dl_ref[...])
        kj = k[j * s1:(j + 1) * s1]
        o_ref[:, j * d2:j * d2 + d] = (kj[:, :d] * window).astype(o_ref.dtype)
        o_ref[:, j * d2 + d:(j + 1) * d2] = jnp.where(tt == 0, 0.0, kj[:, d:] * window).astype(o_ref.dtype)


def hy_filter(seq, w0, b0, w1, b1, w2, b2, freq, wout, grp=4):
    fw = w1.shape[0]
    d2 = wout.shape[1]
    s1 = seq // DFT_NB
    z = jnp.asarray(_hy_features(seq))
    w0p = jnp.zeros((128, fw), F32).at[:w0.shape[0]].set(w0)
    full = lambda shape: pl.BlockSpec(shape, lambda i: (0,) * len(shape))
    r = lambda v: v.reshape(1, -1)
    return pl.pallas_call(
        functools.partial(_hy_filter_kernel, seq=seq, grp=grp),
        out_shape=jax.ShapeDtypeStruct((s1, DFT_NB * d2), BF16),
        grid=(DFT_NB // grp,),
        in_specs=[pl.BlockSpec((grp * s1, 128), lambda i: (i, 0)),
                  full((128, fw)), full((1, fw)), full((fw, fw)), full((1, fw)),
                  full((fw, fw)), full((1, fw)), full((1, fw)), full((fw, d2)), full((1, d2 // 2))],
        out_specs=pl.BlockSpec((s1, grp * d2), lambda i: (0, i)),
        compiler_params=_cparams(("parallel",)),
        name="hy_filter",
    )(z, w0p, r(b0), w1, r(b1), w2, r(b2), r(freq), wout, jnp.asarray(_hy_deltas(d2 // 2)))


DFT_ROW_ALIGN = 16


def _dft_slabs(seq):
    nf = (2 * seq // DFT_NB) // 2 + 1
    return nf, -(-nf // DFT_ROW_ALIGN) * DFT_ROW_ALIGN


@functools.lru_cache(maxsize=None)
def _dft_tables(seq):
    n = 2 * seq
    nb = DFT_NB
    na = n // nb
    nf, nfp = _dft_slabs(seq)
    s1 = np.arange(na // 2)
    f1 = np.arange(nf)
    ph = 2.0 * math.pi * np.outer(f1, s1) / na
    fs = np.zeros((2 * nfp, na // 2))
    fs[:nf] = np.cos(ph)
    fs[nfp:nfp + nf] = -np.sin(ph)
    f2 = np.arange(nb)
    s2 = np.arange(nb)
    freq = f1[:, None, None] + na * f2[None, :, None]
    th = 2.0 * math.pi * ((freq * s2[None, None, :]) % n) / n
    gr, gi = np.cos(th), -np.sin(th)
    g = np.concatenate([np.concatenate([gr, -gi], axis=2),
                        np.concatenate([gi, gr], axis=2)], axis=1)
    ginv = np.transpose(g, (0, 2, 1))
    t1 = np.arange(na // 2)
    ph2 = 2.0 * math.pi * np.outer(t1, f1) / na
    wgt = np.full((nf,), 2.0)
    wgt[0] = wgt[nf - 1] = 1.0
    finv = np.zeros((na // 2, 2 * nfp))
    finv[:, :nf] = np.cos(ph2) * wgt
    finv[:, nfp:nfp + nf] = -np.sin(ph2) * wgt
    return tuple(a.astype(np.float32) for a in (fs, g, ginv, finv))


def _dft1_kernel(f_ref, x_ref, o_ref):
    y = jnp.dot(f_ref[...].astype(BF16), x_ref[...], preferred_element_type=F32)
    o_ref[...] = y.reshape(o_ref.shape).astype(o_ref.dtype)


def dft_stage1(x2d, fs, tn=8192):
    k, cols = x2d.shape
    na = fs.shape[0] // 2
    return pl.pallas_call(
        _dft1_kernel,
        out_shape=jax.ShapeDtypeStruct((2, na, cols), BF16),
        grid=(cols // tn,),
        in_specs=[pl.BlockSpec((2 * na, k), lambda c: (0, 0)),
                  pl.BlockSpec((k, tn), lambda c: (0, c))],
        out_specs=pl.BlockSpec((2, na, tn), lambda c: (0, 0, c)),
        compiler_params=_cparams(("parallel",)),
        name="dft_stage1",
    )(fs, x2d)


def _hspec_kernel(g_ref, af_ref, ab_ref, o_ref, *, scale):
    nb2, dt = o_ref.shape[1], o_ref.shape[2]
    g = g_ref[0].astype(BF16)
    hf = jnp.dot(g, af_ref[...].reshape(nb2, dt), preferred_element_type=F32)
    hb = jnp.dot(g, ab_ref[...].reshape(nb2, dt), preferred_element_type=F32)
    nb = nb2 // 2
    o_ref[0, :nb, :] = ((hf[:nb] + hb[:nb]) * scale).astype(o_ref.dtype)
    o_ref[0, nb:, :] = ((hf[nb:] - hb[nb:]) * scale).astype(o_ref.dtype)


def filter_spectrum(ak, g, d, n, dt=2048):
    nb = ak.shape[2]
    nf = g.shape[0]
    nd = d // dt
    scale = 1.0 / n
    return pl.pallas_call(
        functools.partial(_hspec_kernel, scale=scale),
        out_shape=jax.ShapeDtypeStruct((nf, 2 * nb, d), BF16),
        grid=(nf, nd),
        in_specs=[pl.BlockSpec((1, 2 * nb, 2 * nb), lambda f, j: (f, 0, 0)),
                  pl.BlockSpec((2, 1, nb, dt), lambda f, j: (0, f, 0, j)),
                  pl.BlockSpec((2, 1, nb, dt), lambda f, j: (0, f, 0, nd + j))],
        out_specs=pl.BlockSpec((1, 2 * nb, dt), lambda f, j: (f, 0, j)),
        compiler_params=_cparams(("parallel", "parallel")),
        name="filter_spectrum",
    )(g, ak, ak)


def _xspec_kernel(g_ref, gi_ref, a_ref, h_ref, o_ref, *, nf):
    nb2, dt = h_ref.shape[1], h_ref.shape[2]
    nb = nb2 // 2
    live = pl.program_id(0) < nf

    @pl.when(live)
    def _():
        x = jnp.dot(g_ref[0].astype(BF16), a_ref[...].reshape(nb2, dt), preferred_element_type=F32)
        h = h_ref[0].astype(F32)
        xr, xi, hr, hi = x[:nb], x[nb:], h[:nb], h[nb:]
        y = jnp.concatenate([xr * hr - xi * hi, xr * hi + xi * hr], axis=0).astype(BF16)
        b = jnp.dot(gi_ref[0].astype(BF16), y, preferred_element_type=F32)
        o_ref[...] = b.reshape(o_ref.shape).astype(o_ref.dtype)

    @pl.when(jnp.logical_not(live))
    def _():
        o_ref[...] = jnp.zeros_like(o_ref)


def spectrum_product(a, h, g, ginv, dt=2048):
    _, nfp, nb, d = a.shape
    nf = g.shape[0]
    slab = lambda f: jnp.minimum(f, nf - 1)
    gspec = pl.BlockSpec((1, 2 * nb, 2 * nb), lambda f, j: (slab(f), 0, 0))
    aspec = pl.BlockSpec((2, 1, nb, dt), lambda f, j: (0, f, 0, j))
    return pl.pallas_call(
        functools.partial(_xspec_kernel, nf=nf),
        out_shape=jax.ShapeDtypeStruct((2, nfp, nb, d), BF16),
        grid=(nfp, d // dt),
        in_specs=[gspec, gspec, aspec, pl.BlockSpec((1, 2 * nb, dt), lambda f, j: (slab(f), 0, j))],
        out_specs=aspec,
        compiler_params=_cparams(("parallel", "parallel")),
        name="spectrum_product",
    )(g, ginv, a, h)


def _idft2_kernel(f_ref, b_ref, vg_ref, x0_ref, skip_ref, o_ref):
    conv = jnp.dot(f_ref[...].astype(BF16), b_ref[...], preferred_element_type=F32)
    vg = vg_ref[...].astype(F32)
    o_ref[...] = ((conv + vg * skip_ref[...]) * x0_ref[...].astype(F32)).astype(o_ref.dtype)


def idft_stage2_mix(b2d, finv, vg2d, x02d, skip_t, tn=4096):
    k, cols = b2d.shape
    rows = finv.shape[0]
    return pl.pallas_call(
        _idft2_kernel,
        out_shape=jax.ShapeDtypeStruct((rows, cols), BF16),
        grid=(cols // tn,),
        in_specs=[pl.BlockSpec((rows, k), lambda c: (0, 0)),
                  pl.BlockSpec((k, tn), lambda c: (0, c)),
                  pl.BlockSpec((rows, tn), lambda c: (0, c)),
                  pl.BlockSpec((rows, tn), lambda c: (0, c)),
                  pl.BlockSpec((1, tn), lambda c: (0, 0))],
        out_specs=pl.BlockSpec((rows, tn), lambda c: (0, c)),
        compiler_params=_cparams(("parallel",)),
        name="idft_stage2_mix",
    )(finv, b2d, vg2d, x02d, skip_t)


def _outproj_slab_kernel(a_ref, x_ref, gate_ref, w_ref, o_ref, wb_ref):
    @pl.when(pl.program_id(1) == 0)
    def _():
        wb_ref[...] = w_ref[...].astype(BF16)

    s1, grp = x_ref.shape[0], x_ref.shape[1]
    k = w_ref.shape[0]
    a = jnp.concatenate([a_ref[:, j * k:(j + 1) * k] for j in range(grp)], axis=0)
    y = jnp.dot(a, wb_ref[...], preferred_element_type=F32)
    for j in range(grp):
        o_ref[:, j, :] = x_ref[:, j, :] + gate_ref[...] * y[j * s1:(j + 1) * s1]


def outproj_residual_slab(a2d, x, gate, w, tn=1024):
    t, d = x.shape
    k = w.shape[0]
    s1 = t // DFT_NB
    grp = HY_GROUP
    xspec = pl.BlockSpec((s1, grp, tn), lambda j, i: (0, i, j))
    out = pl.pallas_call(
        _outproj_slab_kernel,
        out_shape=jax.ShapeDtypeStruct((s1, DFT_NB, d), F32),
        grid=(d // tn, DFT_NB // grp),
        in_specs=[pl.BlockSpec((s1, grp * k), lambda j, i: (0, i)), xspec,
                  pl.BlockSpec((1, tn), lambda j, i: (0, j)),
                  pl.BlockSpec((k, tn), lambda j, i: (0, j))],
        out_specs=xspec,
        scratch_shapes=[pltpu.VMEM((k, tn), BF16)],
        compiler_params=_cparams(("arbitrary", "arbitrary")),
        name="outproj_residual_slab",
    )(a2d, x.reshape(s1, DFT_NB, d), gate, w)
    return out.reshape(t, d)


def hyena_mixer(hm, x, gate, w_in, layer, conv_w, conv_b, f_w0, f_b0, f_w1, f_b1, f_w2, f_b2, f_freq,
                f_wout, skip, w_out):
    seq, d = hm.shape
    nb = DFT_NB
    fs, g, ginv, finv = _dft_tables(seq)
    _, nfp = _dft_slabs(seq)
    p = matmul_cols(hm, w_in, layer, 3 * d)
    vg, x0 = hy_gate(p, conv_w, conv_b)
    kf = hy_filter(seq, f_w0, f_b0, f_w1, f_b1, f_w2, f_b2, f_freq, f_wout)
    ak = dft_stage1(kf, fs)
    h = filter_spectrum(ak.reshape(2, nfp, nb, 2 * d), g, d, 2 * seq)
    a = dft_stage1(vg, fs)
    b = spectrum_product(a.reshape(2, nfp, nb, d), h, g, ginv)
    tn = 2 * d
    skip_t = jnp.tile(skip.reshape(1, d), (1, tn // d))
    y = idft_stage2_mix(b.reshape(2 * nfp, nb * d), finv, vg, x0, skip_t, tn=tn)
    return outproj_residual_slab(y, x, gate, w_out)


def _qk_prep_kernel(m_ref, p_ref, n_ref, w_ref, b_ref, s_ref, o_ref):
    i = pl.program_id(0)
    nt = pl.num_programs(0)
    f = lambda r: r[...].astype(F32)
    u = _conv3(f(m_ref), f(p_ref), f(n_ref), w_ref[...], b_ref[...], i, nt)
    o_ref[...] = (u * jax.nn.sigmoid(u) * s_ref[...]).astype(o_ref.dtype)


def qk_prep(p, conv_w, conv_b, qk_dim, dqk, tm=1024, tc=1024):
    t = p.shape[0]
    c = 2 * qk_dim
    scale = np.ones((1, c), np.float32)
    scale[:, qk_dim:] = 1.0 / math.sqrt(dqk)
    return pl.pallas_call(
        _qk_prep_kernel,
        out_shape=jax.ShapeDtypeStruct((t, c), BF16),
        grid=(t // tm, c // tc),
        in_specs=_halo_specs(tm, tc, t, lambda j: j) + [
            pl.BlockSpec((SHORT_CONV, tc), lambda i, j: (0, j)),
            pl.BlockSpec((1, tc), lambda i, j: (0, j)),
            pl.BlockSpec((1, tc), lambda i, j: (0, j))],
        out_specs=pl.BlockSpec((tm, tc), lambda i, j: (i, j)),
        compiler_params=_cparams(("parallel", "parallel")),
        name="qk_prep",
    )(p, p, p, conv_w, conv_b.reshape(1, c), jnp.asarray(scale))


def _log_sigmoid(x):
    return jnp.minimum(x, 0.0) - jnp.log1p(jnp.exp(-jnp.abs(x)))


def _gates_kernel(hm_ref, w_ref, b_ref, col_ref, row_ref, scal_ref):
    hi = lax.Precision.HIGHEST
    nh = ML_HEADS
    hm = hm_ref[...]
    cs = hm.shape[0]
    gt = jnp.dot(hm, w_ref[...].astype(BF16), preferred_element_type=F32) + b_ref[...]
    gtt = gt.T
    r = lax.broadcasted_iota(jnp.int32, (cs, cs), 0)
    c = lax.broadcasted_iota(jnp.int32, (cs, cs), 1)
    lower = (r >= c).astype(F32)
    upper = (r <= c).astype(F32)
    i_f, f_f, i_b, f_b = (gt[:, k * nh:(k + 1) * nh] for k in range(4))
    lf_f, lf_b = _log_sigmoid(f_f), _log_sigmoid(f_b)
    b_f = jnp.dot(lower, lf_f, precision=hi, preferred_element_type=F32)
    b_b = jnp.dot(upper, lf_b, precision=hi, preferred_element_type=F32)
    g_f = jnp.sum(lf_f, axis=0, keepdims=True)
    g_b = jnp.sum(lf_b, axis=0, keepdims=True)
    a_f = g_f - b_f + i_f
    a_b = g_b - b_b + i_b
    col_ref[...] = jnp.concatenate([b_f, b_b, a_f, a_b], axis=1)
    scal_ref[0] = jnp.concatenate([g_f, g_b, jnp.max(a_f, axis=0, keepdims=True),
                                   jnp.max(a_b, axis=0, keepdims=True)], axis=1)
    i_ft, f_ft, i_bt, f_bt = (gtt[k * nh:(k + 1) * nh, :] for k in range(4))
    b_ft = jnp.dot(_log_sigmoid(f_ft), upper, precision=hi, preferred_element_type=F32)
    b_bt = jnp.dot(_log_sigmoid(f_bt), lower, precision=hi, preferred_element_type=F32)
    row_ref[...] = jnp.concatenate([b_ft, b_bt, i_ft, i_bt], axis=0)


def ml_gates(hm, w_g, gate_b, cs):
    t, d = hm.shape
    g4 = w_g.shape[1]
    nc = t // cs
    lanes = 128
    w_pad = jnp.pad(w_g, ((0, 0), (0, lanes - g4)))
    b_pad = jnp.pad(gate_b.reshape(1, g4), ((0, 0), (0, lanes - g4)))
    return pl.pallas_call(
        _gates_kernel,
        out_shape=(jax.ShapeDtypeStruct((t, g4), F32), jax.ShapeDtypeStruct((g4, t), F32),
                   jax.ShapeDtypeStruct((nc, 1, g4), F32)),
        grid=(nc,),
        in_specs=[pl.BlockSpec((cs, d), lambda c: (c, 0)),
                  pl.BlockSpec((d, lanes), lambda c: (0, 0)),
                  pl.BlockSpec((1, lanes), lambda c: (0, 0))],
        out_specs=(pl.BlockSpec((cs, g4), lambda c: (c, 0)),
                   pl.BlockSpec((g4, cs), lambda c: (0, c)),
                   pl.BlockSpec((1, 1, g4), lambda c: (c, 0, 0))),
        compiler_params=_cparams(("parallel",)),
        name="ml_gates",
    )(hm, w_pad, b_pad)


def _mlstm_direction(qk_ref, v_ref, col_ref, row_ref, sc_ref, o_ref, c_ref, n_ref, m_ref, base, causal,
                     dqk, dv):
    nh = ML_HEADS
    cs = qk_ref.shape[0]
    qkd = nh * dqk
    per_head = lambda f: jnp.stack([f(h) for h in range(nh)])
    q = per_head(lambda h: qk_ref[:, h * dqk:(h + 1) * dqk])
    k = per_head(lambda h: qk_ref[:, qkd + h * dqk:qkd + (h + 1) * dqk])
    v = per_head(lambda h: v_ref[:, h * dv:(h + 1) * dv])
    b_col = per_head(lambda h: col_ref[:, base + h:base + h + 1])
    a_col = per_head(lambda h: col_ref[:, 2 * nh + base + h:2 * nh + base + h + 1])
    b_row = per_head(lambda h: row_ref[base + h:base + h + 1, :])
    li_row = per_head(lambda h: row_ref[2 * nh + base + h:2 * nh + base + h + 1, :])
    g = per_head(lambda h: sc_ref[0, :, base + h:base + h + 1])
    m_loc = per_head(lambda h: sc_ref[0, :, 2 * nh + base + h:2 * nh + base + h + 1])
    c_st = c_ref[base:base + nh]
    n_st = n_ref[base:base + nh]
    m_st = m_ref[base:base + nh][:, :, 0:1]
    r = lax.broadcasted_iota(jnp.int32, (1, cs, cs), 1)
    s = lax.broadcasted_iota(jnp.int32, (1, cs, cs), 2)
    mask = (s <= r) if causal else (s >= r)
    bdot = lambda a, b, ca, cb: lax.dot_general(a, b, (((ca,), (cb,)), ((0,), (0,))),
                                                preferred_element_type=F32)
    dlog = jnp.where(mask, b_col - b_row + li_row, -jnp.inf)
    m_inter = b_col + m_st
    m_t = jnp.maximum(m_inter, jnp.max(dlog, axis=-1, keepdims=True))
    p = jnp.exp(dlog - m_t) * bdot(q, k, 2, 2)
    s_inter = jnp.exp(m_inter - m_t)
    num = s_inter * bdot(q, c_st.astype(BF16), 2, 1) + bdot(p.astype(BF16), v, 2, 1)
    den = (s_inter * jnp.sum(q.astype(F32) * n_st, axis=-1, keepdims=True)
           + jnp.sum(p, axis=-1, keepdims=True))
    hout = num / jnp.maximum(jnp.abs(den), jnp.exp(-m_t))
    for h in range(nh):
        o_ref[:, h * dv:(h + 1) * dv] = hout[h].astype(o_ref.dtype)
    kw = k.astype(F32) * jnp.exp(a_col - m_loc)
    kwb = kw.astype(BF16)
    c_loc = per_head(lambda h: lax.dot_general(kwb[h], v[h], (((0,), (0,)), ((), ())),
                                               preferred_element_type=F32))
    n_loc = jnp.sum(kw, axis=1, keepdims=True)
    m_new = jnp.maximum(g + m_st, m_loc)
    s_prev = jnp.exp(g + m_st - m_new)
    s_loc = jnp.exp(m_loc - m_new)
    c_ref[base:base + nh] = s_prev * c_st + s_loc * c_loc
    n_ref[base:base + nh] = s_prev * n_st + s_loc * n_loc
    m_ref[base:base + nh] = jnp.broadcast_to(m_new, (nh,) + m_ref.shape[1:])


def _mlstm_kernel(qkf_ref, qkb_ref, vf_ref, vb_ref, colf_ref, colb_ref, rowf_ref, rowb_ref,
                  scf_ref, scb_ref, of_ref, ob_ref, c_ref, n_ref, m_ref, *, dqk, dv):
    @pl.when(pl.program_id(0) == 0)
    def _():
        c_ref[...] = jnp.zeros_like(c_ref)
        n_ref[...] = jnp.zeros_like(n_ref)
        m_ref[...] = jnp.zeros_like(m_ref)

    _mlstm_direction(qkf_ref, vf_ref, colf_ref, rowf_ref, scf_ref, of_ref, c_ref, n_ref, m_ref,
                     0, True, dqk, dv)
    _mlstm_direction(qkb_ref, vb_ref, colb_ref, rowb_ref, scb_ref, ob_ref, c_ref, n_ref, m_ref,
                     ML_HEADS, False, dqk, dv)


def mlstm_bidir(qk, p, col, row, scal, cs, dqk, dv):
    t = qk.shape[0]
    nh = ML_HEADS
    nc = t // cs
    qkd2 = 2 * nh * dqk
    vd = nh * dv
    vblk = qkd2 // vd
    g4 = col.shape[1]
    fwd = lambda c: c
    bwd = lambda c: nc - 1 - c
    mk = lambda fn: dict(
        qk=pl.BlockSpec((cs, qkd2), lambda c: (fn(c), 0)),
        v=pl.BlockSpec((cs, vd), lambda c: (fn(c), vblk)),
        col=pl.BlockSpec((cs, g4), lambda c: (fn(c), 0)),
        row=pl.BlockSpec((g4, cs), lambda c: (0, fn(c))),
        sc=pl.BlockSpec((1, 1, g4), lambda c: (fn(c), 0, 0)),
        o=pl.BlockSpec((cs, vd), lambda c: (fn(c), 0)))
    sf, sb = mk(fwd), mk(bwd)
    return pl.pallas_call(
        functools.partial(_mlstm_kernel, dqk=dqk, dv=dv),
        out_shape=(jax.ShapeDtypeStruct((t, vd), BF16), jax.ShapeDtypeStruct((t, vd), BF16)),
        grid=(nc,),
        in_specs=[sf["qk"], sb["qk"], sf["v"], sb["v"], sf["col"], sb["col"], sf["row"], sb["row"],
                  sf["sc"], sb["sc"]],
        out_specs=(sf["o"], sb["o"]),
        scratch_shapes=[pltpu.VMEM((2 * nh, dqk, dv), F32), pltpu.VMEM((2 * nh, 1, dqk), F32),
                        pltpu.VMEM((2 * nh, 1, 128), F32)],
        compiler_params=_cparams(("arbitrary",)),
        name="mlstm_bidir",
    )(qk, qk, p, p, col, col, row, row, scal, scal)


def _ml_post_kernel(hf_ref, hb_ref, o_ref, hg_ref, out_ref, *, dv):
    nh = ML_HEADS
    for h in range(nh):
        sl = slice(h * dv, (h + 1) * dv)
        hs = hf_ref[:, sl].astype(F32) + hb_ref[:, sl].astype(F32)
        hs = hs * lax.rsqrt(jnp.mean(hs * hs, axis=-1, keepdims=True) + EPS)
        og = jax.nn.sigmoid(o_ref[:, sl].astype(F32))
        out_ref[:, sl] = (hs * hg_ref[:, sl] * og).astype(out_ref.dtype)


def ml_post(hf, hb, p, head_g, dv, tm=512):
    t, vd = hf.shape
    oblk = p.shape[1] // vd - 1
    spec = pl.BlockSpec((tm, vd), lambda i: (i, 0))
    return pl.pallas_call(
        functools.partial(_ml_post_kernel, dv=dv),
        out_shape=jax.ShapeDtypeStruct((t, vd), BF16),
        grid=(t // tm,),
        in_specs=[spec, spec, pl.BlockSpec((tm, vd), lambda i: (i, oblk)),
                  pl.BlockSpec((1, vd), lambda i: (0, 0))],
        out_specs=spec,
        compiler_params=_cparams(("parallel",)),
        name="ml_post",
    )(hf, hb, p, head_g.reshape(1, vd))


def mlstm_mixer(hm, x, gate, w_in, layer, conv_w, conv_b, gate_b, head_g, w_out):
    seq, d = hm.shape
    nh = ML_HEADS
    qk_dim = d // 2
    dqk = qk_dim // nh
    dv = d // nh
    n_main = 2 * qk_dim + 2 * d
    cs = min(ML_CHUNK, seq)
    p = matmul_cols(hm, w_in, layer, n_main)
    col, row, scal = ml_gates(hm, w_in[layer, :, n_main:], gate_b, cs)
    qk = qk_prep(p, conv_w, conv_b, qk_dim, dqk)
    hf, hb = mlstm_bidir(qk, p, col, row, scal, cs, dqk, dv)
    a = ml_post(hf, hb, p, head_g, dv)
    return outproj_residual(a, x, gate, w_out)


def _first_argmax(vals, axis, n):
    m = jnp.max(vals, axis=axis, keepdims=True)
    iota = lax.broadcasted_iota(jnp.int32, vals.shape, axis)
    idx = jnp.min(jnp.where(vals == m, iota, n), axis=axis, keepdims=True)
    return m, idx, iota


def _router_kernel(x_ref, g_ref, sc_ref, sh_ref, rwt_ref, rb_ref,
                   hf_ref, eidx_ref, rank_ref, wk_ref, cnt_ref, carry_ref):
    ne, ng = N_EXPERTS, N_GROUPS
    per = ne // ng
    tm = x_ref.shape[0]

    @pl.when(pl.program_id(0) == 0)
    def _():
        carry_ref[...] = jnp.zeros_like(carry_ref)

    hf = _normmod(x_ref[...], g_ref[...], sc_ref[...], sh_ref[...])
    hf_ref[...] = hf
    logits = lax.dot_general(rwt_ref[...], hf, (((1,), (1,)), ((), ())),
                             precision=lax.Precision.HIGHEST, preferred_element_type=F32)
    scores = jax.nn.sigmoid(logits)
    sel = scores + rb_ref[...]
    sel3 = sel.reshape(ng, per, tm)
    m1, i1, io3 = _first_argmax(sel3, 1, per)
    m2 = jnp.max(jnp.where(io3 == i1, -jnp.inf, sel3), axis=1, keepdims=True)
    gs = (m1 + m2).reshape(ng, tm)
    gsel = jnp.zeros((ng, tm), F32)
    for _ in range(TOPK_GROUPS):
        _, gi, iog = _first_argmax(gs, 0, ng)
        hit = iog == gi
        gsel = jnp.where(hit, 1.0, gsel)
        gs = jnp.where(hit, -jnp.inf, gs)
    gmask = jnp.broadcast_to(gsel.reshape(ng, 1, tm), (ng, per, tm)).reshape(ne, tm)
    cand = jnp.where(gmask > 0.5, sel, -jnp.inf)
    picked = []
    chosen = jnp.zeros((ne, tm), F32)
    for _ in range(TOP_K):
        _, ei, ioe = _first_argmax(cand, 0, ne)
        hit = ioe == ei
        picked.append((ei, hit))
        chosen = jnp.where(hit, 1.0, chosen)
        cand = jnp.where(hit, -jnp.inf, cand)
    r = lax.broadcasted_iota(jnp.int32, (tm, tm), 0)
    c = lax.broadcasted_iota(jnp.int32, (tm, tm), 1)
    before = (r < c).astype(BF16)
    ranks = jnp.dot(chosen.astype(BF16), before, preferred_element_type=F32) + carry_ref[:, 0:1]
    carry_ref[...] = carry_ref[...] + jnp.sum(chosen, axis=1, keepdims=True)
    cnt_ref[...] = carry_ref[...]
    wks = [jnp.sum(jnp.where(hit, scores, 0.0), axis=0, keepdims=True) for _, hit in picked]
    wsum = functools.reduce(lambda a, b: a + b, wks)
    eidx_ref[...] = jnp.zeros_like(eidx_ref)
    rank_ref[...] = jnp.zeros_like(rank_ref)
    wk_ref[...] = jnp.zeros_like(wk_ref)
    for j, ((ei, hit), wk) in enumerate(zip(picked, wks)):
        eidx_ref[j:j + 1, :] = ei
        rank_ref[j:j + 1, :] = jnp.sum(jnp.where(hit, ranks, 0.0), axis=0, keepdims=True).astype(jnp.int32)
        wk_ref[j:j + 1, :] = wk / wsum * ROUTED_SCALE


def moe_router(x, g, sc, sh, router_w, router_bias, tm=512):
    t, d = x.shape
    ne = N_EXPERTS
    vec = pl.BlockSpec((1, d), lambda i: (0, 0))
    lane = pl.BlockSpec((8, tm), lambda i: (0, i))
    return pl.pallas_call(
        _router_kernel,
        out_shape=(jax.ShapeDtypeStruct((t, d), F32), jax.ShapeDtypeStruct((8, t), jnp.int32),
                   jax.ShapeDtypeStruct((8, t), jnp.int32), jax.ShapeDtypeStruct((8, t), F32),
                   jax.ShapeDtypeStruct((ne, 128), F32)),
        grid=(t // tm,),
        in_specs=[pl.BlockSpec((tm, d), lambda i: (i, 0)), vec, vec, vec,
                  pl.BlockSpec((ne, d), lambda i: (0, 0)), pl.BlockSpec((ne, 1), lambda i: (0, 0))],
        out_specs=(pl.BlockSpec((tm, d), lambda i: (i, 0)), lane, lane, lane,
                   pl.BlockSpec((ne, 128), lambda i: (0, 0))),
        scratch_shapes=[pltpu.VMEM((ne, 128), F32)],
        compiler_params=_cparams(("arbitrary",)),
        name="moe_router",
    )(x, g, sc, sh, router_w.T, router_bias.reshape(ne, 1))


def _plan_kernel(eidx_ref, rank_ref, ps_ref, dest_ref):
    ne = N_EXPERTS
    tm = eidx_ref.shape[1]
    io = lax.broadcasted_iota(jnp.int32, (ne, tm), 0)
    ps = ps_ref[...]
    dest_ref[...] = jnp.zeros_like(dest_ref)
    for j in range(TOP_K):
        hit = io == eidx_ref[j:j + 1, :]
        base = jnp.sum(jnp.where(hit, ps, 0.0), axis=0, keepdims=True)
        dest_ref[j:j + 1, :] = base.astype(jnp.int32) + rank_ref[j:j + 1, :]


def moe_plan(eidx, rank, pad_start, tm=2048):
    t = eidx.shape[1]
    tm = min(tm, t)
    lane = pl.BlockSpec((8, tm), lambda i: (0, i))
    return pl.pallas_call(
        _plan_kernel,
        out_shape=jax.ShapeDtypeStruct((8, t), jnp.int32),
        grid=(t // tm,),
        in_specs=[lane, lane, pl.BlockSpec((N_EXPERTS, 1), lambda i: (0, 0))],
        out_specs=lane,
        compiler_params=_cparams(("parallel",)),
        name="moe_plan",
    )(eidx, rank, pad_start.astype(F32).reshape(N_EXPERTS, 1))


SUBLANES = 8
_PAD_PIECES = tuple(1 << k for k in reversed(range(3, MOE_BLOCK.bit_length() - 1)))


def _dispatch_kernel(dest_ref, zs_ref, zn_ref, hf_ref, xb_ref, zbuf, sem, zsem, *, t_total):
    tm = hf_ref.shape[0]
    base = pl.program_id(0) * tm

    def row_copy(t, j):
        d = dest_ref[j * t_total + base + t]
        return pltpu.make_async_copy(hf_ref.at[pl.ds(t, 1)], xb_ref.at[pl.ds(d, 1)], sem)

    def start(t, carry):
        for j in range(TOP_K):
            row_copy(t, j).start(priority=j % 2)
        return carry

    def wait_all():
        for _ in range(TOP_K):
            pltpu.make_async_copy(hf_ref, xb_ref.at[pl.ds(0, tm)], sem).wait()

    def fill(do_start):
        def body(e, carry):
            zs, zn = zs_ref[e], zn_ref[e]
            end = zs + zn

            def piece(src, dst):
                cp = pltpu.make_async_copy(src, dst, zsem)
                cp.start() if do_start else cp.wait()

            for r in range(SUBLANES - 1):
                @pl.when(r < (zn & (SUBLANES - 1)))
                def _():
                    piece(zbuf.at[pl.ds(0, 1)], xb_ref.at[pl.ds(zs + r, 1)])
            for p in _PAD_PIECES:
                @pl.when((zn & p) != 0)
                def _():
                    q = pl.multiple_of(end - (zn & ~(p - 1)), SUBLANES)
                    piece(zbuf.at[pl.ds(0, p)], xb_ref.at[pl.ds(q, p)])
            return carry
        lax.fori_loop(0, N_EXPERTS, body, 0)

    first = pl.program_id(0) == 0

    @pl.when(first)
    def _():
        zbuf[...] = jnp.zeros_like(zbuf)
        fill(True)

    lax.fori_loop(0, tm, start, 0)

    @pl.when(first)
    def _():
        fill(False)

    wait_all()


def moe_dispatch(dest, zero_start, zero_len, hf, n_slots, tm=256):
    t, d = hf.shape
    return pl.pallas_call(
        functools.partial(_dispatch_kernel, t_total=t),
        out_shape=jax.ShapeDtypeStruct((n_slots, d), hf.dtype),
        grid_spec=pltpu.PrefetchScalarGridSpec(
            num_scalar_prefetch=3,
            grid=(t // tm,),
            in_specs=[pl.BlockSpec((tm, d), lambda i, *_: (i, 0))],
            out_specs=pl.BlockSpec(memory_space=pl.ANY),
            scratch_shapes=[pltpu.VMEM((MOE_BLOCK // 2, d), hf.dtype),
                            pltpu.SemaphoreType.DMA(()), pltpu.SemaphoreType.DMA(())]),
        compiler_params=_cparams(("arbitrary",)),
        name="moe_dispatch",
    )(dest, zero_start, zero_len, hf)


def _ffn(x, wg, wu, wd):
    hg = jnp.dot(x, wg, preferred_element_type=F32)
    hu = jnp.dot(x, wu, preferred_element_type=F32)
    h = (hg * jax.nn.sigmoid(hg) * hu).astype(BF16)
    return jnp.dot(h, wd, preferred_element_type=F32)


def _expert_kernel(be_ref, na_ref, first_ref, next_ref, slot_ref, x_ref, wg_hbm, wu_hbm, wd_hbm, o_ref,
                   wgf, wuf, wdf, wgb, wub, wdb, sems, *, layer):
    b = pl.program_id(0)
    active = b < na_ref[0]

    def fetch(e, s):
        return (pltpu.make_async_copy(wg_hbm.at[layer, e], wgf.at[s], sems.at[s, 0]),
                pltpu.make_async_copy(wu_hbm.at[layer, e], wuf.at[s], sems.at[s, 1]),
                pltpu.make_async_copy(wd_hbm.at[layer, e], wdf.at[s], sems.at[s, 2]))

    @pl.when(jnp.logical_and(active, first_ref[b] == 1))
    def _():
        s = slot_ref[b]

        @pl.when(b == 0)
        def _():
            for cp in fetch(be_ref[b], s):
                cp.start()

        for cp in fetch(be_ref[b], s):
            cp.wait()
        nxt = next_ref[b]

        @pl.when(nxt >= 0)
        def _():
            for cp in fetch(nxt, 1 - s):
                cp.start()

        wgb[...] = wgf[s].astype(BF16)
        wub[...] = wuf[s].astype(BF16)
        wdb[...] = wdf[s].astype(BF16)

    @pl.when(active)
    def _():
        o_ref[...] = _ffn(x_ref[...].astype(BF16), wgb[...], wub[...], wdb[...])


def moe_experts(blk_e, n_active, first, nxt, slot, xb, w_gate, w_up, w_down, layer):
    p, d = xb.shape
    ff = w_gate.shape[3]
    nblk = p // MOE_BLOCK
    row = lambda b, be, na, *_: (jnp.minimum(b, na[0] - 1), 0)
    hbm = pl.BlockSpec(memory_space=pl.ANY)
    return pl.pallas_call(
        functools.partial(_expert_kernel, layer=layer),
        out_shape=jax.ShapeDtypeStruct((p, d), F32),
        grid_spec=pltpu.PrefetchScalarGridSpec(
            num_scalar_prefetch=5,
            grid=(nblk,),
            in_specs=[pl.BlockSpec((MOE_BLOCK, d), row), hbm, hbm, hbm],
            out_specs=pl.BlockSpec((MOE_BLOCK, d), row),
            scratch_shapes=[pltpu.VMEM((2, d, ff), F32), pltpu.VMEM((2, d, ff), F32),
                            pltpu.VMEM((2, ff, d), F32),
                            pltpu.VMEM((d, ff), BF16), pltpu.VMEM((d, ff), BF16),
                            pltpu.VMEM((ff, d), BF16), pltpu.SemaphoreType.DMA((2, 3))]),
        compiler_params=_cparams(("arbitrary",)),
        name="moe_experts",
    )(blk_e, n_active, first, nxt, slot, xb, w_gate, w_up, w_down)


def _combine_kernel(dest_ref, yb_ref, x_ref, hf_ref, wk_ref, gate_ref, wg_ref, wu_ref, wd_ref, fg_ref, o_ref,
                    buf, wgb, wub, wdb, sem, *, t_total, final_norm):
    tm, d = x_ref.shape
    base = pl.program_id(0) * tm

    @pl.when(pl.program_id(0) == 0)
    def _():
        wgb[...] = wg_ref[...].astype(BF16)
        wub[...] = wu_ref[...].astype(BF16)
        wdb[...] = wd_ref[...].astype(BF16)

    def row_copy(t, j):
        dst = dest_ref[j * t_total + base + t]
        return pltpu.make_async_copy(yb_ref.at[pl.ds(dst, 1)], buf.at[j, pl.ds(t, 1)], sem)

    def start(t, carry):
        for j in range(TOP_K):
            row_copy(t, j).start(priority=j % 2)
        return carry

    lax.fori_loop(0, tm, start, 0)
    acc = _ffn(hf_ref[...].astype(BF16), wgb[...], wub[...], wdb[...])
    for j in range(TOP_K):
        pltpu.make_async_copy(yb_ref.at[pl.ds(0, tm)], buf.at[j], sem).wait()
    for j in range(TOP_K):
        acc = acc + wk_ref[:, j:j + 1] * buf[j]
    y = x_ref[...] + gate_ref[...] * acc
    if final_norm:
        y = (y * lax.rsqrt(jnp.mean(y * y, axis=-1, keepdims=True) + EPS)) * fg_ref[...]
    o_ref[...] = y


def moe_combine(dest, yb, x, hf, wk_t, gate, sh_gate, sh_up, sh_down, final_g, final_norm, tm=128):
    t, d = x.shape
    ff = sh_gate.shape[1]
    tm = min(tm, t)
    tile = lambda i, dest: (i, 0)
    full = lambda shape: pl.BlockSpec(shape, lambda i, dest: (0, 0))
    return pl.pallas_call(
        functools.partial(_combine_kernel, t_total=t, final_norm=final_norm),
        out_shape=jax.ShapeDtypeStruct((t, d), F32),
        grid_spec=pltpu.PrefetchScalarGridSpec(
            num_scalar_prefetch=1,
            grid=(t // tm,),
            in_specs=[pl.BlockSpec(memory_space=pl.ANY),
                      pl.BlockSpec((tm, d), tile), pl.BlockSpec((tm, d), tile),
                      pl.BlockSpec((tm, 8), tile), full((1, d)),
                      full((d, ff)), full((d, ff)), full((ff, d)), full((1, d))],
            out_specs=pl.BlockSpec((tm, d), tile),
            scratch_shapes=[pltpu.VMEM((TOP_K, tm, d), F32),
                            pltpu.VMEM((d, ff), BF16), pltpu.VMEM((d, ff), BF16), pltpu.VMEM((ff, d), BF16),
                            pltpu.SemaphoreType.DMA(())]),
        compiler_params=_cparams(("arbitrary",)),
        name="moe_combine",
    )(dest, yb, x, hf, wk_t, gate, sh_gate, sh_up, sh_down, final_g)


def moe_layer(x, g, sc, sh, gate, router_w, router_bias, w_gate, w_up, w_down, sh_gate, sh_up, sh_down,
              layer, final_g, final_norm):
    t, d = x.shape
    ne = N_EXPERTS
    hf, eidx, rank, wk, cnt = moe_router(x, g, sc, sh, router_w, router_bias)
    counts = cnt[:, 0].astype(jnp.int32)
    padded = (counts + MOE_BLOCK - 1) // MOE_BLOCK * MOE_BLOCK
    pad_end = jnp.cumsum(padded)
    pad_start = pad_end - padded
    n_slots = (t * TOP_K + ne * (MOE_BLOCK - 1) + MOE_BLOCK - 1) // MOE_BLOCK * MOE_BLOCK
    nblk = n_slots // MOE_BLOCK
    dest = moe_plan(eidx, rank, pad_start)[:TOP_K].reshape(TOP_K * t)
    blk_start = jnp.arange(nblk, dtype=jnp.int32) * MOE_BLOCK
    n_active = (pad_end[-1] // MOE_BLOCK).astype(jnp.int32).reshape(1)
    blk_e = jnp.minimum(jnp.sum(blk_start[:, None] >= pad_end[None, :], axis=1), ne - 1).astype(jnp.int32)
    last_e = jnp.max(jnp.where(counts > 0, jnp.arange(ne, dtype=jnp.int32), 0))
    live = blk_start < pad_end[-1]
    blk_e = jnp.where(live, blk_e, last_e)
    first = jnp.logical_and(live, jnp.concatenate([jnp.ones((1,), bool), blk_e[1:] != blk_e[:-1]]))
    slot = ((jnp.cumsum(first.astype(jnp.int32)) - 1) % 2).astype(jnp.int32)
    ids = jnp.arange(ne, dtype=jnp.int32)
    later = jnp.logical_and(ids[None, :] > ids[:, None], counts[None, :] > 0)
    next_e = jnp.min(jnp.where(later, ids[None, :], ne), axis=1)
    next_e = jnp.where(next_e < ne, next_e, -1)
    nxt = jnp.sum(jnp.where(blk_e[:, None] == ids[None, :], next_e[None, :], 0), axis=1).astype(jnp.int32)

    xb = moe_dispatch(dest, pad_start + counts, padded - counts, hf, n_slots)
    yb = moe_experts(blk_e, n_active, first.astype(jnp.int32), nxt, slot, xb, w_gate, w_up, w_down, layer)
    return moe_combine(dest, yb, x, hf, wk.T, gate, sh_gate, sh_up, sh_down, final_g, final_norm)


def kernel(x, c, ada_w, ada_b, norm_mix_g, norm_ffn_g, hy_w_in, hy_conv_w, hy_conv_b, hy_f_w0, hy_f_b0, hy_f_w1, hy_f_b1, hy_f_w2, hy_f_b2, hy_f_freq, hy_f_wout, hy_skip, hy_w_out, ml_w_in, ml_conv_w, ml_conv_b, ml_gate_b, ml_head_g, ml_w_out, moe_router_w, moe_router_bias, moe_w_gate, moe_w_up, moe_w_down, sh_w_gate, sh_w_up, sh_w_down, final_g):
    bsz, seq, d = x.shape
    assert bsz == 1, "kernels are written for a single sequence"
    depth = ada_w.shape[0]
    xs = x.reshape(seq, d)
    for layer in range(depth):
        ada = ada_proj(c, ada_w, ada_b, layer)
        sh_m, sc_m, g_m, sh_f, sc_f, g_f = (ada[:, k * d:(k + 1) * d] for k in range(6))
        gm = norm_mix_g[layer].reshape(1, d)
        j = layer // 2
        if layer % 2 == 0:
            hm = normmod_slab(xs, gm, sc_m, sh_m)
            xs = hyena_mixer(hm, xs, g_m, hy_w_in, j, hy_conv_w[j], hy_conv_b[j], hy_f_w0[j], hy_f_b0[j],
                             hy_f_w1[j], hy_f_b1[j], hy_f_w2[j], hy_f_b2[j], hy_f_freq[j], hy_f_wout[j],
                             hy_skip[j], hy_w_out[j])
        else:
            hm = normmod(xs, gm, sc_m, sh_m)
            xs = mlstm_mixer(hm, xs, g_m, ml_w_in, j, ml_conv_w[j], ml_conv_b[j], ml_gate_b[j],
                             ml_head_g[j], ml_w_out[j])
        xs = moe_layer(xs, norm_ffn_g[layer].reshape(1, d), sc_f, sh_f, g_f, moe_router_w[layer],
                       moe_router_bias[layer], moe_w_gate, moe_w_up, moe_w_down,
                       sh_w_gate[layer], sh_w_up[layer], sh_w_down[layer], layer,
                       final_g.reshape(1, d), layer == depth - 1)
    return xs.reshape(bsz, seq, d)
```

```python
import functools
import math

import numpy as np
import jax
import jax.numpy as jnp
from jax import lax
from jax.experimental import pallas as pl
from jax.experimental.pallas import tpu as pltpu

F32 = jnp.float32
BF16 = jnp.bfloat16
EPS = 1e-6

V7X_VMEM_BYTES = 64 * 1024 * 1024
VMEM_LIMIT = V7X_VMEM_BYTES - 8 * 1024 * 1024

SHORT_CONV = 3
HY_BANDS = 16
HY_DECAY_TARGET = 1e-2
HY_FAST_DECAY = 0.3
HY_SLOW_DECAY = 1.5
ML_HEADS = 8
ML_CHUNK = 256
N_EXPERTS = 64
TOP_K = 6
N_GROUPS = 8
TOPK_GROUPS = 4
ROUTED_SCALE = 2.5
MOE_BLOCK = 256
DFT_NB = 128


def _cparams(sem):
    return pltpu.CompilerParams(dimension_semantics=sem, vmem_limit_bytes=VMEM_LIMIT)


def _ada_kernel(c_ref, w_ref, b_ref, o_ref, *, tc):
    rows, n = w_ref.shape[1], w_ref.shape[2]

    @pl.when(pl.program_id(0) == 0)
    def _():
        o_ref[...] = b_ref[0]

    c = c_ref[...]
    cs = c * jax.nn.sigmoid(c)
    for j in range(n // tc):
        part = (w_ref[0, :, j * tc:(j + 1) * tc] * cs).reshape(rows // 8, 8, tc).sum(axis=0)
        o_ref[:, j * tc:(j + 1) * tc] += jnp.sum(part, axis=0, keepdims=True)


def ada_proj(c, ada_w, ada_b, layer, rows=128):
    depth, d, n = ada_w.shape
    return pl.pallas_call(
        functools.partial(_ada_kernel, tc=2048),
        out_shape=jax.ShapeDtypeStruct((1, n), F32),
        grid=(d // rows,),
        in_specs=[pl.BlockSpec((rows, 1), lambda r: (r, 0)),
                  pl.BlockSpec((1, rows, n), lambda r: (layer, r, 0)),
                  pl.BlockSpec((1, 1, n), lambda r: (layer, 0, 0))],
        out_specs=pl.BlockSpec((1, n), lambda r: (0, 0)),
        compiler_params=_cparams(("arbitrary",)),
        name="ada_proj",
    )(c.reshape(d, 1), ada_w, ada_b.reshape(depth, 1, n))


def _normmod(x, g, sc, sh):
    r = lax.rsqrt(jnp.mean(x * x, axis=-1, keepdims=True) + EPS)
    return (x * r) * g * (1.0 + sc) + sh


def _normmod_kernel(x_ref, g_ref, sc_ref, sh_ref, o_ref):
    o_ref[...] = _normmod(x_ref[...], g_ref[...], sc_ref[...], sh_ref[...]).astype(o_ref.dtype)


def normmod(x, g, sc, sh, out_dtype=BF16, tm=512):
    t, d = x.shape
    vec = pl.BlockSpec((1, d), lambda i: (0, 0))
    return pl.pallas_call(
        _normmod_kernel,
        out_shape=jax.ShapeDtypeStruct((t, d), out_dtype),
        grid=(t // tm,),
        in_specs=[pl.BlockSpec((tm, d), lambda i: (i, 0)), vec, vec, vec],
        out_specs=pl.BlockSpec((tm, d), lambda i: (i, 0)),
        compiler_params=_cparams(("parallel",)),
        name="normmod",
    )(x, g, sc, sh)


def _mm_kernel(a_ref, w_ref, o_ref, wb_ref):
    @pl.when(pl.program_id(1) == 0)
    def _():
        wb_ref[...] = w_ref[0].astype(BF16)

    o_ref[...] = jnp.dot(a_ref[...], wb_ref[...], preferred_element_type=F32).astype(o_ref.dtype)


def matmul_cols(a, w, layer, n_out, out_dtype=BF16, tm=1024, tn=1024):
    m, k = a.shape
    tm = min(tm, m)
    return pl.pallas_call(
        _mm_kernel,
        out_shape=jax.ShapeDtypeStruct((m, n_out), out_dtype),
        grid=(n_out // tn, m // tm),
        in_specs=[pl.BlockSpec((tm, k), lambda j, i: (i, 0)),
                  pl.BlockSpec((1, k, tn), lambda j, i: (layer, 0, j))],
        out_specs=pl.BlockSpec((tm, tn), lambda j, i: (i, j)),
        scratch_shapes=[pltpu.VMEM((k, tn), BF16)],
        compiler_params=_cparams(("arbitrary", "arbitrary")),
        name="matmul_cols",
    )(a, w)


def _outproj_kernel(a_ref, x_ref, gate_ref, w_ref, o_ref, wb_ref):
    @pl.when(pl.program_id(1) == 0)
    def _():
        wb_ref[...] = w_ref[...].astype(BF16)

    y = jnp.dot(a_ref[...], wb_ref[...], preferred_element_type=F32)
    o_ref[...] = x_ref[...] + gate_ref[...] * y


def outproj_residual(a, x, gate, w, tm=1024, tn=1024):
    t, k = a.shape
    d = w.shape[1]
    tm = min(tm, t)
    return pl.pallas_call(
        _outproj_kernel,
        out_shape=jax.ShapeDtypeStruct((t, d), F32),
        grid=(d // tn, t // tm),
        in_specs=[pl.BlockSpec((tm, k), lambda j, i: (i, 0)),
                  pl.BlockSpec((tm, tn), lambda j, i: (i, j)),
                  pl.BlockSpec((1, tn), lambda j, i: (0, j)),
                  pl.BlockSpec((k, tn), lambda j, i: (0, j))],
        out_specs=pl.BlockSpec((tm, tn), lambda j, i: (i, j)),
        scratch_shapes=[pltpu.VMEM((k, tn), BF16)],
        compiler_params=_cparams(("arbitrary", "arbitrary")),
        name="outproj_residual",
    )(a, x, gate, w)


HALO = 16


def _conv3(main, prev, nxt, w, b, i, nt):
    tm = main.shape[0]
    rows = lax.broadcasted_iota(jnp.int32, main.shape, 0)
    pr = jnp.where(i > 0, prev[HALO - 1:HALO, :], 0.0)
    nx = jnp.where(i < nt - 1, nxt[0:1, :], 0.0)
    up = jnp.where(rows == 0, pr, pltpu.roll(main, 1, 0))
    dn = jnp.where(rows == tm - 1, nx, pltpu.roll(main, tm - 1, 0))
    return up * w[0:1, :] + main * w[1:2, :] + dn * w[2:3, :] + b


def _halo_specs(tm, tc, t, col_fn):
    nh = t // HALO
    r = tm // HALO
    return [
        pl.BlockSpec((tm, tc), lambda i, j: (i, col_fn(j))),
        pl.BlockSpec((HALO, tc), lambda i, j: (jnp.maximum(i * r - 1, 0), col_fn(j))),
        pl.BlockSpec((HALO, tc), lambda i, j: (jnp.minimum((i + 1) * r, nh - 1), col_fn(j))),
    ]


HY_GROUP = 8


def _normmod_slab_kernel(x_ref, g_ref, sc_ref, sh_ref, perm_ref, o_ref):
    s1, grp, d = x_ref.shape
    x = x_ref[...].reshape(s1 * grp, d)
    y = _normmod(x, g_ref[...], sc_ref[...], sh_ref[...]).astype(BF16)
    o_ref[...] = jnp.dot(perm_ref[...], y, preferred_element_type=F32).astype(o_ref.dtype)


def normmod_slab(x, g, sc, sh):
    t, d = x.shape
    s1 = t // DFT_NB
    grp = HY_GROUP
    n = s1 * grp
    perm = np.zeros((n, n), np.float32)
    src = np.arange(n)
    perm[(src % grp) * s1 + src // grp, src] = 1.0
    vec = pl.BlockSpec((1, d), lambda i: (0, 0))
    return pl.pallas_call(
        _normmod_slab_kernel,
        out_shape=jax.ShapeDtypeStruct((t, d), BF16),
        grid=(DFT_NB // grp,),
        in_specs=[pl.BlockSpec((s1, grp, d), lambda i: (0, i, 0)), vec, vec, vec,
                  pl.BlockSpec((n, n), lambda i: (0, 0))],
        out_specs=pl.BlockSpec((n, d), lambda i: (i, 0)),
        compiler_params=_cparams(("parallel",)),
        name="normmod_slab",
    )(x.reshape(s1, DFT_NB, d), g, sc, sh, jnp.asarray(perm).astype(BF16))


def _shift_rows(x, down):
    n = x.shape[0]
    rows = lax.broadcasted_iota(jnp.int32, x.shape, 0)
    if down:
        return jnp.where(rows == 0, 0.0, pltpu.roll(x, 1, 0))
    return jnp.where(rows == n - 1, 0.0, pltpu.roll(x, n - 1, 0))


def _hy_gate_kernel(*refs):
    mains, prevs, nexts = refs[0:3], refs[3:6], refs[6:9]
    ws, bs = refs[9:12], refs[12:15]
    vg_ref, x0_ref = refs[15:17]
    i = pl.program_id(0)
    nt = pl.num_programs(0)
    s1 = prevs[0].shape[0]
    d = mains[0].shape[1]
    grp = mains[0].shape[0] // s1

    def halo(ref, edge, down):
        h = ref[...].astype(F32)
        return jnp.where(edge, _shift_rows(h, down), h)

    ups = [halo(prevs[g], i == 0, True) for g in range(3)]
    dns = [halo(nexts[g], i == nt - 1, False) for g in range(3)]
    for j in range(grp):
        out = []
        for g in range(3):
            slab = lambda k: mains[g][k * s1:(k + 1) * s1, :].astype(F32)
            up = ups[g] if j == 0 else slab(j - 1)
            dn = dns[g] if j == grp - 1 else slab(j + 1)
            w = ws[g][...]
            out.append(up * w[0:1, :] + slab(j) * w[1:2, :] + dn * w[2:3, :] + bs[g][...])
        x0, x1, v = out
        vg_ref[:, j * d:(j + 1) * d] = (v * x1).astype(vg_ref.dtype)
        x0_ref[:, j * d:(j + 1) * d] = x0.astype(x0_ref.dtype)


def hy_gate(p, conv_w, conv_b):
    t, c3 = p.shape
    d = c3 // 3
    s1 = t // DFT_NB
    grp = HY_GROUP
    main = lambda g: pl.BlockSpec((grp * s1, d), lambda i: (i, g))
    prev = lambda g: pl.BlockSpec((s1, d), lambda i: ((i * grp + DFT_NB - 1) % DFT_NB, g))
    nxt = lambda g: pl.BlockSpec((s1, d), lambda i: (((i + 1) * grp) % DFT_NB, g))
    specs = [f(g) for f in (main, prev, nxt) for g in range(3)]
    specs += [pl.BlockSpec((SHORT_CONV, d), lambda i, g=g: (0, g)) for g in range(3)]
    specs += [pl.BlockSpec((1, d), lambda i, g=g: (0, g)) for g in range(3)]
    out_spec = pl.BlockSpec((s1, grp * d), lambda i: (0, i))
    cb = conv_b.reshape(1, c3)
    out = jax.ShapeDtypeStruct((s1, DFT_NB * d), BF16)
    return pl.pallas_call(
        _hy_gate_kernel,
        out_shape=(out, out),
        grid=(DFT_NB // grp,),
        in_specs=specs,
        out_specs=(out_spec, out_spec),
        compiler_params=_cparams(("parallel",)),
        name="hy_gate",
    )(*([p] * 9), conv_w, conv_w, conv_w, cb, cb, cb)


def _hy_features(seq):
    t = np.linspace(0.0, 1.0, seq, dtype=np.float64)[:, None]
    ang = 2.0 * math.pi * np.arange(seq, dtype=np.float64)[:, None] / seq
    bands = np.linspace(1e-4, HY_BANDS - 1, HY_BANDS, dtype=np.float64)
    z = np.concatenate([t, np.cos(ang * bands), -np.sin(ang * bands)], axis=-1)
    zp = np.zeros((seq, 128), np.float32)
    zp[:, :z.shape[1]] = z
    s1 = seq // DFT_NB
    return zp.reshape(s1, DFT_NB, 128).transpose(1, 0, 2).reshape(seq, 128)


def _hy_deltas(d):
    return np.abs(np.linspace(math.log(HY_DECAY_TARGET) / HY_FAST_DECAY,
                              math.log(HY_DECAY_TARGET) / HY_SLOW_DECAY, d,
                              dtype=np.float64)).astype(np.float32)[None, :]


def _hy_filter_kernel(z_ref, w0_ref, b0_ref, w1_ref, b1_ref, w2_ref, b2_ref, fr_ref, wo_ref,
                      dl_ref, o_ref, *, seq, grp):
    hi = lax.Precision.HIGHEST
    fr = fr_ref[...]
    a = jnp.sin(fr * (jnp.dot(z_ref[...], w0_ref[...], precision=hi, preferred_element_type=F32)
                      + b0_ref[...]))
    a = jnp.sin(fr * (jnp.dot(a, w1_ref[...], precision=hi, preferred_element_type=F32) + b1_ref[...]))
    a = jnp.sin(fr * (jnp.dot(a, w2_ref[...], precision=hi, preferred_element_type=F32) + b2_ref[...]))
    k = jnp.dot(a.astype(BF16), wo_ref[...].astype(BF16), preferred_element_type=F32)
    d2 = k.shape[1]
    d = d2 // 2
    s1 = k.shape[0] // grp
    for j in range(grp):
        s2 = pl.program_id(0) * grp + j
        tt = lax.broadcasted_iota(jnp.int32, (s1, 1), 0) * DFT_NB + s2
        window = jnp.exp(-(tt.astype(F32) * (1.0 / (seq - 1))) * dl_ref[...])
        kj = k[j * s1:(j + 1) * s1]
        o_ref[:, j * d2:j * d2 + d] = (kj[:, :d] * window).astype(o_ref.dtype)
        o_ref[:, j * d2 + d:(j + 1) * d2] = jnp.where(tt == 0, 0.0, kj[:, d:] * window).astype(o_ref.dtype)


def hy_filter(seq, w0, b0, w1, b1, w2, b2, freq, wout, grp=4):
    fw = w1.shape[0]
    d2 = wout.shape[1]
    s1 = seq // DFT_NB
    z = jnp.asarray(_hy_features(seq))
    w0p = jnp.zeros((128, fw), F32).at[:w0.shape[0]].set(w0)
    full = lambda shape: pl.BlockSpec(shape, lambda i: (0,) * len(shape))
    r = lambda v: v.reshape(1, -1)
    return pl.pallas_call(
        functools.partial(_hy_filter_kernel, seq=seq, grp=grp),
        out_shape=jax.ShapeDtypeStruct((s1, DFT_NB * d2), BF16),
        grid=(DFT_NB // grp,),
        in_specs=[pl.BlockSpec((grp * s1, 128), lambda i: (i, 0)),
                  full((128, fw)), full((1, fw)), full((fw, fw)), full((1, fw)),
                  full((fw, fw)), full((1, fw)), full((1, fw)), full((fw, d2)), full((1, d2 // 2))],
        out_specs=pl.BlockSpec((s1, grp * d2), lambda i: (0, i)),
        compiler_params=_cparams(("parallel",)),
        name="hy_filter",
    )(z, w0p, r(b0), w1, r(b1), w2, r(b2), r(freq), wout, jnp.asarray(_hy_deltas(d2 // 2)))


DFT_ROW_ALIGN = 16


def _dft_slabs(seq):
    nf = (2 * seq // DFT_NB) // 2 + 1
    return nf, -(-nf // DFT_ROW_ALIGN) * DFT_ROW_ALIGN


@functools.lru_cache(maxsize=None)
def _dft_tables(seq):
    n = 2 * seq
    nb = DFT_NB
    na = n // nb
    nf, nfp = _dft_slabs(seq)
    s1 = np.arange(na // 2)
    f1 = np.arange(nf)
    ph = 2.0 * math.pi * np.outer(f1, s1) / na
    fs = np.zeros((2 * nfp, na // 2))
    fs[:nf] = np.cos(ph)
    fs[nfp:nfp + nf] = -np.sin(ph)
    f2 = np.arange(nb)
    s2 = np.arange(nb)
    freq = f1[:, None, None] + na * f2[None, :, None]
    th = 2.0 * math.pi * ((freq * s2[None, None, :]) % n) / n
    gr, gi = np.cos(th), -np.sin(th)
    g = np.concatenate([np.concatenate([gr, -gi], axis=2),
                        np.concatenate([gi, gr], axis=2)], axis=1)
    ginv = np.transpose(g, (0, 2, 1))
    t1 = np.arange(na // 2)
    ph2 = 2.0 * math.pi * np.outer(t1, f1) / na
    wgt = np.full((nf,), 2.0)
    wgt[0] = wgt[nf - 1] = 1.0
    finv = np.zeros((na // 2, 2 * nfp))
    finv[:, :nf] = np.cos(ph2) * wgt
    finv[:, nfp:nfp + nf] = -np.sin(ph2) * wgt
    return tuple(a.astype(np.float32) for a in (fs, g, ginv, finv))


def _dft1_kernel(f_ref, x_ref, o_ref):
    y = jnp.dot(f_ref[...].astype(BF16), x_ref[...], preferred_element_type=F32)
    o_ref[...] = y.reshape(o_ref.shape).astype(o_ref.dtype)


def dft_stage1(x2d, fs, tn=16384):
    k, cols = x2d.shape
    na = fs.shape[0] // 2
    return pl.pallas_call(
        _dft1_kernel,
        out_shape=jax.ShapeDtypeStruct((2, na, cols), BF16),
        grid=(cols // tn,),
        in_specs=[pl.BlockSpec((2 * na, k), lambda c: (0, 0)),
                  pl.BlockSpec((k, tn), lambda c: (0, c))],
        out_specs=pl.BlockSpec((2, na, tn), lambda c: (0, 0, c)),
        compiler_params=_cparams(("parallel",)),
        name="dft_stage1",
    )(fs, x2d)


def _hspec_kernel(g_ref, af_ref, ab_ref, o_ref, *, scale):
    nb2, dt = o_ref.shape[1], o_ref.shape[2]
    g = g_ref[0].astype(BF16)
    hf = jnp.dot(g, af_ref[...].reshape(nb2, dt), preferred_element_type=F32)
    hb = jnp.dot(g, ab_ref[...].reshape(nb2, dt), preferred_element_type=F32)
    nb = nb2 // 2
    o_ref[0, :nb, :] = ((hf[:nb] + hb[:nb]) * scale).astype(o_ref.dtype)
    o_ref[0, nb:, :] = ((hf[nb:] - hb[nb:]) * scale).astype(o_ref.dtype)


def filter_spectrum(ak, g, d, n, dt=2048):
    nb = ak.shape[2]
    nf = g.shape[0]
    nd = d // dt
    scale = 1.0 / n
    return pl.pallas_call(
        functools.partial(_hspec_kernel, scale=scale),
        out_shape=jax.ShapeDtypeStruct((nf, 2 * nb, d), BF16),
        grid=(nf, nd),
        in_specs=[pl.BlockSpec((1, 2 * nb, 2 * nb), lambda f, j: (f, 0, 0)),
                  pl.BlockSpec((2, 1, nb, dt), lambda f, j: (0, f, 0, j)),
                  pl.BlockSpec((2, 1, nb, dt), lambda f, j: (0, f, 0, nd + j))],
        out_specs=pl.BlockSpec((1, 2 * nb, dt), lambda f, j: (f, 0, j)),
        compiler_params=_cparams(("parallel", "parallel")),
        name="filter_spectrum",
    )(g, ak, ak)


def _xspec_kernel(g_ref, gi_ref, a_ref, h_ref, o_ref, *, nf):
    nb2, dt = h_ref.shape[1], h_ref.shape[2]
    nb = nb2 // 2
    live = pl.program_id(0) < nf

    @pl.when(live)
    def _():
        x = jnp.dot(g_ref[0].astype(BF16), a_ref[...].reshape(nb2, dt), preferred_element_type=F32)
        h = h_ref[0].astype(F32)
        xr, xi, hr, hi = x[:nb], x[nb:], h[:nb], h[nb:]
        y = jnp.concatenate([xr * hr - xi * hi, xr * hi + xi * hr], axis=0).astype(BF16)
        b = jnp.dot(gi_ref[0].astype(BF16), y, preferred_element_type=F32)
        o_ref[...] = b.reshape(o_ref.shape).astype(o_ref.dtype)

    @pl.when(jnp.logical_not(live))
    def _():
        o_ref[...] = jnp.zeros_like(o_ref)


def spectrum_product(a, h, g, ginv, dt=2048):
    _, nfp, nb, d = a.shape
    nf = g.shape[0]
    slab = lambda f: jnp.minimum(f, nf - 1)
    gspec = pl.BlockSpec((1, 2 * nb, 2 * nb), lambda f, j: (slab(f), 0, 0))
    aspec = pl.BlockSpec((2, 1, nb, dt), lambda f, j: (0, f, 0, j))
    return pl.pallas_call(
        functools.partial(_xspec_kernel, nf=nf),
        out_shape=jax.ShapeDtypeStruct((2, nfp, nb, d), BF16),
        grid=(nfp, d // dt),
        in_specs=[gspec, gspec, aspec, pl.BlockSpec((1, 2 * nb, dt), lambda f, j: (slab(f), 0, j))],
        out_specs=aspec,
        compiler_params=_cparams(("parallel", "parallel")),
        name="spectrum_product",
    )(g, ginv, a, h)


def _idft2_kernel(f_ref, b_ref, vg_ref, x0_ref, skip_ref, o_ref):
    conv = jnp.dot(f_ref[...].astype(BF16), b_ref[...], preferred_element_type=F32)
    vg = vg_ref[...].astype(F32)
    o_ref[...] = ((conv + vg * skip_ref[...]) * x0_ref[...].astype(F32)).astype(o_ref.dtype)


def idft_stage2_mix(b2d, finv, vg2d, x02d, skip_t, tn=4096):
    k, cols = b2d.shape
    rows = finv.shape[0]
    return pl.pallas_call(
        _idft2_kernel,
        out_shape=jax.ShapeDtypeStruct((rows, cols), BF16),
        grid=(cols // tn,),
        in_specs=[pl.BlockSpec((rows, k), lambda c: (0, 0)),
                  pl.BlockSpec((k, tn), lambda c: (0, c)),
                  pl.BlockSpec((rows, tn), lambda c: (0, c)),
                  pl.BlockSpec((rows, tn), lambda c: (0, c)),
                  pl.BlockSpec((1, tn), lambda c: (0, 0))],
        out_specs=pl.BlockSpec((rows, tn), lambda c: (0, c)),
        compiler_params=_cparams(("parallel",)),
        name="idft_stage2_mix",
    )(finv, b2d, vg2d, x02d, skip_t)


def _outproj_slab_kernel(a_ref, x_ref, gate_ref, w_ref, o_ref, wb_ref):
    @pl.when(pl.program_id(1) == 0)
    def _():
        wb_ref[...] = w_ref[...].astype(BF16)

    s1, grp = x_ref.shape[0], x_ref.shape[1]
    k = w_ref.shape[0]
    a = jnp.concatenate([a_ref[:, j * k:(j + 1) * k] for j in range(grp)], axis=0)
    y = jnp.dot(a, wb_ref[...], preferred_element_type=F32)
    for j in range(grp):
        o_ref[:, j, :] = x_ref[:, j, :] + gate_ref[...] * y[j * s1:(j + 1) * s1]


def outproj_residual_slab(a2d, x, gate, w, tn=1024):
    t, d = x.shape
    k = w.shape[0]
    s1 = t // DFT_NB
    grp = HY_GROUP
    xspec = pl.BlockSpec((s1, grp, tn), lambda j, i: (0, i, j))
    out = pl.pallas_call(
        _outproj_slab_kernel,
        out_shape=jax.ShapeDtypeStruct((s1, DFT_NB, d), F32),
        grid=(d // tn, DFT_NB // grp),
        in_specs=[pl.BlockSpec((s1, grp * k), lambda j, i: (0, i)), xspec,
                  pl.BlockSpec((1, tn), lambda j, i: (0, j)),
                  pl.BlockSpec((k, tn), lambda j, i: (0, j))],
        out_specs=xspec,
        scratch_shapes=[pltpu.VMEM((k, tn), BF16)],
        compiler_params=_cparams(("arbitrary", "arbitrary")),
        name="outproj_residual_slab",
    )(a2d, x.reshape(s1, DFT_NB, d), gate, w)
    return out.reshape(t, d)


def hyena_mixer(hm, x, gate, w_in, layer, conv_w, conv_b, f_w0, f_b0, f_w1, f_b1, f_w2, f_b2, f_freq,
                f_wout, skip, w_out):
    seq, d = hm.shape
    nb = DFT_NB
    fs, g, ginv, finv = _dft_tables(seq)
    _, nfp = _dft_slabs(seq)
    p = matmul_cols(hm, w_in, layer, 3 * d)
    vg, x0 = hy_gate(p, conv_w, conv_b)
    kf = hy_filter(seq, f_w0, f_b0, f_w1, f_b1, f_w2, f_b2, f_freq, f_wout)
    ak = dft_stage1(kf, fs)
    h = filter_spectrum(ak.reshape(2, nfp, nb, 2 * d), g, d, 2 * seq)
    a = dft_stage1(vg, fs)
    b = spectrum_product(a.reshape(2, nfp, nb, d), h, g, ginv)
    tn = 4 * d
    skip_t = jnp.tile(skip.reshape(1, d), (1, tn // d))
    y = idft_stage2_mix(b.reshape(2 * nfp, nb * d), finv, vg, x0, skip_t, tn=tn)
    return outproj_residual_slab(y, x, gate, w_out)


def _qk_prep_kernel(m_ref, p_ref, n_ref, w_ref, b_ref, s_ref, o_ref):
    i = pl.program_id(0)
    nt = pl.num_programs(0)
    f = lambda r: r[...].astype(F32)
    u = _conv3(f(m_ref), f(p_ref), f(n_ref), w_ref[...], b_ref[...], i, nt)
    o_ref[...] = (u * jax.nn.sigmoid(u) * s_ref[...]).astype(o_ref.dtype)


def qk_prep(p, conv_w, conv_b, qk_dim, dqk, tm=1024, tc=1024):
    t = p.shape[0]
    c = 2 * qk_dim
    scale = np.ones((1, c), np.float32)
    scale[:, qk_dim:] = 1.0 / math.sqrt(dqk)
    return pl.pallas_call(
        _qk_prep_kernel,
        out_shape=jax.ShapeDtypeStruct((t, c), BF16),
        grid=(t // tm, c // tc),
        in_specs=_halo_specs(tm, tc, t, lambda j: j) + [
            pl.BlockSpec((SHORT_CONV, tc), lambda i, j: (0, j)),
            pl.BlockSpec((1, tc), lambda i, j: (0, j)),
            pl.BlockSpec((1, tc), lambda i, j: (0, j))],
        out_specs=pl.BlockSpec((tm, tc), lambda i, j: (i, j)),
        compiler_params=_cparams(("parallel", "parallel")),
        name="qk_prep",
    )(p, p, p, conv_w, conv_b.reshape(1, c), jnp.asarray(scale))


def _log_sigmoid(x):
    return jnp.minimum(x, 0.0) - jnp.log1p(jnp.exp(-jnp.abs(x)))


def _gates_kernel(hm_ref, w_ref, b_ref, col_ref, row_ref, scal_ref):
    hi = lax.Precision.HIGHEST
    nh = ML_HEADS
    hm = hm_ref[...]
    cs = hm.shape[0]
    gt = jnp.dot(hm, w_ref[...].astype(BF16), preferred_element_type=F32) + b_ref[...]
    gtt = gt.T
    r = lax.broadcasted_iota(jnp.int32, (cs, cs), 0)
    c = lax.broadcasted_iota(jnp.int32, (cs, cs), 1)
    lower = (r >= c).astype(F32)
    upper = (r <= c).astype(F32)
    i_f, f_f, i_b, f_b = (gt[:, k * nh:(k + 1) * nh] for k in range(4))
    lf_f, lf_b = _log_sigmoid(f_f), _log_sigmoid(f_b)
    b_f = jnp.dot(lower, lf_f, precision=hi, preferred_element_type=F32)
    b_b = jnp.dot(upper, lf_b, precision=hi, preferred_element_type=F32)
    g_f = jnp.sum(lf_f, axis=0, keepdims=True)
    g_b = jnp.sum(lf_b, axis=0, keepdims=True)
    a_f = g_f - b_f + i_f
    a_b = g_b - b_b + i_b
    col_ref[...] = jnp.concatenate([b_f, b_b, a_f, a_b], axis=1)
    scal_ref[0] = jnp.concatenate([g_f, g_b, jnp.max(a_f, axis=0, keepdims=True),
                                   jnp.max(a_b, axis=0, keepdims=True)], axis=1)
    i_ft, f_ft, i_bt, f_bt = (gtt[k * nh:(k + 1) * nh, :] for k in range(4))
    b_ft = jnp.dot(_log_sigmoid(f_ft), upper, precision=hi, preferred_element_type=F32)
    b_bt = jnp.dot(_log_sigmoid(f_bt), lower, precision=hi, preferred_element_type=F32)
    row_ref[...] = jnp.concatenate([b_ft, b_bt, i_ft, i_bt], axis=0)


def ml_gates(hm, w_g, gate_b, cs):
    t, d = hm.shape
    g4 = w_g.shape[1]
    nc = t // cs
    lanes = 128
    w_pad = jnp.pad(w_g, ((0, 0), (0, lanes - g4)))
    b_pad = jnp.pad(gate_b.reshape(1, g4), ((0, 0), (0, lanes - g4)))
    return pl.pallas_call(
        _gates_kernel,
        out_shape=(jax.ShapeDtypeStruct((t, g4), F32), jax.ShapeDtypeStruct((g4, t), F32),
                   jax.ShapeDtypeStruct((nc, 1, g4), F32)),
        grid=(nc,),
        in_specs=[pl.BlockSpec((cs, d), lambda c: (c, 0)),
                  pl.BlockSpec((d, lanes), lambda c: (0, 0)),
                  pl.BlockSpec((1, lanes), lambda c: (0, 0))],
        out_specs=(pl.BlockSpec((cs, g4), lambda c: (c, 0)),
                   pl.BlockSpec((g4, cs), lambda c: (0, c)),
                   pl.BlockSpec((1, 1, g4), lambda c: (c, 0, 0))),
        compiler_params=_cparams(("parallel",)),
        name="ml_gates",
    )(hm, w_pad, b_pad)


def _mlstm_direction(qk_ref, v_ref, col_ref, row_ref, sc_ref, o_ref, c_ref, n_ref, m_ref, base, causal,
                     dqk, dv):
    nh = ML_HEADS
    cs = qk_ref.shape[0]
    qkd = nh * dqk
    per_head = lambda f: jnp.stack([f(h) for h in range(nh)])
    q = per_head(lambda h: qk_ref[:, h * dqk:(h + 1) * dqk])
    k = per_head(lambda h: qk_ref[:, qkd + h * dqk:qkd + (h + 1) * dqk])
    v = per_head(lambda h: v_ref[:, h * dv:(h + 1) * dv])
    b_col = per_head(lambda h: col_ref[:, base + h:base + h + 1])
    a_col = per_head(lambda h: col_ref[:, 2 * nh + base + h:2 * nh + base + h + 1])
    b_row = per_head(lambda h: row_ref[base + h:base + h + 1, :])
    li_row = per_head(lambda h: row_ref[2 * nh + base + h:2 * nh + base + h + 1, :])
    g = per_head(lambda h: sc_ref[0, :, base + h:base + h + 1])
    m_loc = per_head(lambda h: sc_ref[0, :, 2 * nh + base + h:2 * nh + base + h + 1])
    c_st = c_ref[base:base + nh]
    n_st = n_ref[base:base + nh]
    m_st = m_ref[base:base + nh][:, :, 0:1]
    r = lax.broadcasted_iota(jnp.int32, (1, cs, cs), 1)
    s = lax.broadcasted_iota(jnp.int32, (1, cs, cs), 2)
    mask = (s <= r) if causal else (s >= r)
    bdot = lambda a, b, ca, cb: lax.dot_general(a, b, (((ca,), (cb,)), ((0,), (0,))),
                                                preferred_element_type=F32)
    dlog = jnp.where(mask, b_col - b_row + li_row, -jnp.inf)
    m_inter = b_col + m_st
    m_t = jnp.maximum(m_inter, jnp.max(dlog, axis=-1, keepdims=True))
    p = jnp.exp(dlog - m_t) * bdot(q, k, 2, 2)
    s_inter = jnp.exp(m_inter - m_t)
    num = s_inter * bdot(q, c_st.astype(BF16), 2, 1) + bdot(p.astype(BF16), v, 2, 1)
    den = (s_inter * jnp.sum(q.astype(F32) * n_st, axis=-1, keepdims=True)
           + jnp.sum(p, axis=-1, keepdims=True))
    hout = num / jnp.maximum(jnp.abs(den), jnp.exp(-m_t))
    for h in range(nh):
        o_ref[:, h * dv:(h + 1) * dv] = hout[h].astype(o_ref.dtype)
    kw = k.astype(F32) * jnp.exp(a_col - m_loc)
    kwb = kw.astype(BF16)
    c_loc = per_head(lambda h: lax.dot_general(kwb[h], v[h], (((0,), (0,)), ((), ())),
                                               preferred_element_type=F32))
    n_loc = jnp.sum(kw, axis=1, keepdims=True)
    m_new = jnp.maximum(g + m_st, m_loc)
    s_prev = jnp.exp(g + m_st - m_new)
    s_loc = jnp.exp(m_loc - m_new)
    c_ref[base:base + nh] = s_prev * c_st + s_loc * c_loc
    n_ref[base:base + nh] = s_prev * n_st + s_loc * n_loc
    m_ref[base:base + nh] = jnp.broadcast_to(m_new, (nh,) + m_ref.shape[1:])


def _mlstm_kernel(qkf_ref, qkb_ref, vf_ref, vb_ref, colf_ref, colb_ref, rowf_ref, rowb_ref,
                  scf_ref, scb_ref, of_ref, ob_ref, c_ref, n_ref, m_ref, *, dqk, dv):
    @pl.when(pl.program_id(0) == 0)
    def _():
        c_ref[...] = jnp.zeros_like(c_ref)
        n_ref[...] = jnp.zeros_like(n_ref)
        m_ref[...] = jnp.zeros_like(m_ref)

    _mlstm_direction(qkf_ref, vf_ref, colf_ref, rowf_ref, scf_ref, of_ref, c_ref, n_ref, m_ref,
                     0, True, dqk, dv)
    _mlstm_direction(qkb_ref, vb_ref, colb_ref, rowb_ref, scb_ref, ob_ref, c_ref, n_ref, m_ref,
                     ML_HEADS, False, dqk, dv)


def mlstm_bidir(qk, p, col, row, scal, cs, dqk, dv):
    t = qk.shape[0]
    nh = ML_HEADS
    nc = t // cs
    qkd2 = 2 * nh * dqk
    vd = nh * dv
    vblk = qkd2 // vd
    g4 = col.shape[1]
    fwd = lambda c: c
    bwd = lambda c: nc - 1 - c
    mk = lambda fn: dict(
        qk=pl.BlockSpec((cs, qkd2), lambda c: (fn(c), 0)),
        v=pl.BlockSpec((cs, vd), lambda c: (fn(c), vblk)),
        col=pl.BlockSpec((cs, g4), lambda c: (fn(c), 0)),
        row=pl.BlockSpec((g4, cs), lambda c: (0, fn(c))),
        sc=pl.BlockSpec((1, 1, g4), lambda c: (fn(c), 0, 0)),
        o=pl.BlockSpec((cs, vd), lambda c: (fn(c), 0)))
    sf, sb = mk(fwd), mk(bwd)
    return pl.pallas_call(
        functools.partial(_mlstm_kernel, dqk=dqk, dv=dv),
        out_shape=(jax.ShapeDtypeStruct((t, vd), BF16), jax.ShapeDtypeStruct((t, vd), BF16)),
        grid=(nc,),
        in_specs=[sf["qk"], sb["qk"], sf["v"], sb["v"], sf["col"], sb["col"], sf["row"], sb["row"],
                  sf["sc"], sb["sc"]],
        out_specs=(sf["o"], sb["o"]),
        scratch_shapes=[pltpu.VMEM((2 * nh, dqk, dv), F32), pltpu.VMEM((2 * nh, 1, dqk), F32),
                        pltpu.VMEM((2 * nh, 1, 128), F32)],
        compiler_params=_cparams(("arbitrary",)),
        name="mlstm_bidir",
    )(qk, qk, p, p, col, col, row, row, scal, scal)


def _ml_post_kernel(hf_ref, hb_ref, o_ref, hg_ref, out_ref, *, dv):
    nh = ML_HEADS
    for h in range(nh):
        sl = slice(h * dv, (h + 1) * dv)
        hs = hf_ref[:, sl].astype(F32) + hb_ref[:, sl].astype(F32)
        hs = hs * lax.rsqrt(jnp.mean(hs * hs, axis=-1, keepdims=True) + EPS)
        og = jax.nn.sigmoid(o_ref[:, sl].astype(F32))
        out_ref[:, sl] = (hs * hg_ref[:, sl] * og).astype(out_ref.dtype)


def ml_post(hf, hb, p, head_g, dv, tm=512):
    t, vd = hf.shape
    oblk = p.shape[1] // vd - 1
    spec = pl.BlockSpec((tm, vd), lambda i: (i, 0))
    return pl.pallas_call(
        functools.partial(_ml_post_kernel, dv=dv),
        out_shape=jax.ShapeDtypeStruct((t, vd), BF16),
        grid=(t // tm,),
        in_specs=[spec, spec, pl.BlockSpec((tm, vd), lambda i: (i, oblk)),
                  pl.BlockSpec((1, vd), lambda i: (0, 0))],
        out_specs=spec,
        compiler_params=_cparams(("parallel",)),
        name="ml_post",
    )(hf, hb, p, head_g.reshape(1, vd))


def mlstm_mixer(hm, x, gate, w_in, layer, conv_w, conv_b, gate_b, head_g, w_out):
    seq, d = hm.shape
    nh = ML_HEADS
    qk_dim = d // 2
    dqk = qk_dim // nh
    dv = d // nh
    n_main = 2 * qk_dim + 2 * d
    cs = min(ML_CHUNK, seq)
    p = matmul_cols(hm, w_in, layer, n_main)
    col, row, scal = ml_gates(hm, w_in[layer, :, n_main:], gate_b, cs)
    qk = qk_prep(p, conv_w, conv_b, qk_dim, dqk)
    hf, hb = mlstm_bidir(qk, p, col, row, scal, cs, dqk, dv)
    a = ml_post(hf, hb, p, head_g, dv)
    return outproj_residual(a, x, gate, w_out)


def _first_argmax(vals, axis, n):
    m = jnp.max(vals, axis=axis, keepdims=True)
    iota = lax.broadcasted_iota(jnp.int32, vals.shape, axis)
    idx = jnp.min(jnp.where(vals == m, iota, n), axis=axis, keepdims=True)
    return m, idx, iota


def _router_kernel(x_ref, g_ref, sc_ref, sh_ref, rwt_ref, rb_ref,
                   hf_ref, eidx_ref, rank_ref, wk_ref, cnt_ref, carry_ref):
    ne, ng = N_EXPERTS, N_GROUPS
    per = ne // ng
    tm = x_ref.shape[0]

    @pl.when(pl.program_id(0) == 0)
    def _():
        carry_ref[...] = jnp.zeros_like(carry_ref)

    hf = _normmod(x_ref[...], g_ref[...], sc_ref[...], sh_ref[...])
    hf_ref[...] = hf
    logits = lax.dot_general(rwt_ref[...], hf, (((1,), (1,)), ((), ())),
                             precision=lax.Precision.HIGHEST, preferred_element_type=F32)
    scores = jax.nn.sigmoid(logits)
    sel = scores + rb_ref[...]
    sel3 = sel.reshape(ng, per, tm)
    m1, i1, io3 = _first_argmax(sel3, 1, per)
    m2 = jnp.max(jnp.where(io3 == i1, -jnp.inf, sel3), axis=1, keepdims=True)
    gs = (m1 + m2).reshape(ng, tm)
    gsel = jnp.zeros((ng, tm), F32)
    for _ in range(TOPK_GROUPS):
        _, gi, iog = _first_argmax(gs, 0, ng)
        hit = iog == gi
        gsel = jnp.where(hit, 1.0, gsel)
        gs = jnp.where(hit, -jnp.inf, gs)
    gmask = jnp.broadcast_to(gsel.reshape(ng, 1, tm), (ng, per, tm)).reshape(ne, tm)
    cand = jnp.where(gmask > 0.5, sel, -jnp.inf)
    picked = []
    chosen = jnp.zeros((ne, tm), F32)
    for _ in range(TOP_K):
        _, ei, ioe = _first_argmax(cand, 0, ne)
        hit = ioe == ei
        picked.append((ei, hit))
        chosen = jnp.where(hit, 1.0, chosen)
        cand = jnp.where(hit, -jnp.inf, cand)
    r = lax.broadcasted_iota(jnp.int32, (tm, tm), 0)
    c = lax.broadcasted_iota(jnp.int32, (tm, tm), 1)
    before = (r < c).astype(BF16)
    ranks = jnp.dot(chosen.astype(BF16), before, preferred_element_type=F32) + carry_ref[:, 0:1]
    carry_ref[...] = carry_ref[...] + jnp.sum(chosen, axis=1, keepdims=True)
    cnt_ref[...] = carry_ref[...]
    wks = [jnp.sum(jnp.where(hit, scores, 0.0), axis=0, keepdims=True) for _, hit in picked]
    wsum = functools.reduce(lambda a, b: a + b, wks)
    eidx_ref[...] = jnp.zeros_like(eidx_ref)
    rank_ref[...] = jnp.zeros_like(rank_ref)
    wk_ref[...] = jnp.zeros_like(wk_ref)
    for j, ((ei, hit), wk) in enumerate(zip(picked, wks)):
        eidx_ref[j:j + 1, :] = ei
        rank_ref[j:j + 1, :] = jnp.sum(jnp.where(hit, ranks, 0.0), axis=0, keepdims=True).astype(jnp.int32)
        wk_ref[j:j + 1, :] = wk / wsum * ROUTED_SCALE


def moe_router(x, g, sc, sh, router_w, router_bias, tm=512):
    t, d = x.shape
    ne = N_EXPERTS
    vec = pl.BlockSpec((1, d), lambda i: (0, 0))
    lane = pl.BlockSpec((8, tm), lambda i: (0, i))
    return pl.pallas_call(
        _router_kernel,
        out_shape=(jax.ShapeDtypeStruct((t, d), F32), jax.ShapeDtypeStruct((8, t), jnp.int32),
                   jax.ShapeDtypeStruct((8, t), jnp.int32), jax.ShapeDtypeStruct((8, t), F32),
                   jax.ShapeDtypeStruct((ne, 128), F32)),
        grid=(t // tm,),
        in_specs=[pl.BlockSpec((tm, d), lambda i: (i, 0)), vec, vec, vec,
                  pl.BlockSpec((ne, d), lambda i: (0, 0)), pl.BlockSpec((ne, 1), lambda i: (0, 0))],
        out_specs=(pl.BlockSpec((tm, d), lambda i: (i, 0)), lane, lane, lane,
                   pl.BlockSpec((ne, 128), lambda i: (0, 0))),
        scratch_shapes=[pltpu.VMEM((ne, 128), F32)],
        compiler_params=_cparams(("arbitrary",)),
        name="moe_router",
    )(x, g, sc, sh, router_w.T, router_bias.reshape(ne, 1))


def _plan_kernel(eidx_ref, rank_ref, ps_ref, dest_ref):
    ne = N_EXPERTS
    tm = eidx_ref.shape[1]
    io = lax.broadcasted_iota(jnp.int32, (ne, tm), 0)
    ps = ps_ref[...]
    dest_ref[...] = jnp.zeros_like(dest_ref)
    for j in range(TOP_K):
        hit = io == eidx_ref[j:j + 1, :]
        base = jnp.sum(jnp.where(hit, ps, 0.0), axis=0, keepdims=True)
        dest_ref[j:j + 1, :] = base.astype(jnp.int32) + rank_ref[j:j + 1, :]


def moe_plan(eidx, rank, pad_start, tm=2048):
    t = eidx.shape[1]
    tm = min(tm, t)
    lane = pl.BlockSpec((8, tm), lambda i: (0, i))
    return pl.pallas_call(
        _plan_kernel,
        out_shape=jax.ShapeDtypeStruct((8, t), jnp.int32),
        grid=(t // tm,),
        in_specs=[lane, lane, pl.BlockSpec((N_EXPERTS, 1), lambda i: (0, 0))],
        out_specs=lane,
        compiler_params=_cparams(("parallel",)),
        name="moe_plan",
    )(eidx, rank, pad_start.astype(F32).reshape(N_EXPERTS, 1))


SUBLANES = 8
_PAD_PIECES = tuple(1 << k for k in reversed(range(3, MOE_BLOCK.bit_length() - 1)))


def _dispatch_kernel(dest_ref, zs_ref, zn_ref, hf_ref, xb_ref, zbuf, sem, zsem, *, t_total):
    tm = hf_ref.shape[0]
    base = pl.program_id(0) * tm

    def row_copy(t, j):
        d = dest_ref[j * t_total + base + t]
        return pltpu.make_async_copy(hf_ref.at[pl.ds(t, 1)], xb_ref.at[pl.ds(d, 1)], sem)

    def start(t, carry):
        for j in range(TOP_K):
            row_copy(t, j).start()
        return carry

    def wait_all():
        for _ in range(TOP_K):
            pltpu.make_async_copy(hf_ref, xb_ref.at[pl.ds(0, tm)], sem).wait()

    def fill(do_start):
        def body(e, carry):
            zs, zn = zs_ref[e], zn_ref[e]
            end = zs + zn

            def piece(src, dst):
                cp = pltpu.make_async_copy(src, dst, zsem)
                cp.start() if do_start else cp.wait()

            for r in range(SUBLANES - 1):
                @pl.when(r < (zn & (SUBLANES - 1)))
                def _():
                    piece(zbuf.at[pl.ds(0, 1)], xb_ref.at[pl.ds(zs + r, 1)])
            for p in _PAD_PIECES:
                @pl.when((zn & p) != 0)
                def _():
                    q = pl.multiple_of(end - (zn & ~(p - 1)), SUBLANES)
                    piece(zbuf.at[pl.ds(0, p)], xb_ref.at[pl.ds(q, p)])
            return carry
        lax.fori_loop(0, N_EXPERTS, body, 0)

    first = pl.program_id(0) == 0

    @pl.when(first)
    def _():
        zbuf[...] = jnp.zeros_like(zbuf)
        fill(True)

    lax.fori_loop(0, tm, start, 0)

    @pl.when(first)
    def _():
        fill(False)

    wait_all()


def moe_dispatch(dest, zero_start, zero_len, hf, n_slots, tm=256):
    t, d = hf.shape
    return pl.pallas_call(
        functools.partial(_dispatch_kernel, t_total=t),
        out_shape=jax.ShapeDtypeStruct((n_slots, d), hf.dtype),
        grid_spec=pltpu.PrefetchScalarGridSpec(
            num_scalar_prefetch=3,
            grid=(t // tm,),
            in_specs=[pl.BlockSpec((tm, d), lambda i, *_: (i, 0))],
            out_specs=pl.BlockSpec(memory_space=pl.ANY),
            scratch_shapes=[pltpu.VMEM((MOE_BLOCK // 2, d), hf.dtype),
                            pltpu.SemaphoreType.DMA(()), pltpu.SemaphoreType.DMA(())]),
        compiler_params=_cparams(("arbitrary",)),
        name="moe_dispatch",
    )(dest, zero_start, zero_len, hf)


def _ffn(x, wg, wu, wd):
    hg = jnp.dot(x, wg, preferred_element_type=F32)
    hu = jnp.dot(x, wu, preferred_element_type=F32)
    h = (hg * jax.nn.sigmoid(hg) * hu).astype(BF16)
    return jnp.dot(h, wd, preferred_element_type=F32)


def _expert_kernel(be_ref, na_ref, first_ref, next_ref, slot_ref, x_ref, wg_hbm, wu_hbm, wd_hbm, o_ref,
                   wgf, wuf, wdf, wgb, wub, wdb, sems, *, layer):
    b = pl.program_id(0)
    active = b < na_ref[0]

    def fetch(e, s):
        return (pltpu.make_async_copy(wg_hbm.at[layer, e], wgf.at[s], sems.at[s, 0]),
                pltpu.make_async_copy(wu_hbm.at[layer, e], wuf.at[s], sems.at[s, 1]),
                pltpu.make_async_copy(wd_hbm.at[layer, e], wdf.at[s], sems.at[s, 2]))

    @pl.when(jnp.logical_and(active, first_ref[b] == 1))
    def _():
        s = slot_ref[b]

        @pl.when(b == 0)
        def _():
            for cp in fetch(be_ref[b], s):
                cp.start()

        for cp in fetch(be_ref[b], s):
            cp.wait()
        nxt = next_ref[b]

        @pl.when(nxt >= 0)
        def _():
            for cp in fetch(nxt, 1 - s):
                cp.start()

        wgb[...] = wgf[s].astype(BF16)
        wub[...] = wuf[s].astype(BF16)
        wdb[...] = wdf[s].astype(BF16)

    @pl.when(active)
    def _():
        o_ref[...] = _ffn(x_ref[...].astype(BF16), wgb[...], wub[...], wdb[...])


def moe_experts(blk_e, n_active, first, nxt, slot, xb, w_gate, w_up, w_down, layer):
    p, d = xb.shape
    ff = w_gate.shape[3]
    nblk = p // MOE_BLOCK
    row = lambda b, be, na, *_: (jnp.minimum(b, na[0] - 1), 0)
    hbm = pl.BlockSpec(memory_space=pl.ANY)
    return pl.pallas_call(
        functools.partial(_expert_kernel, layer=layer),
        out_shape=jax.ShapeDtypeStruct((p, d), F32),
        grid_spec=pltpu.PrefetchScalarGridSpec(
            num_scalar_prefetch=5,
            grid=(nblk,),
            in_specs=[pl.BlockSpec((MOE_BLOCK, d), row), hbm, hbm, hbm],
            out_specs=pl.BlockSpec((MOE_BLOCK, d), row),
            scratch_shapes=[pltpu.VMEM((2, d, ff), F32), pltpu.VMEM((2, d, ff), F32),
                            pltpu.VMEM((2, ff, d), F32),
                            pltpu.VMEM((d, ff), BF16), pltpu.VMEM((d, ff), BF16),
                            pltpu.VMEM((ff, d), BF16), pltpu.SemaphoreType.DMA((2, 3))]),
        compiler_params=_cparams(("arbitrary",)),
        name="moe_experts",
    )(blk_e, n_active, first, nxt, slot, xb, w_gate, w_up, w_down)


def _combine_kernel(dest_ref, yb_ref, x_ref, hf_ref, wk_ref, gate_ref, wg_ref, wu_ref, wd_ref, fg_ref, o_ref,
                    buf, wgb, wub, wdb, sem, *, t_total, final_norm):
    tm, d = x_ref.shape
    base = pl.program_id(0) * tm

    @pl.when(pl.program_id(0) == 0)
    def _():
        wgb[...] = wg_ref[...].astype(BF16)
        wub[...] = wu_ref[...].astype(BF16)
        wdb[...] = wd_ref[...].astype(BF16)

    def row_copy(t, j):
        dst = dest_ref[j * t_total + base + t]
        return pltpu.make_async_copy(yb_ref.at[pl.ds(dst, 1)], buf.at[j, pl.ds(t, 1)], sem)

    def start(t, carry):
        for j in range(TOP_K):
            row_copy(t, j).start()
        return carry

    lax.fori_loop(0, tm, start, 0)
    acc = _ffn(hf_ref[...].astype(BF16), wgb[...], wub[...], wdb[...])
    for j in range(TOP_K):
        pltpu.make_async_copy(yb_ref.at[pl.ds(0, tm)], buf.at[j], sem).wait()
    for j in range(TOP_K):
        acc = acc + wk_ref[:, j:j + 1] * buf[j]
    y = x_ref[...] + gate_ref[...] * acc
    if final_norm:
        y = (y * lax.rsqrt(jnp.mean(y * y, axis=-1, keepdims=True) + EPS)) * fg_ref[...]
    o_ref[...] = y


def moe_combine(dest, yb, x, hf, wk_t, gate, sh_gate, sh_up, sh_down, final_g, final_norm, tm=128):
    t, d = x.shape
    ff = sh_gate.shape[1]
    tm = min(tm, t)
    tile = lambda i, dest: (i, 0)
    full = lambda shape: pl.BlockSpec(shape, lambda i, dest: (0, 0))
    return pl.pallas_call(
        functools.partial(_combine_kernel, t_total=t, final_norm=final_norm),
        out_shape=jax.ShapeDtypeStruct((t, d), F32),
        grid_spec=pltpu.PrefetchScalarGridSpec(
            num_scalar_prefetch=1,
            grid=(t // tm,),
            in_specs=[pl.BlockSpec(memory_space=pl.ANY),
                      pl.BlockSpec((tm, d), tile), pl.BlockSpec((tm, d), tile),
                      pl.BlockSpec((tm, 8), tile), full((1, d)),
                      full((d, ff)), full((d, ff)), full((ff, d)), full((1, d))],
            out_specs=pl.BlockSpec((tm, d), tile),
            scratch_shapes=[pltpu.VMEM((TOP_K, tm, d), F32),
                            pltpu.VMEM((d, ff), BF16), pltpu.VMEM((d, ff), BF16), pltpu.VMEM((ff, d), BF16),
                            pltpu.SemaphoreType.DMA(())]),
        compiler_params=_cparams(("arbitrary",)),
        name="moe_combine",
    )(dest, yb, x, hf, wk_t, gate, sh_gate, sh_up, sh_down, final_g)


def moe_layer(x, g, sc, sh, gate, router_w, router_bias, w_gate, w_up, w_down, sh_gate, sh_up, sh_down,
              layer, final_g, final_norm):
    t, d = x.shape
    ne = N_EXPERTS
    hf, eidx, rank, wk, cnt = moe_router(x, g, sc, sh, router_w, router_bias)
    counts = cnt[:, 0].astype(jnp.int32)
    padded = (counts + MOE_BLOCK - 1) // MOE_BLOCK * MOE_BLOCK
    pad_end = jnp.cumsum(padded)
    pad_start = pad_end - padded
    n_slots = (t * TOP_K + ne * (MOE_BLOCK - 1) + MOE_BLOCK - 1) // MOE_BLOCK * MOE_BLOCK
    nblk = n_slots // MOE_BLOCK
    dest = moe_plan(eidx, rank, pad_start)[:TOP_K].reshape(TOP_K * t)
    blk_start = jnp.arange(nblk, dtype=jnp.int32) * MOE_BLOCK
    n_active = (pad_end[-1] // MOE_BLOCK).astype(jnp.int32).reshape(1)
    blk_e = jnp.minimum(jnp.sum(blk_start[:, None] >= pad_end[None, :], axis=1), ne - 1).astype(jnp.int32)
    last_e = jnp.max(jnp.where(counts > 0, jnp.arange(ne, dtype=jnp.int32), 0))
    live = blk_start < pad_end[-1]
    blk_e = jnp.where(live, blk_e, last_e)
    first = jnp.logical_and(live, jnp.concatenate([jnp.ones((1,), bool), blk_e[1:] != blk_e[:-1]]))
    slot = ((jnp.cumsum(first.astype(jnp.int32)) - 1) % 2).astype(jnp.int32)
    ids = jnp.arange(ne, dtype=jnp.int32)
    later = jnp.logical_and(ids[None, :] > ids[:, None], counts[None, :] > 0)
    next_e = jnp.min(jnp.where(later, ids[None, :], ne), axis=1)
    next_e = jnp.where(next_e < ne, next_e, -1)
    nxt = jnp.sum(jnp.where(blk_e[:, None] == ids[None, :], next_e[None, :], 0), axis=1).astype(jnp.int32)

    xb = moe_dispatch(dest, pad_start + counts, padded - counts, hf, n_slots)
    yb = moe_experts(blk_e, n_active, first.astype(jnp.int32), nxt, slot, xb, w_gate, w_up, w_down, layer)
    return moe_combine(dest, yb, x, hf, wk.T, gate, sh_gate, sh_up, sh_down, final_g, final_norm)


def kernel(x, c, ada_w, ada_b, norm_mix_g, norm_ffn_g, hy_w_in, hy_conv_w, hy_conv_b, hy_f_w0, hy_f_b0, hy_f_w1, hy_f_b1, hy_f_w2, hy_f_b2, hy_f_freq, hy_f_wout, hy_skip, hy_w_out, ml_w_in, ml_conv_w, ml_conv_b, ml_gate_b, ml_head_g, ml_w_out, moe_router_w, moe_router_bias, moe_w_gate, moe_w_up, moe_w_down, sh_w_gate, sh_w_up, sh_w_down, final_g):
    bsz, seq, d = x.shape
    assert bsz == 1, "kernels are written for a single sequence"
    depth = ada_w.shape[0]
    xs = x.reshape(seq, d)
    for layer in range(depth):
        ada = ada_proj(c, ada_w, ada_b, layer)
        sh_m, sc_m, g_m, sh_f, sc_f, g_f = (ada[:, k * d:(k + 1) * d] for k in range(6))
        gm = norm_mix_g[layer].reshape(1, d)
        j = layer // 2
        if layer % 2 == 0:
            hm = normmod_slab(xs, gm, sc_m, sh_m)
            xs = hyena_mixer(hm, xs, g_m, hy_w_in, j, hy_conv_w[j], hy_conv_b[j], hy_f_w0[j], hy_f_b0[j],
                             hy_f_w1[j], hy_f_b1[j], hy_f_w2[j], hy_f_b2[j], hy_f_freq[j], hy_f_wout[j],
                             hy_skip[j], hy_w_out[j])
        else:
            hm = normmod(xs, gm, sc_m, sh_m)
            xs = mlstm_mixer(hm, xs, g_m, ml_w_in, j, ml_conv_w[j], ml_conv_b[j], ml_gate_b[j],
                             ml_head_g[j], ml_w_out[j])
        xs = moe_layer(xs, norm_ffn_g[layer].reshape(1, d), sc_f, sh_f, g_f, moe_router_w[layer],
                       moe_router_bias[layer], moe_w_gate, moe_w_up, moe_w_down,
                       sh_w_gate[layer], sh_w_up[layer], sh_w_down[layer], layer,
                       final_g.reshape(1, d), layer == depth - 1)
    return xs.reshape(bsz, seq, d)
```
